```python
import math
import jax, jax.numpy as jnp
from jax import lax
import numpy as np

D_MODEL = 1024
BATCH = 16
SEQ = 2048
DEPTH = 2

MLA_HEADS = 4
MLA_Q_RANK = 256
MLA_KV_RANK = 128
MLA_NOPE = 128
MLA_ROPE = 64
MLA_V = 128
ROPE_THETA = 10000.0
ATTN_BLOCK = 128
MASK_VALUE = -1e30
GDN_HEADS = 4
GDN_DK = 128
GDN_DV = 128
GDN_CONV = 4
GDN_CHUNK = 64
POOL_WINDOWS = (2, 4, 8, 16)
POOL_GROUP = 128
HGRN_HEADS = 4
HGRN_DK = 128
HGRN_DV = 128
HGRN_CHUNK = 32
N_BRANCH = 4
BRANCH_WIDTH = 512
D_FF = 2816
N_EXPERTS = 8
TOP_K = 2
D_FF_EXPERT = 3584
MOE_BLOCK = 512
N_DENSE = (DEPTH + 1) // 2
N_MOE = DEPTH // 2
DEEPNORM_ALPHA = (2 * DEPTH) ** 0.25
DEEPNORM_BETA = (8 * DEPTH) ** -0.25
LN_EPS = 1e-5
RMS_EPS = 1e-6

SPLIT_SIZES = (MLA_Q_RANK, MLA_KV_RANK, MLA_ROPE,
               GDN_HEADS * GDN_DK, GDN_HEADS * GDN_DK, GDN_HEADS * GDN_DV, GDN_HEADS * GDN_DV, GDN_HEADS, GDN_HEADS,
               len(POOL_WINDOWS) * POOL_GROUP,
               HGRN_HEADS * HGRN_DK, HGRN_HEADS * HGRN_DK, HGRN_HEADS * HGRN_DV, HGRN_HEADS * HGRN_DV,
               N_BRANCH * D_MODEL)
D_IN = sum(SPLIT_SIZES)

kernel_name = 'hybrid_gated_mla_gdn_pool_hgrn2_moe'


def layer_norm(x, g, b):
    xf = x.astype(jnp.float32)
    mu = jnp.mean(xf, -1, keepdims=True)
    var = jnp.mean(jnp.square(xf - mu), -1, keepdims=True)
    y = (xf - mu) * lax.rsqrt(var + LN_EPS) * g.astype(jnp.float32) + b.astype(jnp.float32)
    return y.astype(x.dtype)


def rms_norm(x, g):
    xf = x.astype(jnp.float32)
    y = xf * lax.rsqrt(jnp.mean(xf * xf, -1, keepdims=True) + RMS_EPS) * g.astype(jnp.float32)
    return y.astype(x.dtype)


def l2_norm(x):
    xf = x.astype(jnp.float32)
    return xf * lax.rsqrt(jnp.sum(xf * xf, -1, keepdims=True) + RMS_EPS)


def masked_exp(mask, diff):
    return jnp.where(mask, jnp.exp(jnp.where(mask, diff, 0.0)), 0.0)


def rope(x, pos):
    half = x.shape[-1] // 2
    inv = ROPE_THETA ** (-jnp.arange(half, dtype=jnp.float32) / half)
    ang = pos.astype(jnp.float32)[..., None] * inv
    ang = ang.reshape(ang.shape[:2] + (1,) * (x.ndim - 3) + (half,))
    cos, sin = jnp.cos(ang), jnp.sin(ang)
    xf = x.astype(jnp.float32)
    x1, x2 = xf[..., :half], xf[..., half:]
    return jnp.concatenate([x1 * cos - x2 * sin, x2 * cos + x1 * sin], -1).astype(x.dtype)


def swiglu(x, wg, wu, wd):
    return (jax.nn.silu(x @ wg) * (x @ wu)) @ wd


def mla_mixer(cq, ckv, kr, pos, g_q, w_uq, g_kv, w_ukv):
    B, S, _ = cq.shape
    H = MLA_HEADS
    q = (rms_norm(cq, g_q) @ w_uq).reshape(B, S, H, MLA_NOPE + MLA_ROPE)
    q = jnp.concatenate([q[..., :MLA_NOPE], rope(q[..., MLA_NOPE:], pos)], -1)
    kv = (rms_norm(ckv, g_kv) @ w_ukv).reshape(B, S, H, MLA_NOPE + MLA_V)
    k_nope, v = kv[..., :MLA_NOPE], kv[..., MLA_NOPE:]
    k_rope = rope(kr, pos)
    k = jnp.concatenate([k_nope, jnp.broadcast_to(k_rope[:, :, None, :], (B, S, H, MLA_ROPE))], -1)
    scale = (MLA_NOPE + MLA_ROPE) ** -0.5
    outs = []
    for start in range(0, S, ATTN_BLOCK):
        end = start + ATTN_BLOCK
        s = jnp.einsum('bqhd,bkhd->bhqk', q[:, start:end], k[:, :end]).astype(jnp.float32) * scale
        mask = (start + jnp.arange(ATTN_BLOCK))[:, None] >= jnp.arange(end)[None, :]
        p = jax.nn.softmax(jnp.where(mask, s, MASK_VALUE), axis=-1).astype(v.dtype)
        outs.append(jnp.einsum('bhqk,bkhd->bqhd', p, v[:, :end]))
    return jnp.concatenate(outs, 1).reshape(B, S, H * MLA_V)


def causal_dwconv(x, w):
    K, C = w.shape
    return lax.conv_general_dilated(x, w[:, None, :].astype(x.dtype), window_strides=(1,),
                                    padding=[(K - 1, 0)], dimension_numbers=('NWC', 'WIO', 'NWC'),
                                    feature_group_count=C)


def gated_delta_rule(q, k, v, g, beta):
    B, H, S, DK = q.shape
    DV = v.shape[-1]
    C = GDN_CHUNK
    N = S // C
    q = q * DK ** -0.5
    q, k = q.reshape(B, H, N, C, DK), k.reshape(B, H, N, C, DK)
    v = v.reshape(B, H, N, C, DV)
    g, beta = g.reshape(B, H, N, C), beta.reshape(B, H, N, C)
    kb, vb = k * beta[..., None], v * beta[..., None]
    gc = jnp.cumsum(g, -1)
    tri = jnp.tril(jnp.ones((C, C), bool))
    stri = jnp.tril(jnp.ones((C, C), bool), -1)
    decay = masked_exp(tri, gc[..., :, None] - gc[..., None, :])
    a = jnp.where(stri, jnp.einsum('bhnid,bhnjd->bhnij', kb, k) * decay, 0.0)
    eye = jnp.eye(C, dtype=jnp.float32)
    t_inv = lax.linalg.triangular_solve(eye + a, jnp.broadcast_to(eye, a.shape), left_side=True,
                                        lower=True, unit_diagonal=True)
    u = t_inv @ vb
    w = t_inv @ (kb * jnp.exp(gc)[..., None])
    qk = jnp.einsum('bhnid,bhnjd->bhnij', q, k) * decay

    def step(state, xs):
        q_c, k_c, u_c, w_c, qk_c, gc_c = xs
        v_new = u_c - w_c @ state
        o = (q_c * jnp.exp(gc_c)[..., None]) @ state + qk_c @ v_new
        g_last = gc_c[..., -1]
        state = state * jnp.exp(g_last)[..., None, None] + jnp.einsum(
            'bhcd,bhcv->bhdv', k_c * jnp.exp(g_last[..., None] - gc_c)[..., None], v_new)
        return state, o

    xs = tuple(jnp.moveaxis(t, 2, 0) for t in (q, k, u, w, qk, gc))
    _, o = lax.scan(step, jnp.zeros((B, H, DK, DV), jnp.float32), xs)
    return jnp.moveaxis(o, 0, 2).reshape(B, H, S, DV)


def gdn_mixer(q, k, v, z, a, b, conv_w, a_log, dt_bias, norm_g):
    B, S, _ = q.shape
    H = GDN_HEADS
    qkv = jax.nn.silu(causal_dwconv(jnp.concatenate([q, k, v], -1), conv_w))
    q, k, v = jnp.split(qkv, [H * GDN_DK, 2 * H * GDN_DK], axis=-1)
    qh = l2_norm(q.reshape(B, S, H, GDN_DK)).transpose(0, 2, 1, 3)
    kh = l2_norm(k.reshape(B, S, H, GDN_DK)).transpose(0, 2, 1, 3)
    vh = v.astype(jnp.float32).reshape(B, S, H, GDN_DV).transpose(0, 2, 1, 3)
    beta = jax.nn.sigmoid(b.astype(jnp.float32)).transpose(0, 2, 1)
    g = -jnp.exp(a_log.astype(jnp.float32)) * jax.nn.softplus(a.astype(jnp.float32) + dt_bias.astype(jnp.float32))
    o = gated_delta_rule(qh, kh, vh, g.transpose(0, 2, 1), beta).transpose(0, 2, 1, 3)
    o = rms_norm(o, norm_g) * jax.nn.silu(z.astype(jnp.float32).reshape(B, S, H, GDN_DV))
    return o.reshape(B, S, H * GDN_DV).astype(q.dtype)


def pool_mixer(u, w_group, scale):
    B, S, _ = u.shape
    ug = u.astype(jnp.float32).reshape(B, S, len(POOL_WINDOWS), POOL_GROUP)
    cs = jnp.cumsum(ug, axis=1)
    t = jnp.arange(S)
    outs = []
    for gi, win in enumerate(POOL_WINDOWS):
        c = cs[:, :, gi]
        lag = jnp.pad(c, ((0, 0), (win, 0), (0, 0)))[:, :S]
        cnt = jnp.minimum(t + 1, win).astype(jnp.float32)[None, :, None]
        outs.append((c - lag) / cnt - ug[:, :, gi])
    pooled = jnp.stack(outs, 2)
    y = jnp.einsum('bsgc,gcd->bsgd', pooled, w_group.astype(jnp.float32))
    return (y.reshape(B, S, -1) * scale.astype(jnp.float32)).astype(u.dtype)


def gla_chunk_scan(q, k, v, log_f):
    B, H, S, DK = q.shape
    DV = v.shape[-1]
    C = HGRN_CHUNK
    N = S // C
    b = jnp.cumsum(log_f.reshape(B, H, N, C, DK), axis=3)
    tri = jnp.tril(jnp.ones((C, C), bool))[:, :, None]

    def step(state, xs):
        q_c, k_c, v_c, b_c = xs
        dec = masked_exp(tri, b_c[:, :, :, None, :] - b_c[:, :, None, :, :])
        att = jnp.einsum('bhid,bhjd,bhijd->bhij', q_c, k_c, dec)
        o = att @ v_c + jnp.einsum('bhid,bhdv->bhiv', q_c * jnp.exp(b_c), state)
        b_last = b_c[:, :, -1]
        state = state * jnp.exp(b_last)[..., None] + jnp.einsum(
            'bhjd,bhjv->bhdv', k_c * jnp.exp(b_last[:, :, None, :] - b_c), v_c)
        return state, o

    rs = lambda t: jnp.moveaxis(t.reshape(B, H, N, C, t.shape[-1]), 2, 0)
    xs = (rs(q), rs(k), rs(v), jnp.moveaxis(b, 2, 0))
    _, o = lax.scan(step, jnp.zeros((B, H, DK, DV), jnp.float32), xs)
    return jnp.moveaxis(o, 0, 2).reshape(B, H, S, DV)


def hgrn2_mixer(q, f_pre, i, g, lb, norm_g):
    B, S, _ = q.shape
    H = HGRN_HEADS
    heads = lambda t, d: t.astype(jnp.float32).reshape(B, S, H, d).transpose(0, 2, 1, 3)
    qh = heads(jax.nn.silu(q), HGRN_DK)
    fp = heads(f_pre, HGRN_DK)
    lbh = lb.astype(jnp.float32).reshape(H, 1, HGRN_DK)
    log_f = jnp.log(lbh + (1.0 - lbh) * jax.nn.sigmoid(fp))
    kh = (1.0 - lbh) * jax.nn.sigmoid(-fp)
    vh = heads(i, HGRN_DV)
    o = gla_chunk_scan(qh, kh, vh, log_f).transpose(0, 2, 1, 3)
    o = rms_norm(o, norm_g) * jax.nn.sigmoid(g.astype(jnp.float32).reshape(B, S, H, HGRN_DV))
    return o.reshape(B, S, H * HGRN_DV).astype(q.dtype)


def moe_swiglu(x, w_router, w_gate, w_up, w_down):
    B, S, D = x.shape
    T = B * S
    xt = x.reshape(T, D)
    logits = (xt @ w_router).astype(jnp.float32)
    top_val, top_idx = lax.top_k(logits, TOP_K)
    gate = jax.nn.softmax(top_val, axis=-1)
    e_flat = top_idx.reshape(-1)
    tok_flat = jnp.repeat(jnp.arange(T, dtype=jnp.int32), TOP_K)
    w_flat = gate.reshape(-1)
    order = jnp.argsort(e_flat)
    e_s, tok_s, w_s = e_flat[order], tok_flat[order], w_flat[order]
    counts = jnp.zeros((N_EXPERTS,), jnp.int32).at[e_flat].add(1)
    starts = jnp.cumsum(counts) - counts
    padded = (counts + MOE_BLOCK - 1) // MOE_BLOCK * MOE_BLOCK
    pends = jnp.cumsum(padded)
    pstarts = pends - padded
    dest = pstarts[e_s] + (jnp.arange(T * TOP_K, dtype=jnp.int32) - starts[e_s])
    n_rows = ((T * TOP_K + MOE_BLOCK - 1) // MOE_BLOCK + N_EXPERTS) * MOE_BLOCK
    row_tok = jnp.full((n_rows,), T, jnp.int32).at[dest].set(tok_s)
    row_w = jnp.zeros((n_rows,), jnp.float32).at[dest].set(w_s)
    n_blk = n_rows // MOE_BLOCK
    blk_exp = jnp.minimum(jnp.searchsorted(pends, jnp.arange(n_blk, dtype=jnp.int32) * MOE_BLOCK, side='right'),
                          N_EXPERTS - 1)
    x_pad = jnp.concatenate([xt, jnp.zeros((1, D), xt.dtype)], 0)
    xb = x_pad[row_tok].reshape(n_blk, MOE_BLOCK, D)
    yb = lax.map(lambda a: swiglu(a[0], w_gate[a[1]], w_up[a[1]], w_down[a[1]]), (xb, blk_exp))
    yb = yb.reshape(n_rows, D) * row_w[:, None].astype(yb.dtype)
    y = jax.ops.segment_sum(yb, row_tok, num_segments=T + 1)[:T]
    return y.reshape(B, S, D)


def setup_inputs(seed: int = 0) -> dict:
    key = jax.random.key(seed)
    ks = iter(list(jax.random.split(key, 40)))
    f32 = jnp.float32
    L = DEPTH

    def nrm(shape, scale):
        return jax.random.normal(next(ks), shape, f32) * scale

    def gain(shape):
        return 1.0 + 0.02 * jax.random.normal(next(ks), shape, f32)

    x = jax.random.normal(next(ks), (BATCH, SEQ, D_MODEL), f32)
    positions = jnp.broadcast_to(jnp.arange(SEQ, dtype=jnp.int32), (BATCH, SEQ))
    dt = jnp.exp(jax.random.uniform(next(ks), (L, GDN_HEADS), f32, math.log(1e-3), math.log(1e-1)))
    return {
        'x': x,
        'positions': positions,
        'w_in': nrm((L, D_MODEL, D_IN), D_MODEL ** -0.5),
        'mla_q_norm': gain((L, MLA_Q_RANK)),
        'mla_w_uq': nrm((L, MLA_Q_RANK, MLA_HEADS * (MLA_NOPE + MLA_ROPE)), MLA_Q_RANK ** -0.5),
        'mla_kv_norm': gain((L, MLA_KV_RANK)),
        'mla_w_ukv': nrm((L, MLA_KV_RANK, MLA_HEADS * (MLA_NOPE + MLA_V)), MLA_KV_RANK ** -0.5),
        'gdn_conv': nrm((L, GDN_CONV, GDN_HEADS * (2 * GDN_DK + GDN_DV)), GDN_CONV ** -0.5),
        'gdn_a_log': jnp.log(jax.random.uniform(next(ks), (L, GDN_HEADS), f32, 1.0, 16.0)),
        'gdn_dt_bias': dt + jnp.log(-jnp.expm1(-dt)),
        'gdn_norm': gain((L, GDN_DV)),
        'pool_w': nrm((L, len(POOL_WINDOWS), POOL_GROUP, POOL_GROUP), POOL_GROUP ** -0.5),
        'pool_scale': gain((L, len(POOL_WINDOWS) * POOL_GROUP)),
        'hgrn_lb_logits': nrm((L, HGRN_HEADS * HGRN_DK), 0.5),
        'hgrn_norm': gain((L, HGRN_DV)),
        'w_branch': nrm((L, N_BRANCH, BRANCH_WIDTH, D_MODEL), BRANCH_WIDTH ** -0.5 * DEEPNORM_BETA),
        'w_out': nrm((L, D_MODEL, D_MODEL), D_MODEL ** -0.5 * DEEPNORM_BETA),
        'ln_mix_g': gain((L, D_MODEL)),
        'ln_mix_b': nrm((L, D_MODEL), 0.02),
        'ffn_w_gate': nrm((N_DENSE, D_MODEL, D_FF), D_MODEL ** -0.5),
        'ffn_w_up': nrm((N_DENSE, D_MODEL, D_FF), D_MODEL ** -0.5),
        'ffn_w_down': nrm((N_DENSE, D_FF, D_MODEL), D_FF ** -0.5 * DEEPNORM_BETA),
        'moe_router': nrm((N_MOE, D_MODEL, N_EXPERTS), D_MODEL ** -0.5),
        'moe_w_gate': nrm((N_MOE, N_EXPERTS, D_MODEL, D_FF_EXPERT), D_MODEL ** -0.5),
        'moe_w_up': nrm((N_MOE, N_EXPERTS, D_MODEL, D_FF_EXPERT), D_MODEL ** -0.5),
        'moe_w_down': nrm((N_MOE, N_EXPERTS, D_FF_EXPERT, D_MODEL), D_FF_EXPERT ** -0.5 * DEEPNORM_BETA),
        'ln_ffn_g': gain((L, D_MODEL)),
        'ln_ffn_b': nrm((L, D_MODEL), 0.02),
    }


def reference(x, positions, w_in, mla_q_norm, mla_w_uq, mla_kv_norm, mla_w_ukv, gdn_conv, gdn_a_log,
              gdn_dt_bias, gdn_norm, pool_w, pool_scale, hgrn_lb_logits, hgrn_norm, w_branch, w_out,
              ln_mix_g, ln_mix_b, ffn_w_gate, ffn_w_up, ffn_w_down, moe_router, moe_w_gate, moe_w_up,
              moe_w_down, ln_ffn_g, ln_ffn_b):
    B, S, D = x.shape
    cuts = []
    acc = 0
    for size in SPLIT_SIZES[:-1]:
        acc += size
        cuts.append(acc)
    p_lb = jax.nn.softmax(hgrn_lb_logits.astype(jnp.float32), axis=0)
    lower_bounds = jnp.cumsum(p_lb, axis=0) - p_lb[0]
    for l in range(DEPTH):
        proj = x @ w_in[l]
        (cq, ckv, kr, gq, gk, gv, gz, ga, gb, pu, hq, hf, hi, hg, gate_pre) = jnp.split(proj, cuts, axis=-1)
        y_a = mla_mixer(cq, ckv, kr, positions, mla_q_norm[l], mla_w_uq[l], mla_kv_norm[l], mla_w_ukv[l])
        y_b = gdn_mixer(gq, gk, gv, gz, ga, gb, gdn_conv[l], gdn_a_log[l], gdn_dt_bias[l], gdn_norm[l])
        y_c = pool_mixer(pu, pool_w[l], pool_scale[l])
        y_d = hgrn2_mixer(hq, hf, hi, hg, lower_bounds[l], hgrn_norm[l])
        gates = jax.nn.sigmoid(gate_pre.reshape(B, S, N_BRANCH, D))
        merged = None
        for m, y in enumerate((y_a, y_b, y_c, y_d)):
            term = gates[:, :, m] * (y @ w_branch[l, m])
            merged = term if merged is None else merged + term
        x = layer_norm(DEEPNORM_ALPHA * x + merged @ w_out[l], ln_mix_g[l], ln_mix_b[l])
        j = l // 2
        if l % 2 == 0:
            ffn = swiglu(x, ffn_w_gate[j], ffn_w_up[j], ffn_w_down[j])
        else:
            ffn = moe_swiglu(x, moe_router[j], moe_w_gate[j], moe_w_up[j], moe_w_down[j])
        x = layer_norm(DEEPNORM_ALPHA * x + ffn, ln_ffn_g[l], ln_ffn_b[l])
    return x
```

```python
import functools
import math

import jax
import jax.numpy as jnp
from jax import lax
from jax.experimental import pallas as pl
from jax.experimental.pallas import tpu as pltpu

F32 = jnp.float32
BF16 = jnp.bfloat16

D_MODEL = 1024
DEPTH = 2
MLA_HEADS = 4
MLA_Q_RANK = 256
MLA_KV_RANK = 128
MLA_NOPE = 128
MLA_ROPE = 64
MLA_V = 128
ROPE_THETA = 10000.0
MASK_VALUE = -1e30
GDN_HEADS = 4
GDN_DK = 128
GDN_DV = 128
GDN_CONV = 4
GDN_CHUNK = 64
POOL_WINDOWS = (2, 4, 8, 16)
POOL_GROUP = 128
HGRN_HEADS = 4
HGRN_DK = 128
HGRN_DV = 128
N_BRANCH = 4
BRANCH_WIDTH = 512
D_FF = 2816
N_EXPERTS = 8
TOP_K = 2
D_FF_EXPERT = 3584
DEEPNORM_ALPHA = (2 * DEPTH) ** 0.25
LN_EPS = 1e-5
RMS_EPS = 1e-6

LANES = 128
MLA_HEAD_PAD = 256
SPAN = 256
MOE_ROWS = 512
VMEM_LIMIT = 56 * 1024 * 1024

PK_GQ, PK_GK, PK_GV, PK_GZ, PK_PU, PK_HQ, PK_HF, PK_HI, PK_HG = range(9)
PK_CQ = 18
PK_CKV = 38
PK_KRA = 39
PK_KRB = 40
PK_AB = 41
PK_WIDTH = 42 * LANES


def _params(sem, vmem=VMEM_LIMIT):
    return pltpu.CompilerParams(dimension_semantics=sem, vmem_limit_bytes=vmem)


def _dot(a, b):
    return jnp.dot(a, b, preferred_element_type=F32)


def _dot_nt(a, b):
    return lax.dot_general(a, b, (((1,), (1,)), ((), ())), preferred_element_type=F32)


def _dot_tn(a, b):
    return lax.dot_general(a, b, (((0,), (0,)), ((), ())), preferred_element_type=F32)


def _split3(x):
    hi = x.astype(BF16)
    r1 = x - hi.astype(F32)
    mid = r1.astype(BF16)
    lo = (r1 - mid.astype(F32)).astype(BF16)
    return hi, mid, lo


def _exact_left_dot(m01, x):
    hi, mid, lo = _split3(x)
    return _dot(m01, hi) + _dot(m01, mid) + _dot(m01, lo)


def _sigmoid(x):
    return 1.0 / (1.0 + jnp.exp(-x))


def _silu(x):
    return x * _sigmoid(x)


def _softplus(x):
    return jnp.maximum(x, 0.0) + jnp.log(1.0 + jnp.exp(-jnp.abs(x)))


def _layer_norm(x, g, b):
    mu = jnp.mean(x, -1, keepdims=True)
    xc = x - mu
    var = jnp.mean(xc * xc, -1, keepdims=True)
    return xc * lax.rsqrt(var + LN_EPS) * g + b


def _rms(x, g):
    return x * lax.rsqrt(jnp.mean(x * x, -1, keepdims=True) + RMS_EPS) * g


def _inproj_kernel(x_ref, w_ref, o_ref):
    o_ref[...] = _dot(x_ref[...].astype(BF16), w_ref[...]).astype(o_ref.dtype)


def _inproj(x2, w_packed, tm=1024, tn=768):
    T, D = x2.shape
    N = w_packed.shape[1]
    return pl.pallas_call(
        _inproj_kernel,
        grid=(T // tm, N // tn),
        in_specs=[pl.BlockSpec((tm, D), lambda i, j: (i, 0)),
                  pl.BlockSpec((D, tn), lambda i, j: (0, j))],
        out_specs=pl.BlockSpec((tm, tn), lambda i, j: (i, j)),
        out_shape=jax.ShapeDtypeStruct((T, N), F32),
        compiler_params=_params(("parallel", "arbitrary")),
        name="inproj",
    )(x2, w_packed)


def _rope_table_kernel(pos_ref, inv_ref, c_ref, s_ref):
    ang = pos_ref[...].astype(F32) * inv_ref[...]
    c_ref[...] = jnp.cos(ang)
    s_ref[...] = jnp.sin(ang)


def _rope_tables(pos_col, inv_row, tm=1024):
    T = pos_col.shape[0]
    return pl.pallas_call(
        _rope_table_kernel,
        grid=(T // tm,),
        in_specs=[pl.BlockSpec((tm, 1), lambda i: (i, 0)),
                  pl.BlockSpec((1, LANES), lambda i: (0, 0))],
        out_specs=[pl.BlockSpec((tm, LANES), lambda i: (i, 0))] * 2,
        out_shape=[jax.ShapeDtypeStruct((T, LANES), F32)] * 2,
        compiler_params=_params(("parallel",)),
        name="rope_tables",
    )(pos_col, inv_row)


def _mla_prep_kernel(cq_ref, ckv_ref, kra_ref, krb_ref, c_ref, s_ref, gq_ref, gkv_ref,
                     wq_ref, wqr_ref, wk_ref, wv_ref, q_ref, k_ref, v_ref):
    scale = (MLA_NOPE + MLA_ROPE) ** -0.5
    cos = c_ref[...]
    sin = s_ref[...]
    nq = _rms(cq_ref[...], gq_ref[...]).astype(BF16)
    q_pre = _dot(nq, wq_ref[...])
    q_rot = _dot(nq, wqr_ref[...])
    nkv = _rms(ckv_ref[...], gkv_ref[...]).astype(BF16)
    k_nope = _dot(nkv, wk_ref[...])
    v_ref[...] = _dot(nkv, wv_ref[...]).astype(v_ref.dtype)
    k_rope = (kra_ref[...] * cos + krb_ref[...] * sin).astype(k_ref.dtype)
    for h in range(MLA_HEADS):
        o = h * MLA_HEAD_PAD
        q_ref[:, o:o + LANES] = (q_pre[:, o:o + LANES] * scale).astype(q_ref.dtype)
        q_ref[:, o + LANES:o + 2 * LANES] = (
            (q_pre[:, o + LANES:o + 2 * LANES] * cos + q_rot[:, h * LANES:(h + 1) * LANES] * sin) * scale
        ).astype(q_ref.dtype)
        k_ref[:, o:o + LANES] = k_nope[:, h * LANES:(h + 1) * LANES].astype(k_ref.dtype)
        k_ref[:, o + LANES:o + 2 * LANES] = k_rope


def _mla_prep(proj, cos_t, sin_t, g_q, g_kv, wq, wqr, wk, wv, tm=512):
    T = proj.shape[0]
    row = lambda w, j: pl.BlockSpec((tm, w), lambda i, j=j: (i, j))
    full = lambda a: pl.BlockSpec(a.shape, lambda i: (0,) * a.ndim)
    HW = MLA_HEADS * MLA_HEAD_PAD
    return pl.pallas_call(
        _mla_prep_kernel,
        grid=(T // tm,),
        in_specs=[row(256, PK_CQ), row(128, PK_CKV), row(128, PK_KRA), row(128, PK_KRB),
                  row(128, 0), row(128, 0), full(g_q), full(g_kv), full(wq), full(wqr), full(wk), full(wv)],
        out_specs=[row(HW, 0), row(HW, 0), row(MLA_HEADS * MLA_V, 0)],
        out_shape=[jax.ShapeDtypeStruct((T, HW), BF16), jax.ShapeDtypeStruct((T, HW), BF16),
                   jax.ShapeDtypeStruct((T, MLA_HEADS * MLA_V), BF16)],
        compiler_params=_params(("parallel",)),
        name="mla_prep",
    )(proj, proj, proj, proj, cos_t, sin_t, g_q, g_kv, wq, wqr, wk, wv)


def _attn_kernel(q_ref, k_ref, v_ref, o_ref, *, tq):
    i = pl.program_id(1)
    row = lax.broadcasted_iota(jnp.int32, (tq, tq), 0)
    col = lax.broadcasted_iota(jnp.int32, (tq, tq), 1)
    causal = row >= col
    for h in range(MLA_HEADS):
        qh = q_ref[:, h * MLA_HEAD_PAD:(h + 1) * MLA_HEAD_PAD]

        def chunk(j, carry, masked, h=h, qh=qh):
            m, l, acc = carry
            start = pl.multiple_of(j * tq, tq)
            kh = k_ref[pl.ds(start, tq), h * MLA_HEAD_PAD:(h + 1) * MLA_HEAD_PAD]
            vh = v_ref[pl.ds(start, tq), h * MLA_V:(h + 1) * MLA_V]
            s = _dot_nt(qh, kh)
            if masked:
                s = jnp.where(causal, s, MASK_VALUE)
            m_new = jnp.maximum(m, jnp.max(s, -1, keepdims=True))
            p = jnp.exp(s - m_new)
            a = jnp.exp(m - m_new)
            l = a * l + jnp.sum(p, -1, keepdims=True)
            acc = a * acc + _dot(p.astype(BF16), vh)
            return m_new, l, acc

        carry = (jnp.full((tq, 1), MASK_VALUE, F32), jnp.zeros((tq, 1), F32), jnp.zeros((tq, MLA_V), F32))
        carry = lax.fori_loop(0, i, functools.partial(chunk, masked=False), carry)
        m, l, acc = chunk(i, carry, True)
        o_ref[:, h * MLA_V:(h + 1) * MLA_V] = (acc / l).astype(o_ref.dtype)


def _attention(q, k, v, B, S, tq=512):
    T = q.shape[0]
    nq = S // tq
    HW = MLA_HEADS * MLA_HEAD_PAD
    HV = MLA_HEADS * MLA_V
    return pl.pallas_call(
        functools.partial(_attn_kernel, tq=tq),
        grid=(B, nq),
        in_specs=[pl.BlockSpec((tq, HW), lambda b, i: (b * nq + i, 0)),
                  pl.BlockSpec((S, HW), lambda b, i: (b, 0)),
                  pl.BlockSpec((S, HV), lambda b, i: (b, 0))],
        out_specs=pl.BlockSpec((tq, HV), lambda b, i: (b * nq + i, 0)),
        out_shape=jax.ShapeDtypeStruct((T, HV), BF16),
        compiler_params=_params(("parallel", "arbitrary")),
        name="mla_attention",
    )(q, k, v)


def _shift_rows(x, prev8, k):
    rx = pltpu.roll(x, k, 0)
    cx = pltpu.roll(prev8, k, 0)
    r8 = lax.broadcasted_iota(jnp.int32, (8, x.shape[1]), 0)
    top = jnp.where(r8 < k, cx, rx[0:8])
    return jnp.concatenate([top, rx[8:]], axis=0)


def _gdn_kernel(q_ref, k_ref, v_ref, z_ref, ab_ref, cw_ref, alog_ref, dtb_ref, ng_ref, o_ref,
                carry_ref, state_ref):
    si = pl.program_id(1)
    HD = GDN_HEADS * GDN_DK

    @pl.when(si == 0)
    def _():
        carry_ref[...] = jnp.zeros_like(carry_ref)
        state_ref[...] = jnp.zeros_like(state_ref)

    def conv_silu(x_ref, idx):
        x = x_ref[...]
        prev8 = carry_ref[:, idx * HD:(idx + 1) * HD]
        w4 = cw_ref[:, idx * HD:(idx + 1) * HD]
        y = x * w4[GDN_CONV - 1:GDN_CONV]
        for kk in range(1, GDN_CONV):
            y = y + _shift_rows(x, prev8, kk) * w4[GDN_CONV - 1 - kk:GDN_CONV - kk]
        carry_ref[:, idx * HD:(idx + 1) * HD] = x[SPAN - 8:SPAN]
        return _silu(y)

    q = conv_silu(q_ref, 0)
    k = conv_silu(k_ref, 1)
    v = conv_silu(v_ref, 2)

    ab = ab_ref[...]
    g_full = -jnp.exp(alog_ref[...]) * _softplus(ab + dtb_ref[...])
    beta_full = _sigmoid(ab)

    row = lax.broadcasted_iota(jnp.int32, (SPAN, SPAN), 0)
    col = lax.broadcasted_iota(jnp.int32, (SPAN, SPAN), 1)
    same_chunk = (row // GDN_CHUNK) == (col // GDN_CHUNK)
    tril = same_chunk & (col <= row)
    stril = same_chunk & (col < row)
    eye = (row == col).astype(F32)
    gc_full = _exact_left_dot(tril.astype(BF16), g_full)
    gc_t = gc_full.T

    for h in range(GDN_HEADS):
        sl = slice(h * GDN_DK, (h + 1) * GDN_DK)
        qh = q[:, sl]
        kh = k[:, sl]
        vh = v[:, sl]
        qh = qh * lax.rsqrt(jnp.sum(qh * qh, -1, keepdims=True) + RMS_EPS) * (GDN_DK ** -0.5)
        kh = kh * lax.rsqrt(jnp.sum(kh * kh, -1, keepdims=True) + RMS_EPS)
        gc = gc_full[:, h:h + 1]
        gc_row = gc_t[h:h + 1, :]
        beta = beta_full[:, GDN_HEADS + h:GDN_HEADS + h + 1]
        decay = jnp.where(tril, jnp.exp(jnp.where(tril, gc - gc_row, 0.0)), 0.0)
        k16 = kh.astype(BF16)
        a = jnp.where(stril, beta * _dot_nt(k16, k16) * decay, 0.0)
        qk = jnp.where(tril, _dot_nt(qh.astype(BF16), k16) * decay, 0.0)
        x = eye - a
        p16 = a.astype(BF16)
        for _ in range(5):
            p = _dot(p16, p16)
            p16 = p.astype(BF16)
            x = x + _dot(x.astype(BF16), p16)
        egc = jnp.exp(gc)
        rhs = jnp.concatenate([vh * beta, kh * (beta * egc)], axis=1).astype(BF16)
        uw = _dot(x.astype(BF16), rhs)
        u = uw[:, :GDN_DV]
        w = uw[:, GDN_DV:]
        qg = (qh * egc).astype(BF16)
        state = state_ref[h]
        o_state = []
        v_new = []
        for c in range(SPAN // GDN_CHUNK):
            cs = slice(c * GDN_CHUNK, (c + 1) * GDN_CHUNK)
            s16 = state.astype(BF16)
            vn = u[cs] - _dot(w[cs].astype(BF16), s16)
            o_state.append(_dot(qg[cs], s16))
            g_last = gc[(c + 1) * GDN_CHUNK - 1:(c + 1) * GDN_CHUNK, :]
            kd = (kh[cs] * jnp.exp(g_last - gc[cs])).astype(BF16)
            state = state * jnp.exp(g_last) + _dot_tn(kd, vn.astype(BF16))
            v_new.append(vn)
        state_ref[h] = state
        o = jnp.concatenate(o_state, axis=0) + _dot(qk.astype(BF16), jnp.concatenate(v_new, axis=0).astype(BF16))
        o = _rms(o, ng_ref[...]) * _silu(z_ref[:, sl])
        o_ref[:, sl] = o.astype(o_ref.dtype)


def _gdn(proj, conv_w, alog_row, dtb_row, norm_g, B, S):
    T = proj.shape[0]
    ns = S // SPAN
    HD = GDN_HEADS * GDN_DK
    blk = lambda w, j: pl.BlockSpec((SPAN, w), lambda b, s, j=j: (b * ns + s, j))
    full = lambda a: pl.BlockSpec(a.shape, lambda b, s: (0,) * a.ndim)
    return pl.pallas_call(
        _gdn_kernel,
        grid=(B, ns),
        in_specs=[blk(HD, PK_GQ), blk(HD, PK_GK), blk(HD, PK_GV), blk(HD, PK_GZ), blk(LANES, PK_AB),
                  full(conv_w), full(alog_row), full(dtb_row), full(norm_g)],
        out_specs=blk(HD, 0),
        out_shape=jax.ShapeDtypeStruct((T, HD), BF16),
        scratch_shapes=[pltpu.VMEM((8, 3 * HD), F32), pltpu.VMEM((GDN_HEADS, GDN_DK, GDN_DV), F32)],
        compiler_params=_params(("parallel", "arbitrary")),
        name="gdn",
    )(proj, proj, proj, proj, proj, conv_w, alog_row, dtb_row, norm_g)


def _pool_kernel(u_ref, w_ref, sc_ref, o_ref, carry_ref, *, tp):
    si = pl.program_id(1)
    G = POOL_GROUP

    @pl.when(si == 0)
    def _():
        carry_ref[...] = jnp.zeros_like(carry_ref)

    u = u_ref[...]
    ext = jnp.concatenate([carry_ref[...], u], axis=0)
    carry_ref[...] = u[tp - 16:tp]
    t = si * tp + lax.broadcasted_iota(jnp.int32, (tp, 1), 0)
    win_sum = ext
    outs = []
    shift = 1
    for gi, win in enumerate(POOL_WINDOWS):
        while shift < win:
            win_sum = win_sum + pltpu.roll(win_sum, shift, 0)
            shift *= 2
        cnt = jnp.minimum(t + 1, win).astype(F32)
        ug = u[:, gi * G:(gi + 1) * G]
        pooled = win_sum[16:, gi * G:(gi + 1) * G] / cnt - ug
        outs.append(_dot(pooled.astype(BF16), w_ref[gi]))
    y = jnp.concatenate(outs, axis=1) * sc_ref[...]
    o_ref[...] = y.astype(o_ref.dtype)


def _pool(proj, w_group, scale_row, B, S, tp=512):
    T = proj.shape[0]
    ns = S // tp
    W = len(POOL_WINDOWS) * POOL_GROUP
    return pl.pallas_call(
        functools.partial(_pool_kernel, tp=tp),
        grid=(B, ns),
        in_specs=[pl.BlockSpec((tp, W), lambda b, s: (b * ns + s, PK_PU)),
                  pl.BlockSpec(w_group.shape, lambda b, s: (0, 0, 0)),
                  pl.BlockSpec(scale_row.shape, lambda b, s: (0, 0))],
        out_specs=pl.BlockSpec((tp, W), lambda b, s: (b * ns + s, 0)),
        out_shape=jax.ShapeDtypeStruct((T, W), BF16),
        scratch_shapes=[pltpu.VMEM((16, W), F32)],
        compiler_params=_params(("parallel", "arbitrary")),
        name="pool",
    )(proj, w_group, scale_row)


def _mid_broadcast(c, s):
    n = c.shape[0]
    if 2 * s >= 8:
        c3 = c.reshape(n // (2 * s), 2 * s, c.shape[1])
        mid = jnp.broadcast_to(c3[:, s - 1:s, :], c3.shape)
        return mid.reshape(c.shape)
    pos = lax.broadcasted_iota(jnp.int32, c.shape, 0) % (2 * s)
    out = c
    for p in range(2 * s):
        d = p - (s - 1)
        if d != 0:
            out = jnp.where(pos == p, pltpu.roll(c, d % n, 0), out)
    return out


def _hgrn_kernel(q_ref, f_ref, i_ref, g_ref, lb_ref, ng_ref, o_ref, state_ref):
    si = pl.program_id(1)

    @pl.when(si == 0)
    def _():
        state_ref[...] = jnp.zeros_like(state_ref)

    lb = lb_ref[...]
    q = _silu(q_ref[...])
    fp = f_ref[...]
    log_f = jnp.log(lb + (1.0 - lb) * _sigmoid(fp))
    k = (1.0 - lb) * _sigmoid(-fp)
    v16 = i_ref[...].astype(BF16)

    row = lax.broadcasted_iota(jnp.int32, (SPAN, SPAN), 0)
    col = lax.broadcasted_iota(jnp.int32, (SPAN, SPAN), 1)
    c = _exact_left_dot((col <= row).astype(BF16), log_f)
    c_last = c[SPAN - 1:SPAN, :]
    rowi = lax.broadcasted_iota(jnp.int32, (SPAN, 1), 0)

    q16 = q.astype(BF16)
    k16 = k.astype(BF16)
    att = [jnp.where(row == col, _dot_nt(q16[:, h * HGRN_DK:(h + 1) * HGRN_DK],
                                         k16[:, h * HGRN_DK:(h + 1) * HGRN_DK]), 0.0)
           for h in range(HGRN_HEADS)]
    s = 1
    while s < SPAN:
        wgt = jnp.exp(-jnp.abs(c - _mid_broadcast(c, s)))
        right = ((rowi // s) % 2) == 1
        ql = jnp.where(right, q * wgt, 0.0).astype(BF16)
        kl = jnp.where(right, 0.0, k * wgt).astype(BF16)
        same_parent = (row // (2 * s)) == (col // (2 * s))
        for h in range(HGRN_HEADS):
            sl = slice(h * HGRN_DK, (h + 1) * HGRN_DK)
            att[h] = att[h] + jnp.where(same_parent, _dot_nt(ql[:, sl], kl[:, sl]), 0.0)
        s *= 2

    qe = (q * jnp.exp(c)).astype(BF16)
    ke = (k * jnp.exp(c_last - c)).astype(BF16)
    e_last = jnp.exp(c_last)
    for h in range(HGRN_HEADS):
        sl = slice(h * HGRN_DK, (h + 1) * HGRN_DK)
        st = state_ref[h]
        o = _dot(att[h].astype(BF16), v16[:, sl]) + _dot_nt(qe[:, sl], st.astype(BF16))
        state_ref[h] = st * e_last[:, sl] + _dot_tn(v16[:, sl], ke[:, sl])
        o = _rms(o, ng_ref[...]) * _sigmoid(g_ref[:, sl])
        o_ref[:, sl] = o.astype(o_ref.dtype)


def _hgrn(proj, lb_row, norm_g, B, S):
    T = proj.shape[0]
    ns = S // SPAN
    HD = HGRN_HEADS * HGRN_DK
    blk = lambda j: pl.BlockSpec((SPAN, HD), lambda b, s, j=j: (b * ns + s, j))
    full = lambda a: pl.BlockSpec(a.shape, lambda b, s: (0,) * a.ndim)
    return pl.pallas_call(
        _hgrn_kernel,
        grid=(B, ns),
        in_specs=[blk(PK_HQ), blk(PK_HF), blk(PK_HI), blk(PK_HG), full(lb_row), full(norm_g)],
        out_specs=blk(0),
        out_shape=jax.ShapeDtypeStruct((T, HD), BF16),
        scratch_shapes=[pltpu.VMEM((HGRN_HEADS, HGRN_DV, HGRN_DK), F32)],
        compiler_params=_params(("parallel", "arbitrary")),
        name="hgrn2",
    )(proj, proj, proj, proj, lb_row, norm_g)


def _merge_kernel(x_ref, ya_ref, yb_ref, yc_ref, yd_ref, wg_ref, wb_ref, wo_ref, g_ref, b_ref, o_ref):
    x = x_ref[...]
    x16 = x.astype(BF16)
    merged = None
    for m, y_ref in enumerate((ya_ref, yb_ref, yc_ref, yd_ref)):
        gate = _sigmoid(_dot(x16, wg_ref[:, m * D_MODEL:(m + 1) * D_MODEL]))
        term = gate * _dot(y_ref[...], wb_ref[m])
        merged = term if merged is None else merged + term
    h = DEEPNORM_ALPHA * x + _dot(merged.astype(BF16), wo_ref[...])
    o_ref[...] = _layer_norm(h, g_ref[...], b_ref[...])


def _merge(x2, ya, yb, yc, yd, w_gate, w_branch, w_out, ln_g, ln_b, tm=512):
    T, D = x2.shape
    row = lambda w: pl.BlockSpec((tm, w), lambda i: (i, 0))
    full = lambda a: pl.BlockSpec(a.shape, lambda i: (0,) * a.ndim, pipeline_mode=pl.Buffered(1))
    return pl.pallas_call(
        _merge_kernel,
        grid=(T // tm,),
        in_specs=[row(D), row(BRANCH_WIDTH), row(BRANCH_WIDTH), row(BRANCH_WIDTH), row(BRANCH_WIDTH),
                  full(w_gate), full(w_branch), full(w_out), full(ln_g), full(ln_b)],
        out_specs=row(D),
        out_shape=jax.ShapeDtypeStruct((T, D), F32),
        compiler_params=_params(("parallel",)),
        name="merge",
    )(x2, ya, yb, yc, yd, w_gate, w_branch, w_out, ln_g, ln_b)


def _ffn_kernel(x_ref, wg_ref, wu_ref, wd_ref, g_ref, b_ref, o_ref):
    x = x_ref[...]
    x16 = x.astype(BF16)
    h = _silu(_dot(x16, wg_ref[...])) * _dot(x16, wu_ref[...])
    y = DEEPNORM_ALPHA * x + _dot(h.astype(BF16), wd_ref[...])
    o_ref[...] = _layer_norm(y, g_ref[...], b_ref[...])


def _ffn(x2, wg, wu, wd, ln_g, ln_b, tm=512):
    T, D = x2.shape
    row = pl.BlockSpec((tm, D), lambda i: (i, 0))
    full = lambda a: pl.BlockSpec(a.shape, lambda i: (0,) * a.ndim, pipeline_mode=pl.Buffered(1))
    return pl.pallas_call(
        _ffn_kernel,
        grid=(T // tm,),
        in_specs=[row, full(wg), full(wu), full(wd), full(ln_g), full(ln_b)],
        out_specs=row,
        out_shape=jax.ShapeDtypeStruct((T, D), F32),
        compiler_params=_params(("parallel",)),
        name="ffn_dense",
    )(x2, wg, wu, wd, ln_g, ln_b)


def _router_kernel(x_ref, w_ref, meta_ref, gate_ref, cnt_ref, base_ref, *, tm):
    i = pl.program_id(0)

    @pl.when(i == 0)
    def _():
        base_ref[...] = jnp.zeros_like(base_ref)

    xh, xm, _ = _split3(x_ref[...])
    wh, wm, _ = _split3(w_ref[...])
    logits = _dot(xh, wh) + _dot(xh, wm) + _dot(xm, wh)
    lane = lax.broadcasted_iota(jnp.int32, (tm, LANES), 1)
    neg = jnp.float32(-jnp.inf)
    logits = jnp.where(lane < N_EXPERTS, logits, neg)
    v1 = jnp.max(logits, -1, keepdims=True)
    e1 = jnp.min(jnp.where(logits == v1, lane, LANES), -1, keepdims=True)
    rest = jnp.where(lane == e1, neg, logits)
    v2 = jnp.max(rest, -1, keepdims=True)
    e2 = jnp.min(jnp.where(rest == v2, lane, LANES), -1, keepdims=True)
    ex = jnp.exp(v2 - v1)
    w1 = 1.0 / (1.0 + ex)
    w2 = ex / (1.0 + ex)
    onehot = ((lane == e1) | (lane == e2)).astype(F32)
    r = lax.broadcasted_iota(jnp.int32, (tm, tm), 0)
    cc = lax.broadcasted_iota(jnp.int32, (tm, tm), 1)
    before = _dot((cc < r).astype(BF16), onehot.astype(BF16)) + base_ref[0:1, :]
    rank1 = jnp.sum(jnp.where(lane == e1, before, 0.0), -1, keepdims=True)
    rank2 = jnp.sum(jnp.where(lane == e2, before, 0.0), -1, keepdims=True)
    total = base_ref[0:1, :] + jnp.sum(onehot, 0, keepdims=True)
    base_ref[...] = jnp.broadcast_to(total, base_ref.shape)
    cnt_ref[...] = jnp.broadcast_to(total, cnt_ref.shape).astype(jnp.int32)
    meta = jnp.where(lane == 0, e1, jnp.where(lane == 1, e2, jnp.where(
        lane == 2, rank1.astype(jnp.int32), jnp.where(lane == 3, rank2.astype(jnp.int32), 0))))
    meta_ref[...] = meta
    gate_ref[...] = jnp.where(lane == 0, w1, jnp.where(lane == 1, w2, 0.0))


def _router(x2, w_router_pad, tm=512):
    T, D = x2.shape
    return pl.pallas_call(
        functools.partial(_router_kernel, tm=tm),
        grid=(T // tm,),
        in_specs=[pl.BlockSpec((tm, D), lambda i: (i, 0)),
                  pl.BlockSpec(w_router_pad.shape, lambda i: (0, 0))],
        out_specs=[pl.BlockSpec((tm, LANES), lambda i: (i, 0)),
                   pl.BlockSpec((tm, LANES), lambda i: (i, 0)),
                   pl.BlockSpec((8, LANES), lambda i: (0, 0))],
        out_shape=[jax.ShapeDtypeStruct((T, LANES), jnp.int32),
                   jax.ShapeDtypeStruct((T, LANES), F32),
                   jax.ShapeDtypeStruct((8, LANES), jnp.int32)],
        scratch_shapes=[pltpu.VMEM((8, LANES), F32)],
        compiler_params=_params(("arbitrary",)),
        name="moe_router",
    )(x2, w_router_pad)


def _dispatch_kernel(dest_ref, x_ref, xb_in_ref, xb_ref, sem, *, td):
    del xb_in_ref

    def copy(r, slot):
        return pltpu.make_async_copy(x_ref.at[pl.ds(r, 1)], xb_ref.at[pl.ds(dest_ref[0, 0, slot * td + r], 1)], sem)

    def start(r, c):
        copy(r, 0).start()
        copy(r, 1).start()
        return c

    def wait(r, c):
        copy(r, 0).wait()
        copy(r, 1).wait()
        return c

    lax.fori_loop(0, td, start, 0)
    lax.fori_loop(0, td, wait, 0)


def _dispatch(x2, dest_blocks, n_rows, td=256):
    T, D = x2.shape
    xb0 = jnp.zeros((n_rows, D), F32)
    return pl.pallas_call(
        functools.partial(_dispatch_kernel, td=td),
        grid=(T // td,),
        in_specs=[pl.BlockSpec((1, 1, 2 * td), lambda i: (i, 0, 0), memory_space=pltpu.SMEM),
                  pl.BlockSpec((td, D), lambda i: (i, 0)),
                  pl.BlockSpec(memory_space=pl.ANY)],
        out_specs=pl.BlockSpec(memory_space=pl.ANY),
        out_shape=jax.ShapeDtypeStruct((n_rows, D), F32),
        scratch_shapes=[pltpu.SemaphoreType.DMA(())],
        input_output_aliases={2: 0},
        compiler_params=_params(("arbitrary",)),
        name="moe_dispatch",
    )(dest_blocks, x2, xb0)


def _expert_kernel(blk_exp_ref, n_used_ref, x_ref, wg_ref, wu_ref, wd_ref, o_ref, acc_ref):
    i = pl.program_id(0)
    j = pl.program_id(1)

    @pl.when(i < n_used_ref[0])
    def _():
        x16 = x_ref[...].astype(BF16)
        h = _silu(_dot(x16, wg_ref[0])) * _dot(x16, wu_ref[0])
        y = _dot(h.astype(BF16), wd_ref[0])

        @pl.when(j == 0)
        def _():
            acc_ref[...] = y

        @pl.when(j > 0)
        def _():
            acc_ref[...] += y

        @pl.when(j == pl.num_programs(1) - 1)
        def _():
            o_ref[...] = acc_ref[...]

    @pl.when(i >= n_used_ref[0])
    def _():
        o_ref[...] = jnp.zeros_like(o_ref)


def _experts(xb, blk_exp, n_used, wg, wu, wd, tf=1792):
    n_rows, D = xb.shape
    n_blk = n_rows // MOE_ROWS
    nf = D_FF_EXPERT // tf

    def rows(i, j, be, nu):
        return (jnp.minimum(i, nu[0] - 1), 0)

    grid_spec = pltpu.PrefetchScalarGridSpec(
        num_scalar_prefetch=2,
        grid=(n_blk, nf),
        in_specs=[pl.BlockSpec((MOE_ROWS, D), rows),
                  pl.BlockSpec((1, D, tf), lambda i, j, be, nu: (be[jnp.minimum(i, nu[0] - 1)], 0, j)),
                  pl.BlockSpec((1, D, tf), lambda i, j, be, nu: (be[jnp.minimum(i, nu[0] - 1)], 0, j)),
                  pl.BlockSpec((1, tf, D), lambda i, j, be, nu: (be[jnp.minimum(i, nu[0] - 1)], j, 0))],
        out_specs=pl.BlockSpec((MOE_ROWS, D), lambda i, j, be, nu: (i, 0)),
        scratch_shapes=[pltpu.VMEM((MOE_ROWS, D), F32)],
    )
    return pl.pallas_call(
        _expert_kernel,
        grid_spec=grid_spec,
        out_shape=jax.ShapeDtypeStruct((n_rows, D), F32),
        compiler_params=_params(("arbitrary", "arbitrary")),
        name="moe_experts",
    )(blk_exp, n_used, xb, wg, wu, wd)


def _combine_kernel(dest_ref, x_ref, gate_ref, yb_ref, g_ref, b_ref, o_ref, buf_ref, sem, *, tc):
    def copy(r, slot):
        return pltpu.make_async_copy(yb_ref.at[pl.ds(dest_ref[0, 0, slot * tc + r], 1)],
                                     buf_ref.at[slot, pl.ds(r, 1)], sem)

    def start(r, c):
        copy(r, 0).start()
        copy(r, 1).start()
        return c

    def wait(r, c):
        copy(r, 0).wait()
        copy(r, 1).wait()
        return c

    lax.fori_loop(0, tc, start, 0)
    lax.fori_loop(0, tc, wait, 0)
    gate = gate_ref[...]
    ffn = buf_ref[0] * gate[:, 0:1] + buf_ref[1] * gate[:, 1:2]
    o_ref[...] = _layer_norm(DEEPNORM_ALPHA * x_ref[...] + ffn, g_ref[...], b_ref[...])


def _combine(x2, gates, yb, dest_blocks, ln_g, ln_b, tc=256):
    T, D = x2.shape
    full = lambda a: pl.BlockSpec(a.shape, lambda i: (0,) * a.ndim)
    return pl.pallas_call(
        functools.partial(_combine_kernel, tc=tc),
        grid=(T // tc,),
        in_specs=[pl.BlockSpec((1, 1, 2 * tc), lambda i: (i, 0, 0), memory_space=pltpu.SMEM),
                  pl.BlockSpec((tc, D), lambda i: (i, 0)),
                  pl.BlockSpec((tc, LANES), lambda i: (i, 0)),
                  pl.BlockSpec(memory_space=pl.ANY), full(ln_g), full(ln_b)],
        out_specs=pl.BlockSpec((tc, D), lambda i: (i, 0)),
        out_shape=jax.ShapeDtypeStruct((T, D), F32),
        scratch_shapes=[pltpu.VMEM((2, tc, D), F32), pltpu.SemaphoreType.DMA(())],
        compiler_params=_params(("arbitrary",)),
        name="moe_combine",
    )(dest_blocks, x2, gates, yb, ln_g, ln_b)


def _moe(x2, w_router, wg, wu, wd, ln_g, ln_b, tok=256):
    T, D = x2.shape
    w_router_pad = jnp.pad(w_router.astype(F32), ((0, 0), (0, LANES - N_EXPERTS)))
    meta, gates, counts = _router(x2, w_router_pad)
    counts = counts[0, :N_EXPERTS]
    padded = (counts + MOE_ROWS - 1) // MOE_ROWS * MOE_ROWS
    pends = jnp.cumsum(padded)
    pstarts = pends - padded
    dest = pstarts[meta[:, 0:2]] + meta[:, 2:4]
    dest_blocks = dest.reshape(T // tok, tok, 2).transpose(0, 2, 1).reshape(T // tok, 1, 2 * tok)
    n_blk = (T * TOP_K + MOE_ROWS - 1) // MOE_ROWS + N_EXPERTS
    n_rows = n_blk * MOE_ROWS
    blk_exp = jnp.minimum(jnp.searchsorted(pends, jnp.arange(n_blk, dtype=jnp.int32) * MOE_ROWS, side='right'),
                          N_EXPERTS - 1).astype(jnp.int32)
    n_used = (pends[-1:] // MOE_ROWS).astype(jnp.int32)
    xb = _dispatch(x2, dest_blocks, n_rows, td=tok)
    yb = _experts(xb, blk_exp, n_used, wg, wu, wd)
    return _combine(x2, gates, yb, dest_blocks, ln_g, ln_b, tc=tok)


def _rot_half_cols(w):
    half = w.shape[-1] // 2
    return jnp.concatenate([-w[..., half:], w[..., :half]], axis=-1)


def _pack_w_in(w):
    D = w.shape[0]
    kr = w[:, 384:448]
    z64 = jnp.zeros((D, 64), w.dtype)
    ab = jnp.concatenate([w[:, 2496:2504], jnp.zeros((D, LANES - 8), w.dtype)], axis=1)
    packed = jnp.concatenate([
        w[:, 448:2496],
        w[:, 2504:3016],
        w[:, 3016:5064],
        w[:, 0:256],
        w[:, 256:384],
        kr, z64,
        _rot_half_cols(kr), z64,
        ab,
    ], axis=1)
    return packed.astype(BF16), w[:, 5064:].astype(BF16)


def _pack_mla(w_uq, w_ukv):
    R = w_uq.shape[0]
    z64 = jnp.zeros((R, 64), w_uq.dtype)
    wq, wqr = [], []
    for h in range(MLA_HEADS):
        o = h * (MLA_NOPE + MLA_ROPE)
        rope_cols = w_uq[:, o + MLA_NOPE:o + MLA_NOPE + MLA_ROPE]
        wq += [w_uq[:, o:o + MLA_NOPE], rope_cols, z64]
        wqr += [_rot_half_cols(rope_cols), z64]
    kv = w_ukv.reshape(w_ukv.shape[0], MLA_HEADS, MLA_NOPE + MLA_V)
    wk = kv[:, :, :MLA_NOPE].reshape(w_ukv.shape[0], MLA_HEADS * MLA_NOPE)
    wv = kv[:, :, MLA_NOPE:].reshape(w_ukv.shape[0], MLA_HEADS * MLA_V)
    return (jnp.concatenate(wq, 1).astype(BF16), jnp.concatenate(wqr, 1).astype(BF16),
            wk.astype(BF16), wv.astype(BF16))


def _lane_row(v, offset=0):
    v = v.astype(F32)
    return jnp.pad(v, (offset, LANES - offset - v.shape[0]))[None, :]


def kernel(x, positions, w_in, mla_q_norm, mla_w_uq, mla_kv_norm, mla_w_ukv, gdn_conv, gdn_a_log, gdn_dt_bias, gdn_norm, pool_w, pool_scale, hgrn_lb_logits, hgrn_norm, w_branch, w_out, ln_mix_g, ln_mix_b, ffn_w_gate, ffn_w_up, ffn_w_down, moe_router, moe_w_gate, moe_w_up, moe_w_down, ln_ffn_g, ln_ffn_b):
    B, S, D = x.shape
    T = B * S
    x2 = x.reshape(T, D)

    half = MLA_ROPE // 2
    inv = ROPE_THETA ** (-jnp.arange(half, dtype=F32) / half)
    inv_row = jnp.concatenate([inv, inv, jnp.zeros((LANES - MLA_ROPE,), F32)])[None, :]
    cos_t, sin_t = _rope_tables(positions.reshape(T, 1).astype(jnp.int32), inv_row)

    p_lb = jax.nn.softmax(hgrn_lb_logits.astype(F32), axis=0)
    lower_bounds = jnp.cumsum(p_lb, axis=0) - p_lb[0]
    row2 = lambda v: v.astype(F32)[None, :]

    for l in range(DEPTH):
        w_packed, w_gate = _pack_w_in(w_in[l])
        proj = _inproj(x2, w_packed)
        wq, wqr, wk, wv = _pack_mla(mla_w_uq[l], mla_w_ukv[l])
        q, k, v = _mla_prep(proj, cos_t, sin_t, row2(mla_q_norm[l]), row2(mla_kv_norm[l]), wq, wqr, wk, wv)
        y_a = _attention(q, k, v, B, S)
        y_b = _gdn(proj, gdn_conv[l].astype(F32), _lane_row(gdn_a_log[l]), _lane_row(gdn_dt_bias[l]),
                   row2(gdn_norm[l]), B, S)
        y_c = _pool(proj, pool_w[l].astype(BF16), row2(pool_scale[l]), B, S)
        y_d = _hgrn(proj, row2(lower_bounds[l]), row2(hgrn_norm[l]), B, S)
        x2 = _merge(x2, y_a, y_b, y_c, y_d, w_gate, w_branch[l].astype(BF16), w_out[l].astype(BF16),
                    row2(ln_mix_g[l]), row2(ln_mix_b[l]))
        j = l // 2
        if l % 2 == 0:
            x2 = _ffn(x2, ffn_w_gate[j].astype(BF16), ffn_w_up[j].astype(BF16), ffn_w_down[j].astype(BF16),
                      row2(ln_ffn_g[l]), row2(ln_ffn_b[l]))
        else:
            x2 = _moe(x2, moe_router[j], moe_w_gate[j].astype(BF16), moe_w_up[j].astype(BF16),
                      moe_w_down[j].astype(BF16), row2(ln_ffn_g[l]), row2(ln_ffn_b[l]))
    return x2.reshape(B, S, D)
```

```python
import functools
import math

import jax
import jax.numpy as jnp
from jax import lax
from jax.experimental import pallas as pl
from jax.experimental.pallas import tpu as pltpu

F32 = jnp.float32
BF16 = jnp.bfloat16

D_MODEL = 1024
DEPTH = 2
MLA_HEADS = 4
MLA_Q_RANK = 256
MLA_KV_RANK = 128
MLA_NOPE = 128
MLA_ROPE = 64
MLA_V = 128
ROPE_THETA = 10000.0
MASK_VALUE = -1e30
GDN_HEADS = 4
GDN_DK = 128
GDN_DV = 128
GDN_CONV = 4
GDN_CHUNK = 64
POOL_WINDOWS = (2, 4, 8, 16)
POOL_GROUP = 128
HGRN_HEADS = 4
HGRN_DK = 128
HGRN_DV = 128
N_BRANCH = 4
BRANCH_WIDTH = 512
D_FF = 2816
N_EXPERTS = 8
TOP_K = 2
D_FF_EXPERT = 3584
DEEPNORM_ALPHA = (2 * DEPTH) ** 0.25
LN_EPS = 1e-5
RMS_EPS = 1e-6
LOG2E = math.log2(math.e)

LANES = 128
MLA_HEAD_PAD = 256
SPAN = 256
HALF = SPAN // 2
MOE_ROWS = 512
VMEM_LIMIT = 56 * 1024 * 1024

PK_GQ, PK_GK, PK_GV, PK_GZ, PK_PU, PK_HQ, PK_HI, PK_HG = range(8)
PK_CQ = 16
PK_CKV = 34
PK_KRA = 35
PK_KRB = 36
PK16_WIDTH = 37 * LANES
PK32_HF = 0
PK32_AB = 4
PK32_WIDTH = 5 * LANES


def _params(sem, vmem=VMEM_LIMIT):
    return pltpu.CompilerParams(dimension_semantics=sem, vmem_limit_bytes=vmem)


def _dot(a, b):
    return jnp.dot(a, b, preferred_element_type=F32)


def _dot_nt(a, b):
    return lax.dot_general(a, b, (((1,), (1,)), ((), ())), preferred_element_type=F32)


def _dot_tn(a, b):
    return lax.dot_general(a, b, (((0,), (0,)), ((), ())), preferred_element_type=F32)


def _split3(x):
    hi = x.astype(BF16)
    r1 = x - hi.astype(F32)
    mid = r1.astype(BF16)
    lo = (r1 - mid.astype(F32)).astype(BF16)
    return hi, mid, lo


def _exact_left_dot(m01, x):
    hi, mid, lo = _split3(x)
    return _dot(m01, hi) + _dot(m01, mid) + _dot(m01, lo)


def _sigmoid(x):
    return 1.0 / (1.0 + jnp.exp(-x))


def _silu(x):
    return x * _sigmoid(x)


def _softplus(x):
    return jnp.maximum(x, 0.0) + jnp.log(1.0 + jnp.exp(-jnp.abs(x)))


def _layer_norm(x, g, b):
    mu = jnp.mean(x, -1, keepdims=True)
    xc = x - mu
    var = jnp.mean(xc * xc, -1, keepdims=True)
    return xc * lax.rsqrt(var + LN_EPS) * g + b


def _rms(x, g):
    return x * lax.rsqrt(jnp.mean(x * x, -1, keepdims=True) + RMS_EPS) * g


def _lane_blocks(s):
    return [s[:, c * LANES:(c + 1) * LANES] for c in range(s.shape[1] // LANES)]


def _row_max(s):
    return jnp.max(functools.reduce(jnp.maximum, _lane_blocks(s)), -1, keepdims=True)


def _row_sum(s):
    return jnp.sum(functools.reduce(jnp.add, _lane_blocks(s)), -1, keepdims=True)


def _inproj_kernel(x_ref, w16_ref, w32_ref, o16_ref, o32_ref):
    x16 = x_ref[...].astype(BF16)
    o16_ref[...] = _dot(x16, w16_ref[...]).astype(o16_ref.dtype)
    o32_ref[...] = _dot(x16, w32_ref[...])


def _inproj(x2, w16, w32, tm=512):
    T, D = x2.shape
    full = lambda a: pl.BlockSpec(a.shape, lambda i: (0, 0), pipeline_mode=pl.Buffered(1))
    return pl.pallas_call(
        _inproj_kernel,
        grid=(T // tm,),
        in_specs=[pl.BlockSpec((tm, D), lambda i: (i, 0)), full(w16), full(w32)],
        out_specs=[pl.BlockSpec((tm, w16.shape[1]), lambda i: (i, 0)),
                   pl.BlockSpec((tm, w32.shape[1]), lambda i: (i, 0))],
        out_shape=[jax.ShapeDtypeStruct((T, w16.shape[1]), BF16),
                   jax.ShapeDtypeStruct((T, w32.shape[1]), F32)],
        compiler_params=_params(("parallel",)),
        name="inproj",
    )(x2, w16, w32)


def _rope_table_kernel(pos_ref, inv_ref, c_ref, s_ref):
    ang = pos_ref[...].astype(F32) * inv_ref[...]
    c_ref[...] = jnp.cos(ang)
    s_ref[...] = jnp.sin(ang)


def _rope_tables(pos_col, inv_row, tm=1024):
    T = pos_col.shape[0]
    return pl.pallas_call(
        _rope_table_kernel,
        grid=(T // tm,),
        in_specs=[pl.BlockSpec((tm, 1), lambda i: (i, 0)),
                  pl.BlockSpec((1, LANES), lambda i: (0, 0))],
        out_specs=[pl.BlockSpec((tm, LANES), lambda i: (i, 0))] * 2,
        out_shape=[jax.ShapeDtypeStruct((T, LANES), F32)] * 2,
        compiler_params=_params(("parallel",)),
        name="rope_tables",
    )(pos_col, inv_row)


def _mla_prep_kernel(cq_ref, ckv_ref, kra_ref, krb_ref, c_ref, s_ref, gq_ref, gkv_ref,
                     wq_ref, wqr_ref, wk_ref, wv_ref, q_ref, k_ref, v_ref):
    scale = (MLA_NOPE + MLA_ROPE) ** -0.5 * LOG2E
    cos = c_ref[...]
    sin = s_ref[...]
    nq = _rms(cq_ref[...].astype(F32), gq_ref[...]).astype(BF16)
    q_pre = _dot(nq, wq_ref[...])
    q_rot = _dot(nq, wqr_ref[...])
    nkv = _rms(ckv_ref[...].astype(F32), gkv_ref[...]).astype(BF16)
    k_nope = _dot(nkv, wk_ref[...])
    v_ref[...] = _dot(nkv, wv_ref[...]).astype(v_ref.dtype)
    k_rope = (kra_ref[...].astype(F32) * cos + krb_ref[...].astype(F32) * sin).astype(k_ref.dtype)
    for h in range(MLA_HEADS):
        o = h * MLA_HEAD_PAD
        q_ref[:, o:o + LANES] = (q_pre[:, o:o + LANES] * scale).astype(q_ref.dtype)
        q_ref[:, o + LANES:o + 2 * LANES] = (
            (q_pre[:, o + LANES:o + 2 * LANES] * cos + q_rot[:, h * LANES:(h + 1) * LANES] * sin) * scale
        ).astype(q_ref.dtype)
        k_ref[:, o:o + LANES] = k_nope[:, h * LANES:(h + 1) * LANES].astype(k_ref.dtype)
        k_ref[:, o + LANES:o + 2 * LANES] = k_rope


def _mla_prep(proj, cos_t, sin_t, g_q, g_kv, wq, wqr, wk, wv, tm=512):
    T = proj.shape[0]
    row = lambda w, j: pl.BlockSpec((tm, w), lambda i, j=j: (i, j))
    full = lambda a: pl.BlockSpec(a.shape, lambda i: (0,) * a.ndim)
    HW = MLA_HEADS * MLA_HEAD_PAD
    return pl.pallas_call(
        _mla_prep_kernel,
        grid=(T // tm,),
        in_specs=[row(256, PK_CQ), row(128, PK_CKV), row(128, PK_KRA), row(128, PK_KRB),
                  row(128, 0), row(128, 0), full(g_q), full(g_kv), full(wq), full(wqr), full(wk), full(wv)],
        out_specs=[row(HW, 0), row(HW, 0), row(MLA_HEADS * MLA_V, 0)],
        out_shape=[jax.ShapeDtypeStruct((T, HW), BF16), jax.ShapeDtypeStruct((T, HW), BF16),
                   jax.ShapeDtypeStruct((T, MLA_HEADS * MLA_V), BF16)],
        compiler_params=_params(("parallel",)),
        name="mla_prep",
    )(proj, proj, proj, proj, cos_t, sin_t, g_q, g_kv, wq, wqr, wk, wv)


def _attn_tile(q_ref, k_ref, v_ref, o_ref, t, tq, causal):
    r0 = t * tq
    for h in range(MLA_HEADS):
        hs = slice(h * MLA_HEAD_PAD, (h + 1) * MLA_HEAD_PAD)
        vs = slice(h * MLA_V, (h + 1) * MLA_V)
        qh = q_ref[:, hs]
        s_d = jnp.where(causal, _dot_nt(qh, k_ref[r0:r0 + tq, hs]), MASK_VALUE)
        m = _row_max(s_d)
        if t > 0:
            s_o = _dot_nt(qh, k_ref[0:r0, hs])
            m = jnp.maximum(m, _row_max(s_o))
        p_d = jnp.exp2(s_d - m)
        l = _row_sum(p_d)
        acc = _dot(p_d.astype(BF16), v_ref[r0:r0 + tq, vs])
        if t > 0:
            p_o = jnp.exp2(s_o - m)
            l = l + _row_sum(p_o)
            acc = acc + _dot(p_o.astype(BF16), v_ref[0:r0, vs])
        o_ref[:, vs] = (acc / l).astype(o_ref.dtype)


def _attn_kernel(q_ref, k_ref, v_ref, o_ref, *, seq, tq):
    i = pl.program_id(1)
    row = lax.broadcasted_iota(jnp.int32, (tq, tq), 0)
    col = lax.broadcasted_iota(jnp.int32, (tq, tq), 1)
    causal = row >= col
    for t in range(seq // tq):
        pl.when(i == t)(functools.partial(_attn_tile, q_ref, k_ref, v_ref, o_ref, t, tq, causal))


def _attention(q, k, v, B, S, tq=256):
    T = q.shape[0]
    nq = S // tq
    HW = MLA_HEADS * MLA_HEAD_PAD
    HV = MLA_HEADS * MLA_V
    return pl.pallas_call(
        functools.partial(_attn_kernel, seq=S, tq=tq),
        grid=(B, nq),
        in_specs=[pl.BlockSpec((tq, HW), lambda b, i: (b * nq + i, 0)),
                  pl.BlockSpec((S, HW), lambda b, i: (b, 0)),
                  pl.BlockSpec((S, HV), lambda b, i: (b, 0))],
        out_specs=pl.BlockSpec((tq, HV), lambda b, i: (b * nq + i, 0)),
        out_shape=jax.ShapeDtypeStruct((T, HV), BF16),
        compiler_params=_params(("parallel", "arbitrary")),
        name="mla_attention",
    )(q, k, v)


def _shift_rows(x, prev8, k):
    rx = pltpu.roll(x, k, 0)
    cx = pltpu.roll(prev8, k, 0)
    r8 = lax.broadcasted_iota(jnp.int32, (8, x.shape[1]), 0)
    top = jnp.where(r8 < k, cx, rx[0:8])
    return jnp.concatenate([top, rx[8:]], axis=0)


def _gdn_kernel(q_ref, k_ref, v_ref, z_ref, ab_ref, cw_ref, alog_ref, dtb_ref, ng_ref, o_ref,
                carry_ref, state_ref, *, nb):
    si = pl.program_id(1)
    HD = GDN_HEADS * GDN_DK
    NC = SPAN // GDN_CHUNK

    @pl.when(si == 0)
    def _():
        carry_ref[...] = jnp.zeros_like(carry_ref)
        state_ref[...] = jnp.zeros_like(state_ref)

    def conv_silu(x_ref, b, idx):
        x = x_ref[b].astype(F32)
        prev8 = carry_ref[b, :, idx * HD:(idx + 1) * HD]
        w4 = cw_ref[:, idx * HD:(idx + 1) * HD]
        y = x * w4[GDN_CONV - 1:GDN_CONV]
        for kk in range(1, GDN_CONV):
            y = y + _shift_rows(x, prev8, kk) * w4[GDN_CONV - 1 - kk:GDN_CONV - kk]
        carry_ref[b, :, idx * HD:(idx + 1) * HD] = x[SPAN - 8:SPAN]
        return _silu(y)

    row = lax.broadcasted_iota(jnp.int32, (SPAN, SPAN), 0)
    col = lax.broadcasted_iota(jnp.int32, (SPAN, SPAN), 1)
    same_chunk = (row // GDN_CHUNK) == (col // GDN_CHUNK)
    tril = same_chunk & (col <= row)
    stril = same_chunk & (col < row)
    eye = (row == col).astype(F32)
    tril16 = tril.astype(BF16)

    units = []
    for b in range(nb):
        q = conv_silu(q_ref, b, 0)
        k = conv_silu(k_ref, b, 1)
        v = conv_silu(v_ref, b, 2)
        ab = ab_ref[b]
        g_full = -jnp.exp(alog_ref[...]) * _softplus(ab + dtb_ref[...])
        beta_full = _sigmoid(ab)
        gc_full = _exact_left_dot(tril16, g_full)
        gc_t = gc_full.T
        for h in range(GDN_HEADS):
            sl = slice(h * GDN_DK, (h + 1) * GDN_DK)
            qh = q[:, sl]
            kh = k[:, sl]
            qh = qh * lax.rsqrt(jnp.sum(qh * qh, -1, keepdims=True) + RMS_EPS) * (GDN_DK ** -0.5)
            kh = kh * lax.rsqrt(jnp.sum(kh * kh, -1, keepdims=True) + RMS_EPS)
            gc = gc_full[:, h:h + 1]
            beta = beta_full[:, GDN_HEADS + h:GDN_HEADS + h + 1]
            decay = jnp.where(tril, jnp.exp(gc - gc_t[h:h + 1, :]), 0.0)
            k16 = kh.astype(BF16)
            a = jnp.where(stril, beta * _dot_nt(k16, k16) * decay, 0.0)
            qk16 = (_dot_nt(qh.astype(BF16), k16) * decay).astype(BF16)
            egc = jnp.exp(gc)
            units.append(dict(
                b=b, h=h, sl=sl, kh=kh, gc=gc, qk16=qk16, a=a,
                rhs=jnp.concatenate([v[:, sl] * beta, kh * (beta * egc)], axis=1).astype(BF16),
                qg=(qh * egc).astype(BF16)))

    xs = [eye - u["a"] for u in units]
    ps = [u["a"].astype(BF16) for u in units]
    for _ in range(5):
        ps = [_dot(p, p).astype(BF16) for p in ps]
        xs = [x + _dot(x.astype(BF16), p) for x, p in zip(xs, ps)]
    uws = [_dot(x.astype(BF16), u["rhs"]) for x, u in zip(xs, units)]

    states = [state_ref[u["b"], u["h"]] for u in units]
    o_state = [[] for _ in units]
    v_new = [[] for _ in units]
    for c in range(NC):
        cs = slice(c * GDN_CHUNK, (c + 1) * GDN_CHUNK)
        s16 = [s.astype(BF16) for s in states]
        vns = [uw[cs, :GDN_DV] - _dot(uw[cs, GDN_DV:].astype(BF16), s) for uw, s in zip(uws, s16)]
        for i, u in enumerate(units):
            o_state[i].append(_dot(u["qg"][cs], s16[i]))
            v_new[i].append(vns[i])
            g_last = u["gc"][(c + 1) * GDN_CHUNK - 1:(c + 1) * GDN_CHUNK, :]
            kd = (u["kh"][cs] * jnp.exp(g_last - u["gc"][cs])).astype(BF16)
            states[i] = states[i] * jnp.exp(g_last) + _dot_tn(kd, vns[i].astype(BF16))
    for i, u in enumerate(units):
        state_ref[u["b"], u["h"]] = states[i]
        o = jnp.concatenate(o_state[i], axis=0) + _dot(u["qk16"], jnp.concatenate(v_new[i], axis=0).astype(BF16))
        o = _rms(o, ng_ref[...]) * _silu(z_ref[u["b"], :, u["sl"]].astype(F32))
        o_ref[u["b"], :, u["sl"]] = o.astype(o_ref.dtype)


def _gdn(proj16, proj32, conv_w, alog_row, dtb_row, norm_g, nb=2):
    B, S, _ = proj16.shape
    HD = GDN_HEADS * GDN_DK
    blk = lambda w, j: pl.BlockSpec((nb, SPAN, w), lambda b, s, j=j: (b, s, j))
    full = lambda a: pl.BlockSpec(a.shape, lambda b, s: (0,) * a.ndim)
    return pl.pallas_call(
        functools.partial(_gdn_kernel, nb=nb),
        grid=(B // nb, S // SPAN),
        in_specs=[blk(HD, PK_GQ), blk(HD, PK_GK), blk(HD, PK_GV), blk(HD, PK_GZ), blk(LANES, PK32_AB),
                  full(conv_w), full(alog_row), full(dtb_row), full(norm_g)],
        out_specs=blk(HD, 0),
        out_shape=jax.ShapeDtypeStruct((B, S, HD), BF16),
        scratch_shapes=[pltpu.VMEM((nb, 8, 3 * HD), F32), pltpu.VMEM((nb, GDN_HEADS, GDN_DK, GDN_DV), F32)],
        compiler_params=_params(("parallel", "arbitrary")),
        name="gdn",
    )(proj16, proj16, proj16, proj16, proj32, conv_w, alog_row, dtb_row, norm_g)


def _pool_kernel(u_ref, w_ref, sc_ref, o_ref, carry_ref, *, tp):
    si = pl.program_id(1)
    G = POOL_GROUP

    @pl.when(si == 0)
    def _():
        carry_ref[...] = jnp.zeros_like(carry_ref)

    u = u_ref[...].astype(F32)
    ext = jnp.concatenate([carry_ref[...], u], axis=0)
    carry_ref[...] = u[tp - 16:tp]
    t = si * tp + lax.broadcasted_iota(jnp.int32, (tp, 1), 0)
    win_sum = ext
    outs = []
    shift = 1
    for gi, win in enumerate(POOL_WINDOWS):
        while shift < win:
            win_sum = win_sum + pltpu.roll(win_sum, shift, 0)
            shift *= 2
        cnt = jnp.minimum(t + 1, win).astype(F32)
        ug = u[:, gi * G:(gi + 1) * G]
        pooled = win_sum[16:, gi * G:(gi + 1) * G] / cnt - ug
        outs.append(_dot(pooled.astype(BF16), w_ref[gi]))
    y = jnp.concatenate(outs, axis=1) * sc_ref[...]
    o_ref[...] = y.astype(o_ref.dtype)


def _pool(proj, w_group, scale_row, B, S, tp=512):
    T = proj.shape[0]
    ns = S // tp
    W = len(POOL_WINDOWS) * POOL_GROUP
    return pl.pallas_call(
        functools.partial(_pool_kernel, tp=tp),
        grid=(B, ns),
        in_specs=[pl.BlockSpec((tp, W), lambda b, s: (b * ns + s, PK_PU)),
                  pl.BlockSpec(w_group.shape, lambda b, s: (0, 0, 0)),
                  pl.BlockSpec(scale_row.shape, lambda b, s: (0, 0))],
        out_specs=pl.BlockSpec((tp, W), lambda b, s: (b * ns + s, 0)),
        out_shape=jax.ShapeDtypeStruct((T, W), BF16),
        scratch_shapes=[pltpu.VMEM((16, W), F32)],
        compiler_params=_params(("parallel", "arbitrary")),
        name="pool",
    )(proj, w_group, scale_row)


def _mid_broadcast(c, s):
    n = c.shape[0]
    if 2 * s >= 8:
        c3 = c.reshape(n // (2 * s), 2 * s, c.shape[1])
        mid = jnp.broadcast_to(c3[:, s - 1:s, :], c3.shape)
        return mid.reshape(c.shape)
    pos = lax.broadcasted_iota(jnp.int32, c.shape, 0) % (2 * s)
    out = c
    for p in range(2 * s):
        d = p - (s - 1)
        if d != 0:
            out = jnp.where(pos == p, pltpu.roll(c, d % n, 0), out)
    return out


def _hgrn_kernel(q_ref, f_ref, i_ref, g_ref, lb_ref, ng_ref, o_ref, state_ref):
    si = pl.program_id(1)

    @pl.when(si == 0)
    def _():
        state_ref[...] = jnp.zeros_like(state_ref)

    lb = lb_ref[...]
    q = _silu(q_ref[...].astype(F32))
    fp = f_ref[...]
    log_f = jnp.log(lb + (1.0 - lb) * _sigmoid(fp))
    k = (1.0 - lb) * _sigmoid(-fp)
    v16 = i_ref[...]

    row = lax.broadcasted_iota(jnp.int32, (SPAN, SPAN), 0)
    col = lax.broadcasted_iota(jnp.int32, (SPAN, SPAN), 1)
    c = _exact_left_dot((col <= row).astype(BF16), log_f)
    c_last = c[SPAN - 1:SPAN, :]

    q16 = q.astype(BF16)
    k16 = k.astype(BF16)
    heads = [slice(h * HGRN_DK, (h + 1) * HGRN_DK) for h in range(HGRN_HEADS)]
    halves = [slice(0, HALF), slice(HALF, SPAN)]
    ri = lax.broadcasted_iota(jnp.int32, (HALF, HALF), 0)
    ci = lax.broadcasted_iota(jnp.int32, (HALF, HALF), 1)
    att = [[jnp.where(ri == ci, _dot_nt(q16[r, sl], k16[r, sl]), 0.0) for sl in heads] for r in halves]
    cross = None
    s = 1
    while s < SPAN:
        w16 = jnp.exp(-jnp.abs(c - _mid_broadcast(c, s))).astype(BF16)
        ql = q16 * w16
        kl = k16 * w16
        if s < HALF:
            bi = ri // s
            bj = ci // s
            region = ((bi - bj) * 2 + (bj & 1)) == 2
            for d, r in enumerate(halves):
                for h, sl in enumerate(heads):
                    att[d][h] = jnp.where(region, _dot_nt(ql[r, sl], kl[r, sl]), att[d][h])
        else:
            cross = [_dot_nt(ql[halves[1], sl], kl[halves[0], sl]) for sl in heads]
        s *= 2

    qe = (q * jnp.exp(c)).astype(BF16)
    ke = (k * jnp.exp(c_last - c)).astype(BF16)
    e_last = jnp.exp(c_last)
    for h, sl in enumerate(heads):
        st = state_ref[h]
        v_lo = v16[halves[0], sl]
        v_hi = v16[halves[1], sl]
        o_lo = _dot(att[0][h].astype(BF16), v_lo)
        o_hi = _dot(cross[h].astype(BF16), v_lo) + _dot(att[1][h].astype(BF16), v_hi)
        o = jnp.concatenate([o_lo, o_hi], axis=0) + _dot_nt(qe[:, sl], st.astype(BF16))
        state_ref[h] = st * e_last[:, sl] + _dot_tn(v16[:, sl], ke[:, sl])
        o = _rms(o, ng_ref[...]) * _sigmoid(g_ref[:, sl].astype(F32))
        o_ref[:, sl] = o.astype(o_ref.dtype)


def _hgrn(proj16, proj32, lb_row, norm_g, B, S):
    T = proj16.shape[0]
    ns = S // SPAN
    HD = HGRN_HEADS * HGRN_DK
    blk = lambda j: pl.BlockSpec((SPAN, HD), lambda b, s, j=j: (b * ns + s, j))
    full = lambda a: pl.BlockSpec(a.shape, lambda b, s: (0,) * a.ndim)
    return pl.pallas_call(
        _hgrn_kernel,
        grid=(B, ns),
        in_specs=[blk(PK_HQ), blk(PK32_HF), blk(PK_HI), blk(PK_HG), full(lb_row), full(norm_g)],
        out_specs=blk(0),
        out_shape=jax.ShapeDtypeStruct((T, HD), BF16),
        scratch_shapes=[pltpu.VMEM((HGRN_HEADS, HGRN_DV, HGRN_DK), F32)],
        compiler_params=_params(("parallel", "arbitrary")),
        name="hgrn2",
    )(proj16, proj32, proj16, proj16, lb_row, norm_g)


def _merge_kernel(x_ref, ya_ref, yb_ref, yc_ref, yd_ref, wg_ref, wb_ref, wo_ref, g_ref, b_ref, o_ref):
    x = x_ref[...]
    x16 = x.astype(BF16)
    merged = None
    for m, y_ref in enumerate((ya_ref, yb_ref, yc_ref, yd_ref)):
        gate = _sigmoid(_dot(x16, wg_ref[:, m * D_MODEL:(m + 1) * D_MODEL]))
        term = gate * _dot(y_ref[...], wb_ref[m])
        merged = term if merged is None else merged + term
    h = DEEPNORM_ALPHA * x + _dot(merged.astype(BF16), wo_ref[...])
    o_ref[...] = _layer_norm(h, g_ref[...], b_ref[...])


def _merge(x2, ya, yb, yc, yd, w_gate, w_branch, w_out, ln_g, ln_b, tm=512):
    T, D = x2.shape
    row = lambda w: pl.BlockSpec((tm, w), lambda i: (i, 0))
    full = lambda a: pl.BlockSpec(a.shape, lambda i: (0,) * a.ndim, pipeline_mode=pl.Buffered(1))
    return pl.pallas_call(
        _merge_kernel,
        grid=(T // tm,),
        in_specs=[row(D), row(BRANCH_WIDTH), row(BRANCH_WIDTH), row(BRANCH_WIDTH), row(BRANCH_WIDTH),
                  full(w_gate), full(w_branch), full(w_out), full(ln_g), full(ln_b)],
        out_specs=row(D),
        out_shape=jax.ShapeDtypeStruct((T, D), F32),
        compiler_params=_params(("parallel",)),
        name="merge",
    )(x2, ya, yb, yc, yd, w_gate, w_branch, w_out, ln_g, ln_b)


def _ffn_kernel(x_ref, wg_ref, wu_ref, wd_ref, g_ref, b_ref, o_ref):
    x = x_ref[...]
    x16 = x.astype(BF16)
    h = _silu(_dot(x16, wg_ref[...])) * _dot(x16, wu_ref[...])
    y = DEEPNORM_ALPHA * x + _dot(h.astype(BF16), wd_ref[...])
    o_ref[...] = _layer_norm(y, g_ref[...], b_ref[...])


def _ffn(x2, wg, wu, wd, ln_g, ln_b, tm=512):
    T, D = x2.shape
    row = pl.BlockSpec((tm, D), lambda i: (i, 0))
    full = lambda a: pl.BlockSpec(a.shape, lambda i: (0,) * a.ndim, pipeline_mode=pl.Buffered(1))
    return pl.pallas_call(
        _ffn_kernel,
        grid=(T // tm,),
        in_specs=[row, full(wg), full(wu), full(wd), full(ln_g), full(ln_b)],
        out_specs=row,
        out_shape=jax.ShapeDtypeStruct((T, D), F32),
        compiler_params=_params(("parallel",)),
        name="ffn_dense",
    )(x2, wg, wu, wd, ln_g, ln_b)


def _router_kernel(x_ref, w_ref, meta_ref, gate_ref, cnt_ref, base_ref, *, tm):
    i = pl.program_id(0)

    @pl.when(i == 0)
    def _():
        base_ref[...] = jnp.zeros_like(base_ref)

    xh, xm, _ = _split3(x_ref[...])
    wh, wm, _ = _split3(w_ref[...])
    logits = _dot(xh, wh) + _dot(xh, wm) + _dot(xm, wh)
    lane = lax.broadcasted_iota(jnp.int32, (tm, LANES), 1)
    neg = jnp.float32(-jnp.inf)
    logits = jnp.where(lane < N_EXPERTS, logits, neg)
    v1 = jnp.max(logits, -1, keepdims=True)
    e1 = jnp.min(jnp.where(logits == v1, lane, LANES), -1, keepdims=True)
    rest = jnp.where(lane == e1, neg, logits)
    v2 = jnp.max(rest, -1, keepdims=True)
    e2 = jnp.min(jnp.where(rest == v2, lane, LANES), -1, keepdims=True)
    ex = jnp.exp(v2 - v1)
    w1 = 1.0 / (1.0 + ex)
    w2 = ex / (1.0 + ex)
    onehot = ((lane == e1) | (lane == e2)).astype(F32)
    r = lax.broadcasted_iota(jnp.int32, (tm, tm), 0)
    cc = lax.broadcasted_iota(jnp.int32, (tm, tm), 1)
    before = _dot((cc < r).astype(BF16), onehot.astype(BF16)) + base_ref[0:1, :]
    rank1 = jnp.sum(jnp.where(lane == e1, before, 0.0), -1, keepdims=True)
    rank2 = jnp.sum(jnp.where(lane == e2, before, 0.0), -1, keepdims=True)
    total = base_ref[0:1, :] + jnp.sum(onehot, 0, keepdims=True)
    base_ref[...] = jnp.broadcast_to(total, base_ref.shape)
    cnt_ref[...] = jnp.broadcast_to(total, cnt_ref.shape).astype(jnp.int32)
    meta = jnp.where(lane == 0, e1, jnp.where(lane == 1, e2, jnp.where(
        lane == 2, rank1.astype(jnp.int32), jnp.where(lane == 3, rank2.astype(jnp.int32), 0))))
    meta_ref[...] = meta
    gate_ref[...] = jnp.where(lane == 0, w1, jnp.where(lane == 1, w2, 0.0))


def _router(x2, w_router_pad, tm=512):
    T, D = x2.shape
    return pl.pallas_call(
        functools.partial(_router_kernel, tm=tm),
        grid=(T // tm,),
        in_specs=[pl.BlockSpec((tm, D), lambda i: (i, 0)),
                  pl.BlockSpec(w_router_pad.shape, lambda i: (0, 0))],
        out_specs=[pl.BlockSpec((tm, LANES), lambda i: (i, 0)),
                   pl.BlockSpec((tm, LANES), lambda i: (i, 0)),
                   pl.BlockSpec((8, LANES), lambda i: (0, 0))],
        out_shape=[jax.ShapeDtypeStruct((T, LANES), jnp.int32),
                   jax.ShapeDtypeStruct((T, LANES), F32),
                   jax.ShapeDtypeStruct((8, LANES), jnp.int32)],
        scratch_shapes=[pltpu.VMEM((8, LANES), F32)],
        compiler_params=_params(("arbitrary",)),
        name="moe_router",
    )(x2, w_router_pad)


def _dispatch_kernel(dest_ref, x_ref, xb_in_ref, xb_ref, sem, *, td):
    del xb_in_ref

    def start(r, c):
        for slot in range(TOP_K):
            pltpu.make_async_copy(x_ref.at[pl.ds(r, 1)],
                                  xb_ref.at[pl.ds(dest_ref[0, 0, slot * td + r], 1)], sem).start()
        return c

    lax.fori_loop(0, td, start, 0, unroll=8)
    pltpu.make_async_copy(xb_ref.at[pl.ds(0, TOP_K * td)], xb_ref.at[pl.ds(0, TOP_K * td)], sem).wait()


def _dispatch(x2, dest_blocks, n_rows, td=256):
    T, D = x2.shape
    xb0 = jnp.zeros((n_rows, D), F32)
    return pl.pallas_call(
        functools.partial(_dispatch_kernel, td=td),
        grid=(T // td,),
        in_specs=[pl.BlockSpec((1, 1, TOP_K * td), lambda i: (i, 0, 0), memory_space=pltpu.SMEM),
                  pl.BlockSpec((td, D), lambda i: (i, 0)),
                  pl.BlockSpec(memory_space=pl.ANY)],
        out_specs=pl.BlockSpec(memory_space=pl.ANY),
        out_shape=jax.ShapeDtypeStruct((n_rows, D), F32),
        scratch_shapes=[pltpu.SemaphoreType.DMA(())],
        input_output_aliases={2: 0},
        compiler_params=_params(("arbitrary",)),
        name="moe_dispatch",
    )(dest_blocks, x2, xb0)


def _expert_kernel(blk_exp_ref, n_used_ref, x_ref, wg_ref, wu_ref, wd_ref, o_ref, acc_ref):
    i = pl.program_id(0)
    j = pl.program_id(1)

    @pl.when(i < n_used_ref[0])
    def _():
        x16 = x_ref[...].astype(BF16)
        h = _silu(_dot(x16, wg_ref[0])) * _dot(x16, wu_ref[0])
        y = _dot(h.astype(BF16), wd_ref[0])

        @pl.when(j == 0)
        def _():
            acc_ref[...] = y

        @pl.when(j > 0)
        def _():
            acc_ref[...] += y

        @pl.when(j == pl.num_programs(1) - 1)
        def _():
            o_ref[...] = acc_ref[...]

    @pl.when(i >= n_used_ref[0])
    def _():
        o_ref[...] = jnp.zeros_like(o_ref)


def _experts(xb, blk_exp, n_used, wg, wu, wd, tf=1792):
    n_rows, D = xb.shape
    n_blk = n_rows // MOE_ROWS
    nf = D_FF_EXPERT // tf

    def rows(i, j, be, nu):
        return (jnp.minimum(i, nu[0] - 1), 0)

    grid_spec = pltpu.PrefetchScalarGridSpec(
        num_scalar_prefetch=2,
        grid=(n_blk, nf),
        in_specs=[pl.BlockSpec((MOE_ROWS, D), rows),
                  pl.BlockSpec((1, D, tf), lambda i, j, be, nu: (be[jnp.minimum(i, nu[0] - 1)], 0, j)),
                  pl.BlockSpec((1, D, tf), lambda i, j, be, nu: (be[jnp.minimum(i, nu[0] - 1)], 0, j)),
                  pl.BlockSpec((1, tf, D), lambda i, j, be, nu: (be[jnp.minimum(i, nu[0] - 1)], j, 0))],
        out_specs=pl.BlockSpec((MOE_ROWS, D), lambda i, j, be, nu: (i, 0)),
        scratch_shapes=[pltpu.VMEM((MOE_ROWS, D), F32)],
    )
    return pl.pallas_call(
        _expert_kernel,
        grid_spec=grid_spec,
        out_shape=jax.ShapeDtypeStruct((n_rows, D), F32),
        compiler_params=_params(("arbitrary", "arbitrary")),
        name="moe_experts",
    )(blk_exp, n_used, xb, wg, wu, wd)


def _combine_kernel(dest_ref, dnext_ref, x_ref, gate_ref, yb_ref, g_ref, b_ref, o_ref, buf_ref, sem, *, tc):
    i = pl.program_id(0)
    n = pl.num_programs(0)

    def issue(d_ref, slot):
        def start(r, c):
            for kk in range(TOP_K):
                pltpu.make_async_copy(yb_ref.at[pl.ds(d_ref[0, 0, kk * tc + r], 1)],
                                      buf_ref.at[slot, kk, pl.ds(r, 1)], sem.at[slot]).start()
            return c
        lax.fori_loop(0, tc, start, 0, unroll=8)

    @pl.when(i == 0)
    def _():
        issue(dest_ref, 0)

    @pl.when(i + 1 < n)
    def _():
        issue(dnext_ref, (i + 1) % 2)

    slot = i % 2
    pltpu.make_async_copy(buf_ref.at[slot], buf_ref.at[slot], sem.at[slot]).wait()
    gate = gate_ref[...]
    ffn = buf_ref[slot, 0] * gate[:, 0:1] + buf_ref[slot, 1] * gate[:, 1:2]
    o_ref[...] = _layer_norm(DEEPNORM_ALPHA * x_ref[...] + ffn, g_ref[...], b_ref[...])


def _combine(x2, gates, yb, dest_blocks, ln_g, ln_b, tc=256):
    T, D = x2.shape
    n = T // tc
    full = lambda a: pl.BlockSpec(a.shape, lambda i: (0,) * a.ndim)
    smem = lambda imap: pl.BlockSpec((1, 1, TOP_K * tc), imap, memory_space=pltpu.SMEM)
    return pl.pallas_call(
        functools.partial(_combine_kernel, tc=tc),
        grid=(n,),
        in_specs=[smem(lambda i: (i, 0, 0)), smem(lambda i: (jnp.minimum(i + 1, n - 1), 0, 0)),
                  pl.BlockSpec((tc, D), lambda i: (i, 0)),
                  pl.BlockSpec((tc, LANES), lambda i: (i, 0)),
                  pl.BlockSpec(memory_space=pl.ANY), full(ln_g), full(ln_b)],
        out_specs=pl.BlockSpec((tc, D), lambda i: (i, 0)),
        out_shape=jax.ShapeDtypeStruct((T, D), F32),
        scratch_shapes=[pltpu.VMEM((2, TOP_K, tc, D), F32), pltpu.SemaphoreType.DMA((2,))],
        compiler_params=_params(("arbitrary",)),
        name="moe_combine",
    )(dest_blocks, dest_blocks, x2, gates, yb, ln_g, ln_b)


def _moe(x2, w_router, wg, wu, wd, ln_g, ln_b, tok=256):
    T, D = x2.shape
    w_router_pad = jnp.pad(w_router.astype(F32), ((0, 0), (0, LANES - N_EXPERTS)))
    meta, gates, counts = _router(x2, w_router_pad)
    counts = counts[0, :N_EXPERTS]
    padded = (counts + MOE_ROWS - 1) // MOE_ROWS * MOE_ROWS
    pends = jnp.cumsum(padded)
    pstarts = pends - padded
    dest = pstarts[meta[:, 0:2]] + meta[:, 2:4]
    dest_blocks = dest.reshape(T // tok, tok, 2).transpose(0, 2, 1).reshape(T // tok, 1, 2 * tok)
    n_blk = (T * TOP_K + MOE_ROWS - 1) // MOE_ROWS + N_EXPERTS
    n_rows = n_blk * MOE_ROWS
    blk_exp = jnp.minimum(jnp.searchsorted(pends, jnp.arange(n_blk, dtype=jnp.int32) * MOE_ROWS, side='right'),
                          N_EXPERTS - 1).astype(jnp.int32)
    n_used = (pends[-1:] // MOE_ROWS).astype(jnp.int32)
    xb = _dispatch(x2, dest_blocks, n_rows, td=tok)
    yb = _experts(xb, blk_exp, n_used, wg, wu, wd)
    return _combine(x2, gates, yb, dest_blocks, ln_g, ln_b, tc=tok)


def _rot_half_cols(w):
    half = w.shape[-1] // 2
    return jnp.concatenate([-w[..., half:], w[..., :half]], axis=-1)


def _pack_w_in(w):
    D = w.shape[0]
    kr = w[:, 384:448]
    z64 = jnp.zeros((D, 64), w.dtype)
    w16 = jnp.concatenate([
        w[:, 448:2496],
        w[:, 2504:3016],
        w[:, 3016:3528],
        w[:, 4040:5064],
        w[:, 0:256],
        w[:, 256:384],
        kr, z64,
        _rot_half_cols(kr), z64,
    ], axis=1)
    w32 = jnp.concatenate([w[:, 3528:4040], w[:, 2496:2504], jnp.zeros((D, LANES - 8), w.dtype)], axis=1)
    return w16.astype(BF16), w32.astype(BF16), w[:, 5064:].astype(BF16)


def _pack_mla(w_uq, w_ukv):
    R = w_uq.shape[0]
    z64 = jnp.zeros((R, 64), w_uq.dtype)
    wq, wqr = [], []
    for h in range(MLA_HEADS):
        o = h * (MLA_NOPE + MLA_ROPE)
        rope_cols = w_uq[:, o + MLA_NOPE:o + MLA_NOPE + MLA_ROPE]
        wq += [w_uq[:, o:o + MLA_NOPE], rope_cols, z64]
        wqr += [_rot_half_cols(rope_cols), z64]
    kv = w_ukv.reshape(w_ukv.shape[0], MLA_HEADS, MLA_NOPE + MLA_V)
    wk = kv[:, :, :MLA_NOPE].reshape(w_ukv.shape[0], MLA_HEADS * MLA_NOPE)
    wv = kv[:, :, MLA_NOPE:].reshape(w_ukv.shape[0], MLA_HEADS * MLA_V)
    return (jnp.concatenate(wq, 1).astype(BF16), jnp.concatenate(wqr, 1).astype(BF16),
            wk.astype(BF16), wv.astype(BF16))


def _lane_row(v, offset=0):
    v = v.astype(F32)
    return jnp.pad(v, (offset, LANES - offset - v.shape[0]))[None, :]


def kernel(x, positions, w_in, mla_q_norm, mla_w_uq, mla_kv_norm, mla_w_ukv, gdn_conv, gdn_a_log, gdn_dt_bias, gdn_norm, pool_w, pool_scale, hgrn_lb_logits, hgrn_norm, w_branch, w_out, ln_mix_g, ln_mix_b, ffn_w_gate, ffn_w_up, ffn_w_down, moe_router, moe_w_gate, moe_w_up, moe_w_down, ln_ffn_g, ln_ffn_b):
    B, S, D = x.shape
    T = B * S
    x2 = x.reshape(T, D)

    half = MLA_ROPE // 2
    inv = ROPE_THETA ** (-jnp.arange(half, dtype=F32) / half)
    inv_row = jnp.concatenate([inv, inv, jnp.zeros((LANES - MLA_ROPE,), F32)])[None, :]
    cos_t, sin_t = _rope_tables(positions.reshape(T, 1).astype(jnp.int32), inv_row)

    p_lb = jax.nn.softmax(hgrn_lb_logits.astype(F32), axis=0)
    lower_bounds = jnp.cumsum(p_lb, axis=0) - p_lb[0]
    row2 = lambda v: v.astype(F32)[None, :]

    for l in range(DEPTH):
        w16, w32, w_gate = _pack_w_in(w_in[l])
        proj16, proj32 = _inproj(x2, w16, w32)
        wq, wqr, wk, wv = _pack_mla(mla_w_uq[l], mla_w_ukv[l])
        q, k, v = _mla_prep(proj16, cos_t, sin_t, row2(mla_q_norm[l]), row2(mla_kv_norm[l]), wq, wqr, wk, wv)
        y_a = _attention(q, k, v, B, S)
        y_b = _gdn(proj16.reshape(B, S, PK16_WIDTH), proj32.reshape(B, S, PK32_WIDTH), gdn_conv[l].astype(F32),
                   _lane_row(gdn_a_log[l]), _lane_row(gdn_dt_bias[l]), row2(gdn_norm[l])
                   ).reshape(T, GDN_HEADS * GDN_DV)
        y_c = _pool(proj16, pool_w[l].astype(BF16), row2(pool_scale[l]), B, S)
        y_d = _hgrn(proj16, proj32, row2(lower_bounds[l]), row2(hgrn_norm[l]), B, S)
        x2 = _merge(x2, y_a, y_b, y_c, y_d, w_gate, w_branch[l].astype(BF16), w_out[l].astype(BF16),
                    row2(ln_mix_g[l]), row2(ln_mix_b[l]))
        j = l // 2
        if l % 2 == 0:
            x2 = _ffn(x2, ffn_w_gate[j].astype(BF16), ffn_w_up[j].astype(BF16), ffn_w_down[j].astype(BF16),
                      row2(ln_ffn_g[l]), row2(ln_ffn_b[l]))
        else:
            x2 = _moe(x2, moe_router[j], moe_w_gate[j].astype(BF16), moe_w_up[j].astype(BF16),
                      moe_w_down[j].astype(BF16), row2(ln_ffn_g[l]), row2(ln_ffn_b[l]))
    return x2.reshape(B, S, D)
```

```python
import functools
import math

import jax
import jax.numpy as jnp
from jax import lax
from jax.experimental import pallas as pl
from jax.experimental.pallas import tpu as pltpu

F32 = jnp.float32
BF16 = jnp.bfloat16

D_MODEL = 1024
DEPTH = 2
MLA_HEADS = 4
MLA_Q_RANK = 256
MLA_KV_RANK = 128
MLA_NOPE = 128
MLA_ROPE = 64
MLA_V = 128
ROPE_THETA = 10000.0
MASK_VALUE = -1e30
GDN_HEADS = 4
GDN_DK = 128
GDN_DV = 128
GDN_CONV = 4
GDN_CHUNK = 64
POOL_WINDOWS = (2, 4, 8, 16)
POOL_GROUP = 128
HGRN_HEADS = 4
HGRN_DK = 128
HGRN_DV = 128
N_BRANCH = 4
BRANCH_WIDTH = 512
D_FF = 2816
N_EXPERTS = 8
TOP_K = 2
D_FF_EXPERT = 3584
DEEPNORM_ALPHA = (2 * DEPTH) ** 0.25
LN_EPS = 1e-5
RMS_EPS = 1e-6
LOG2E = math.log2(math.e)

LANES = 128
MLA_HEAD_PAD = 256
SPAN = 256
HALF = SPAN // 2
MOE_ROWS = 512
VMEM_LIMIT = 56 * 1024 * 1024

PK_GQ, PK_GK, PK_GV, PK_GZ, PK_PU, PK_HQ, PK_HI, PK_HG = range(8)
PK_CQ = 16
PK_CKV = 34
PK_KRA = 35
PK_KRB = 36
PK16_WIDTH = 37 * LANES
PK32_HF = 0
PK32_AB = 4
PK32_WIDTH = 5 * LANES


def _params(sem, vmem=VMEM_LIMIT):
    return pltpu.CompilerParams(dimension_semantics=sem, vmem_limit_bytes=vmem)


def _dot(a, b):
    return jnp.dot(a, b, preferred_element_type=F32)


def _dot_nt(a, b):
    return lax.dot_general(a, b, (((1,), (1,)), ((), ())), preferred_element_type=F32)


def _dot_tn(a, b):
    return lax.dot_general(a, b, (((0,), (0,)), ((), ())), preferred_element_type=F32)


def _split3(x):
    hi = x.astype(BF16)
    r1 = x - hi.astype(F32)
    mid = r1.astype(BF16)
    lo = (r1 - mid.astype(F32)).astype(BF16)
    return hi, mid, lo


def _exact_left_dot(m01, x):
    hi, mid, lo = _split3(x)
    return _dot(m01, hi) + _dot(m01, mid) + _dot(m01, lo)


def _sigmoid(x):
    return 1.0 / (1.0 + jnp.exp(-x))


def _silu(x):
    return x * _sigmoid(x)


def _softplus(x):
    return jnp.maximum(x, 0.0) + jnp.log(1.0 + jnp.exp(-jnp.abs(x)))


def _layer_norm(x, g, b):
    mu = jnp.mean(x, -1, keepdims=True)
    xc = x - mu
    var = jnp.mean(xc * xc, -1, keepdims=True)
    return xc * lax.rsqrt(var + LN_EPS) * g + b


def _rms(x, g):
    return x * lax.rsqrt(jnp.mean(x * x, -1, keepdims=True) + RMS_EPS) * g


def _lane_blocks(s):
    return [s[:, c * LANES:(c + 1) * LANES] for c in range(s.shape[1] // LANES)]


def _row_max(s):
    return jnp.max(functools.reduce(jnp.maximum, _lane_blocks(s)), -1, keepdims=True)


def _row_sum(s):
    return jnp.sum(functools.reduce(jnp.add, _lane_blocks(s)), -1, keepdims=True)


def _inproj_kernel(x_ref, w16_ref, w32_ref, o16_ref, o32_ref):
    x16 = x_ref[...].astype(BF16)
    o16_ref[...] = _dot(x16, w16_ref[...]).astype(o16_ref.dtype)
    o32_ref[...] = _dot(x16, w32_ref[...])


def _inproj(x2, w16, w32, tm=512):
    T, D = x2.shape
    full = lambda a: pl.BlockSpec(a.shape, lambda i: (0, 0), pipeline_mode=pl.Buffered(1))
    return pl.pallas_call(
        _inproj_kernel,
        grid=(T // tm,),
        in_specs=[pl.BlockSpec((tm, D), lambda i: (i, 0)), full(w16), full(w32)],
        out_specs=[pl.BlockSpec((tm, w16.shape[1]), lambda i: (i, 0)),
                   pl.BlockSpec((tm, w32.shape[1]), lambda i: (i, 0))],
        out_shape=[jax.ShapeDtypeStruct((T, w16.shape[1]), BF16),
                   jax.ShapeDtypeStruct((T, w32.shape[1]), F32)],
        compiler_params=_params(("parallel",)),
        name="inproj",
    )(x2, w16, w32)


def _rope_table_kernel(pos_ref, inv_ref, c_ref, s_ref):
    ang = pos_ref[...].astype(F32) * inv_ref[...]
    c_ref[...] = jnp.cos(ang)
    s_ref[...] = jnp.sin(ang)


def _rope_tables(pos_col, inv_row, tm=1024):
    T = pos_col.shape[0]
    return pl.pallas_call(
        _rope_table_kernel,
        grid=(T // tm,),
        in_specs=[pl.BlockSpec((tm, 1), lambda i: (i, 0)),
                  pl.BlockSpec((1, LANES), lambda i: (0, 0))],
        out_specs=[pl.BlockSpec((tm, LANES), lambda i: (i, 0))] * 2,
        out_shape=[jax.ShapeDtypeStruct((T, LANES), F32)] * 2,
        compiler_params=_params(("parallel",)),
        name="rope_tables",
    )(pos_col, inv_row)


def _mla_prep_rows(rs, cq_ref, ckv_ref, kra_ref, krb_ref, c_ref, s_ref, gq_ref, gkv_ref,
                   wq_ref, wqr_ref, wk_ref, wv_ref, q_s, k_s, v_s):
    scale = (MLA_NOPE + MLA_ROPE) ** -0.5 * LOG2E
    cos = c_ref[rs, :]
    sin = s_ref[rs, :]
    nq = _rms(cq_ref[rs, :].astype(F32), gq_ref[...]).astype(BF16)
    q_pre = _dot(nq, wq_ref[...])
    q_rot = _dot(nq, wqr_ref[...])
    nkv = _rms(ckv_ref[rs, :].astype(F32), gkv_ref[...]).astype(BF16)
    k_nope = _dot(nkv, wk_ref[...])
    v_s[rs, :] = _dot(nkv, wv_ref[...]).astype(v_s.dtype)
    k_rope = (kra_ref[rs, :].astype(F32) * cos + krb_ref[rs, :].astype(F32) * sin).astype(k_s.dtype)
    for h in range(MLA_HEADS):
        o = h * MLA_HEAD_PAD
        q_s[rs, o:o + LANES] = (q_pre[:, o:o + LANES] * scale).astype(q_s.dtype)
        q_s[rs, o + LANES:o + 2 * LANES] = (
            (q_pre[:, o + LANES:o + 2 * LANES] * cos + q_rot[:, h * LANES:(h + 1) * LANES] * sin) * scale
        ).astype(q_s.dtype)
        k_s[rs, o:o + LANES] = k_nope[:, h * LANES:(h + 1) * LANES].astype(k_s.dtype)
        k_s[rs, o + LANES:o + 2 * LANES] = k_rope


def _attn_tile(q_s, k_s, v_s, o_ref, t, tq, causal):
    r0 = t * tq
    for h in range(MLA_HEADS):
        hs = slice(h * MLA_HEAD_PAD, (h + 1) * MLA_HEAD_PAD)
        vs = slice(h * MLA_V, (h + 1) * MLA_V)
        qh = q_s[r0:r0 + tq, hs]
        s_d = jnp.where(causal, _dot_nt(qh, k_s[r0:r0 + tq, hs]), MASK_VALUE)
        m = _row_max(s_d)
        if t > 0:
            s_o = _dot_nt(qh, k_s[0:r0, hs])
            m = jnp.maximum(m, _row_max(s_o))
        p_d = jnp.exp2(s_d - m)
        l = _row_sum(p_d)
        acc = _dot(p_d.astype(BF16), v_s[r0:r0 + tq, vs])
        if t > 0:
            p_o = jnp.exp2(s_o - m)
            l = l + _row_sum(p_o)
            acc = acc + _dot(p_o.astype(BF16), v_s[0:r0, vs])
        o_ref[:, vs] = (acc / l).astype(o_ref.dtype)


def _mla_kernel(*refs, seq, tq, tprep):
    prep_refs, o_ref, (q_s, k_s, v_s) = refs[:12], refs[12], refs[13:]
    i = pl.program_id(1)

    @pl.when(i == 0)
    def _():
        for c in range(seq // tprep):
            _mla_prep_rows(slice(c * tprep, (c + 1) * tprep), *prep_refs, q_s, k_s, v_s)

    row = lax.broadcasted_iota(jnp.int32, (tq, tq), 0)
    col = lax.broadcasted_iota(jnp.int32, (tq, tq), 1)
    causal = row >= col
    for t in range(seq // tq):
        pl.when(i == t)(functools.partial(_attn_tile, q_s, k_s, v_s, o_ref, t, tq, causal))


def _mla(proj, cos_t, sin_t, g_q, g_kv, wq, wqr, wk, wv, B, S, tq=256, tprep=512):
    T = proj.shape[0]
    nq = S // tq
    HW = MLA_HEADS * MLA_HEAD_PAD
    HV = MLA_HEADS * MLA_V
    seq = lambda w, j: pl.BlockSpec((S, w), lambda b, i, j=j: (b, j))
    full = lambda a: pl.BlockSpec(a.shape, lambda b, i: (0,) * a.ndim)
    return pl.pallas_call(
        functools.partial(_mla_kernel, seq=S, tq=tq, tprep=min(tprep, S)),
        grid=(B, nq),
        in_specs=[seq(256, PK_CQ), seq(128, PK_CKV), seq(128, PK_KRA), seq(128, PK_KRB),
                  seq(128, 0), seq(128, 0), full(g_q), full(g_kv), full(wq), full(wqr), full(wk), full(wv)],
        out_specs=pl.BlockSpec((tq, HV), lambda b, i: (b * nq + i, 0)),
        out_shape=jax.ShapeDtypeStruct((T, HV), BF16),
        scratch_shapes=[pltpu.VMEM((S, HW), BF16), pltpu.VMEM((S, HW), BF16), pltpu.VMEM((S, HV), BF16)],
        compiler_params=_params(("parallel", "arbitrary")),
        name="mla",
    )(proj, proj, proj, proj, cos_t, sin_t, g_q, g_kv, wq, wqr, wk, wv)


def _gdn_kernel(q_ref, k_ref, v_ref, z_ref, ab_ref, cw_ref, alog_ref, dtb_ref, ng_ref, o_ref,
                carry_ref, state_ref, *, nb):
    si = pl.program_id(1)
    HD = GDN_HEADS * GDN_DK
    NC = SPAN // GDN_CHUNK

    @pl.when(si == 0)
    def _():
        carry_ref[...] = jnp.zeros_like(carry_ref)
        state_ref[...] = jnp.zeros_like(state_ref)

    def conv_silu(x_ref, b, idx):
        x = x_ref[b].astype(F32)
        carry_ref[b, idx, 8:SPAN + 8, :] = x
        w4 = cw_ref[:, idx * HD:(idx + 1) * HD]
        y = x * w4[GDN_CONV - 1:GDN_CONV]
        for kk in range(1, GDN_CONV):
            y = y + carry_ref[b, idx, 8 - kk:8 - kk + SPAN, :] * w4[GDN_CONV - 1 - kk:GDN_CONV - kk]
        carry_ref[b, idx, 0:8, :] = x[SPAN - 8:SPAN]
        return _silu(y)

    row = lax.broadcasted_iota(jnp.int32, (SPAN, SPAN), 0)
    col = lax.broadcasted_iota(jnp.int32, (SPAN, SPAN), 1)
    same_chunk = (row // GDN_CHUNK) == (col // GDN_CHUNK)
    tril = same_chunk & (col <= row)
    stril = same_chunk & (col < row)
    eye = (row == col).astype(F32)
    tril16 = tril.astype(BF16)

    units = []
    for b in range(nb):
        q = conv_silu(q_ref, b, 0)
        k = conv_silu(k_ref, b, 1)
        v = conv_silu(v_ref, b, 2)
        ab = ab_ref[b]
        g_full = -jnp.exp(alog_ref[...]) * _softplus(ab + dtb_ref[...])
        beta_full = _sigmoid(ab)
        gc_full = _exact_left_dot(tril16, g_full)
        gc_t = gc_full.T
        for h in range(GDN_HEADS):
            sl = slice(h * GDN_DK, (h + 1) * GDN_DK)
            qh = q[:, sl]
            kh = k[:, sl]
            qh = qh * lax.rsqrt(jnp.sum(qh * qh, -1, keepdims=True) + RMS_EPS) * (GDN_DK ** -0.5)
            kh = kh * lax.rsqrt(jnp.sum(kh * kh, -1, keepdims=True) + RMS_EPS)
            gc = gc_full[:, h:h + 1]
            beta = beta_full[:, GDN_HEADS + h:GDN_HEADS + h + 1]
            decay = jnp.where(tril, jnp.exp(gc - gc_t[h:h + 1, :]), 0.0)
            k16 = kh.astype(BF16)
            a = jnp.where(stril, beta * _dot_nt(k16, k16) * decay, 0.0)
            qk16 = (_dot_nt(qh.astype(BF16), k16) * decay).astype(BF16)
            egc = jnp.exp(gc)
            units.append(dict(
                b=b, h=h, sl=sl, kh=kh, gc=gc, qk16=qk16, a=a,
                rhs=jnp.concatenate([v[:, sl] * beta, kh * (beta * egc)], axis=1).astype(BF16),
                qg=(qh * egc).astype(BF16)))

    xs = [eye - u["a"] for u in units]
    ps = [u["a"].astype(BF16) for u in units]
    for _ in range(5):
        ps = [_dot(p, p).astype(BF16) for p in ps]
        xs = [x + _dot(x.astype(BF16), p) for x, p in zip(xs, ps)]
    uws = [_dot(x.astype(BF16), u["rhs"]) for x, u in zip(xs, units)]

    states = [state_ref[u["b"], u["h"]] for u in units]
    o_state = [[] for _ in units]
    v_new = [[] for _ in units]
    for c in range(NC):
        cs = slice(c * GDN_CHUNK, (c + 1) * GDN_CHUNK)
        s16 = [s.astype(BF16) for s in states]
        vns = [uw[cs, :GDN_DV] - _dot(uw[cs, GDN_DV:].astype(BF16), s) for uw, s in zip(uws, s16)]
        for i, u in enumerate(units):
            o_state[i].append(_dot(u["qg"][cs], s16[i]))
            v_new[i].append(vns[i])
            g_last = u["gc"][(c + 1) * GDN_CHUNK - 1:(c + 1) * GDN_CHUNK, :]
            kd = (u["kh"][cs] * jnp.exp(g_last - u["gc"][cs])).astype(BF16)
            states[i] = states[i] * jnp.exp(g_last) + _dot_tn(kd, vns[i].astype(BF16))
    for i, u in enumerate(units):
        state_ref[u["b"], u["h"]] = states[i]
        o = jnp.concatenate(o_state[i], axis=0) + _dot(u["qk16"], jnp.concatenate(v_new[i], axis=0).astype(BF16))
        o = _rms(o, ng_ref[...]) * _silu(z_ref[u["b"], :, u["sl"]].astype(F32))
        o_ref[u["b"], :, u["sl"]] = o.astype(o_ref.dtype)


def _gdn(proj16, proj32, conv_w, alog_row, dtb_row, norm_g, nb=2):
    B, S, _ = proj16.shape
    HD = GDN_HEADS * GDN_DK
    blk = lambda w, j: pl.BlockSpec((nb, SPAN, w), lambda b, s, j=j: (b, s, j))
    full = lambda a: pl.BlockSpec(a.shape, lambda b, s: (0,) * a.ndim)
    return pl.pallas_call(
        functools.partial(_gdn_kernel, nb=nb),
        grid=(B // nb, S // SPAN),
        in_specs=[blk(HD, PK_GQ), blk(HD, PK_GK), blk(HD, PK_GV), blk(HD, PK_GZ), blk(LANES, PK32_AB),
                  full(conv_w), full(alog_row), full(dtb_row), full(norm_g)],
        out_specs=blk(HD, 0),
        out_shape=jax.ShapeDtypeStruct((B, S, HD), BF16),
        scratch_shapes=[pltpu.VMEM((nb, 3, SPAN + 8, HD), F32), pltpu.VMEM((nb, GDN_HEADS, GDN_DK, GDN_DV), F32)],
        compiler_params=_params(("parallel", "arbitrary")),
        name="gdn",
    )(proj16, proj16, proj16, proj16, proj32, conv_w, alog_row, dtb_row, norm_g)


def _pool_kernel(u_ref, w_ref, sc_ref, o_ref, carry_ref, *, tp):
    si = pl.program_id(1)
    G = POOL_GROUP

    @pl.when(si == 0)
    def _():
        carry_ref[...] = jnp.zeros_like(carry_ref)

    u = u_ref[...].astype(F32)
    ext = jnp.concatenate([carry_ref[...], u], axis=0)
    carry_ref[...] = u[tp - 16:tp]
    t = si * tp + lax.broadcasted_iota(jnp.int32, (tp, 1), 0)
    win_sum = ext
    outs = []
    shift = 1
    for gi, win in enumerate(POOL_WINDOWS):
        while shift < win:
            win_sum = win_sum + pltpu.roll(win_sum, shift, 0)
            shift *= 2
        cnt = jnp.minimum(t + 1, win).astype(F32)
        ug = u[:, gi * G:(gi + 1) * G]
        pooled = win_sum[16:, gi * G:(gi + 1) * G] / cnt - ug
        outs.append(_dot(pooled.astype(BF16), w_ref[gi]))
    y = jnp.concatenate(outs, axis=1) * sc_ref[...]
    o_ref[...] = y.astype(o_ref.dtype)


def _pool(proj, w_group, scale_row, B, S, tp=512):
    T = proj.shape[0]
    ns = S // tp
    W = len(POOL_WINDOWS) * POOL_GROUP
    return pl.pallas_call(
        functools.partial(_pool_kernel, tp=tp),
        grid=(B, ns),
        in_specs=[pl.BlockSpec((tp, W), lambda b, s: (b * ns + s, PK_PU)),
                  pl.BlockSpec(w_group.shape, lambda b, s: (0, 0, 0)),
                  pl.BlockSpec(scale_row.shape, lambda b, s: (0, 0))],
        out_specs=pl.BlockSpec((tp, W), lambda b, s: (b * ns + s, 0)),
        out_shape=jax.ShapeDtypeStruct((T, W), BF16),
        scratch_shapes=[pltpu.VMEM((16, W), F32)],
        compiler_params=_params(("parallel", "arbitrary")),
        name="pool",
    )(proj, w_group, scale_row)


def _mid_broadcast(c, s):
    n = c.shape[0]
    if 2 * s >= 8:
        c3 = c.reshape(n // (2 * s), 2 * s, c.shape[1])
        mid = jnp.broadcast_to(c3[:, s - 1:s, :], c3.shape)
        return mid.reshape(c.shape)
    pos = lax.broadcasted_iota(jnp.int32, c.shape, 0) % (2 * s)
    out = c
    for p in range(2 * s):
        d = p - (s - 1)
        if d != 0:
            out = jnp.where(pos == p, pltpu.roll(c, d % n, 0), out)
    return out


def _hgrn_kernel(q_ref, f_ref, i_ref, g_ref, lb_ref, ng_ref, o_ref, state_ref, *, nb):
    si = pl.program_id(1)

    @pl.when(si == 0)
    def _():
        state_ref[...] = jnp.zeros_like(state_ref)

    row = lax.broadcasted_iota(jnp.int32, (SPAN, SPAN), 0)
    col = lax.broadcasted_iota(jnp.int32, (SPAN, SPAN), 1)
    lower16 = (col <= row).astype(BF16)
    heads = [slice(h * HGRN_DK, (h + 1) * HGRN_DK) for h in range(HGRN_HEADS)]
    halves = [slice(0, HALF), slice(HALF, SPAN)]
    ri = lax.broadcasted_iota(jnp.int32, (HALF, HALF), 0)
    ci = lax.broadcasted_iota(jnp.int32, (HALF, HALF), 1)
    rowl = lax.broadcasted_iota(jnp.int32, (SPAN, LANES), 0)
    lb = lb_ref[...]

    seqs = []
    for b in range(nb):
        q = _silu(q_ref[b].astype(F32))
        fp = f_ref[b]
        log_f = jnp.log(lb + (1.0 - lb) * _sigmoid(fp))
        k = (1.0 - lb) * _sigmoid(-fp)
        c = _exact_left_dot(lower16, log_f)
        q16 = q.astype(BF16)
        k16 = k.astype(BF16)
        att = [[jnp.where(ri == ci, _dot_nt(q16[r, sl], k16[r, sl]), 0.0) for sl in heads] for r in halves]
        seqs.append(dict(q=q, k=k, c=c, q16=q16, k16=k16, att=att, cross=None))

    s = 1
    while s < SPAN:
        sgn = jnp.where(((rowl // s) % 2) == 1, LOG2E, -LOG2E)
        if s < HALF:
            bi = ri // s
            bj = ci // s
            region = ((bi - bj) * 2 + (bj & 1)) == 2
        for sq in seqs:
            diff = sq["c"] - _mid_broadcast(sq["c"], s)
            w16 = jnp.concatenate([jnp.exp2(diff[:, sl] * sgn) for sl in heads], axis=1).astype(BF16)
            ql = sq["q16"] * w16
            kl = sq["k16"] * w16
            if s < HALF:
                for d, r in enumerate(halves):
                    for h, sl in enumerate(heads):
                        sq["att"][d][h] = jnp.where(region, _dot_nt(ql[r, sl], kl[r, sl]), sq["att"][d][h])
            else:
                sq["cross"] = [_dot_nt(ql[halves[1], sl], kl[halves[0], sl]) for sl in heads]
        s *= 2

    for b, sq in enumerate(seqs):
        c = sq["c"]
        c_last = c[SPAN - 1:SPAN, :]
        v16 = i_ref[b]
        qe = (sq["q"] * jnp.exp(c)).astype(BF16)
        ke = (sq["k"] * jnp.exp(c_last - c)).astype(BF16)
        e_last = jnp.exp(c_last)
        for h, sl in enumerate(heads):
            st = state_ref[b, h]
            v_lo = v16[halves[0], sl]
            v_hi = v16[halves[1], sl]
            o_lo = _dot(sq["att"][0][h].astype(BF16), v_lo)
            o_hi = _dot(sq["cross"][h].astype(BF16), v_lo) + _dot(sq["att"][1][h].astype(BF16), v_hi)
            o = jnp.concatenate([o_lo, o_hi], axis=0) + _dot_nt(qe[:, sl], st.astype(BF16))
            state_ref[b, h] = st * e_last[:, sl] + _dot_tn(v16[:, sl], ke[:, sl])
            o = _rms(o, ng_ref[...]) * _sigmoid(g_ref[b, :, sl].astype(F32))
            o_ref[b, :, sl] = o.astype(o_ref.dtype)


def _hgrn(proj16, proj32, lb_row, norm_g, nb=2):
    B, S, _ = proj16.shape
    HD = HGRN_HEADS * HGRN_DK
    blk = lambda j: pl.BlockSpec((nb, SPAN, HD), lambda b, s, j=j: (b, s, j))
    full = lambda a: pl.BlockSpec(a.shape, lambda b, s: (0,) * a.ndim)
    return pl.pallas_call(
        functools.partial(_hgrn_kernel, nb=nb),
        grid=(B // nb, S // SPAN),
        in_specs=[blk(PK_HQ), blk(PK32_HF), blk(PK_HI), blk(PK_HG), full(lb_row), full(norm_g)],
        out_specs=blk(0),
        out_shape=jax.ShapeDtypeStruct((B, S, HD), BF16),
        scratch_shapes=[pltpu.VMEM((nb, HGRN_HEADS, HGRN_DV, HGRN_DK), F32)],
        compiler_params=_params(("parallel", "arbitrary")),
        name="hgrn2",
    )(proj16, proj32, proj16, proj16, lb_row, norm_g)


def _merge_kernel(x_ref, ya_ref, yb_ref, yc_ref, yd_ref, wg_ref, wb_ref, wo_ref, g_ref, b_ref, o_ref):
    x = x_ref[...]
    x16 = x.astype(BF16)
    merged = None
    for m, y_ref in enumerate((ya_ref, yb_ref, yc_ref, yd_ref)):
        gate = _sigmoid(_dot(x16, wg_ref[:, m * D_MODEL:(m + 1) * D_MODEL]))
        term = gate * _dot(y_ref[...], wb_ref[m])
        merged = term if merged is None else merged + term
    h = DEEPNORM_ALPHA * x + _dot(merged.astype(BF16), wo_ref[...])
    o_ref[...] = _layer_norm(h, g_ref[...], b_ref[...])


def _merge(x2, ya, yb, yc, yd, w_gate, w_branch, w_out, ln_g, ln_b, tm=512):
    T, D = x2.shape
    row = lambda w: pl.BlockSpec((tm, w), lambda i: (i, 0))
    full = lambda a: pl.BlockSpec(a.shape, lambda i: (0,) * a.ndim, pipeline_mode=pl.Buffered(1))
    return pl.pallas_call(
        _merge_kernel,
        grid=(T // tm,),
        in_specs=[row(D), row(BRANCH_WIDTH), row(BRANCH_WIDTH), row(BRANCH_WIDTH), row(BRANCH_WIDTH),
                  full(w_gate), full(w_branch), full(w_out), full(ln_g), full(ln_b)],
        out_specs=row(D),
        out_shape=jax.ShapeDtypeStruct((T, D), F32),
        compiler_params=_params(("parallel",)),
        name="merge",
    )(x2, ya, yb, yc, yd, w_gate, w_branch, w_out, ln_g, ln_b)


def _ffn_kernel(x_ref, wg_ref, wu_ref, wd_ref, g_ref, b_ref, o_ref):
    x = x_ref[...]
    x16 = x.astype(BF16)
    h = _silu(_dot(x16, wg_ref[...])) * _dot(x16, wu_ref[...])
    y = DEEPNORM_ALPHA * x + _dot(h.astype(BF16), wd_ref[...])
    o_ref[...] = _layer_norm(y, g_ref[...], b_ref[...])


def _ffn(x2, wg, wu, wd, ln_g, ln_b, tm=512):
    T, D = x2.shape
    row = pl.BlockSpec((tm, D), lambda i: (i, 0))
    full = lambda a: pl.BlockSpec(a.shape, lambda i: (0,) * a.ndim, pipeline_mode=pl.Buffered(1))
    return pl.pallas_call(
        _ffn_kernel,
        grid=(T // tm,),
        in_specs=[row, full(wg), full(wu), full(wd), full(ln_g), full(ln_b)],
        out_specs=row,
        out_shape=jax.ShapeDtypeStruct((T, D), F32),
        compiler_params=_params(("parallel",)),
        name="ffn_dense",
    )(x2, wg, wu, wd, ln_g, ln_b)


def _router_kernel(x_ref, w_ref, meta_ref, gate_ref, cnt_ref, base_ref, *, tm):
    i = pl.program_id(0)

    @pl.when(i == 0)
    def _():
        base_ref[...] = jnp.zeros_like(base_ref)

    xh, xm, _ = _split3(x_ref[...])
    wh, wm, _ = _split3(w_ref[...])
    logits = _dot(xh, wh) + _dot(xh, wm) + _dot(xm, wh)
    lane = lax.broadcasted_iota(jnp.int32, (tm, LANES), 1)
    neg = jnp.float32(-jnp.inf)
    logits = jnp.where(lane < N_EXPERTS, logits, neg)
    v1 = jnp.max(logits, -1, keepdims=True)
    e1 = jnp.min(jnp.where(logits == v1, lane, LANES), -1, keepdims=True)
    rest = jnp.where(lane == e1, neg, logits)
    v2 = jnp.max(rest, -1, keepdims=True)
    e2 = jnp.min(jnp.where(rest == v2, lane, LANES), -1, keepdims=True)
    ex = jnp.exp(v2 - v1)
    w1 = 1.0 / (1.0 + ex)
    w2 = ex / (1.0 + ex)
    onehot = ((lane == e1) | (lane == e2)).astype(F32)
    r = lax.broadcasted_iota(jnp.int32, (tm, tm), 0)
    cc = lax.broadcasted_iota(jnp.int32, (tm, tm), 1)
    before = _dot((cc < r).astype(BF16), onehot.astype(BF16)) + base_ref[0:1, :]
    rank1 = jnp.sum(jnp.where(lane == e1, before, 0.0), -1, keepdims=True)
    rank2 = jnp.sum(jnp.where(lane == e2, before, 0.0), -1, keepdims=True)
    total = base_ref[0:1, :] + jnp.sum(onehot, 0, keepdims=True)
    base_ref[...] = jnp.broadcast_to(total, base_ref.shape)
    cnt_ref[...] = jnp.broadcast_to(total, cnt_ref.shape).astype(jnp.int32)
    meta = jnp.where(lane == 0, e1, jnp.where(lane == 1, e2, jnp.where(
        lane == 2, rank1.astype(jnp.int32), jnp.where(lane == 3, rank2.astype(jnp.int32), 0))))
    meta_ref[...] = meta
    gate_ref[...] = jnp.where(lane == 0, w1, jnp.where(lane == 1, w2, 0.0))


def _router(x2, w_router_pad, tm=512):
    T, D = x2.shape
    return pl.pallas_call(
        functools.partial(_router_kernel, tm=tm),
        grid=(T // tm,),
        in_specs=[pl.BlockSpec((tm, D), lambda i: (i, 0)),
                  pl.BlockSpec(w_router_pad.shape, lambda i: (0, 0))],
        out_specs=[pl.BlockSpec((tm, LANES), lambda i: (i, 0)),
                   pl.BlockSpec((tm, LANES), lambda i: (i, 0)),
                   pl.BlockSpec((8, LANES), lambda i: (0, 0))],
        out_shape=[jax.ShapeDtypeStruct((T, LANES), jnp.int32),
                   jax.ShapeDtypeStruct((T, LANES), F32),
                   jax.ShapeDtypeStruct((8, LANES), jnp.int32)],
        scratch_shapes=[pltpu.VMEM((8, LANES), F32)],
        compiler_params=_params(("arbitrary",)),
        name="moe_router",
    )(x2, w_router_pad)


def _dispatch_kernel(zb_ref, dest_ref, x_ref, xb_ref, zero_ref, sem, zsem, *, td):
    @pl.when(pl.program_id(0) == 0)
    def _():
        zero_ref[...] = jnp.zeros_like(zero_ref)
        nz = 2 * N_EXPERTS
        clear = [pltpu.make_async_copy(zero_ref, xb_ref.at[pl.ds(zb_ref[z] * MOE_ROWS, MOE_ROWS)], zsem)
                 for z in range(nz)]
        for z, cp in enumerate(clear):
            pl.when(zb_ref[nz + z] > 0)(cp.start)
        for z, cp in enumerate(clear):
            pl.when(zb_ref[nz + z] > 0)(cp.wait)

    def start(r, c):
        for slot in range(TOP_K):
            pltpu.make_async_copy(x_ref.at[pl.ds(r, 1)],
                                  xb_ref.at[pl.ds(dest_ref[0, 0, slot * td + r], 1)], sem).start(priority=slot)
        return c

    lax.fori_loop(0, td, start, 0, unroll=8)
    pltpu.make_async_copy(xb_ref.at[pl.ds(0, TOP_K * td)], xb_ref.at[pl.ds(0, TOP_K * td)], sem).wait()


def _dispatch(x2, dest_blocks, clear_blocks, n_rows, td=256):
    T, D = x2.shape
    grid_spec = pltpu.PrefetchScalarGridSpec(
        num_scalar_prefetch=1,
        grid=(T // td,),
        in_specs=[pl.BlockSpec((1, 1, TOP_K * td), lambda i, zb: (i, 0, 0), memory_space=pltpu.SMEM),
                  pl.BlockSpec((td, D), lambda i, zb: (i, 0))],
        out_specs=pl.BlockSpec(memory_space=pl.ANY),
        scratch_shapes=[pltpu.VMEM((MOE_ROWS, D), F32), pltpu.SemaphoreType.DMA(()), pltpu.SemaphoreType.DMA(())],
    )
    return pl.pallas_call(
        functools.partial(_dispatch_kernel, td=td),
        grid_spec=grid_spec,
        out_shape=jax.ShapeDtypeStruct((n_rows, D), F32),
        compiler_params=_params(("arbitrary",)),
        name="moe_dispatch",
    )(clear_blocks, dest_blocks, x2)


def _expert_kernel(blk_exp_ref, n_used_ref, x_ref, wg_ref, wu_ref, wd_ref, o_ref, acc_ref):
    i = pl.program_id(0)
    j = pl.program_id(1)

    @pl.when(i < n_used_ref[0])
    def _():
        x16 = x_ref[...].astype(BF16)
        h = _silu(_dot(x16, wg_ref[0])) * _dot(x16, wu_ref[0])
        y = _dot(h.astype(BF16), wd_ref[0])

        @pl.when(j == 0)
        def _():
            acc_ref[...] = y

        @pl.when(j > 0)
        def _():
            acc_ref[...] += y

        @pl.when(j == pl.num_programs(1) - 1)
        def _():
            o_ref[...] = acc_ref[...]

    @pl.when(i >= n_used_ref[0])
    def _():
        o_ref[...] = jnp.zeros_like(o_ref)


def _experts(xb, blk_exp, n_used, wg, wu, wd, tf=1792):
    n_rows, D = xb.shape
    n_blk = n_rows // MOE_ROWS
    nf = D_FF_EXPERT // tf

    def rows(i, j, be, nu):
        return (jnp.minimum(i, nu[0] - 1), 0)

    def chunk(i, j, nu):
        ic = jnp.minimum(i, nu[0] - 1)
        return jnp.where(ic % 2 == 0, jnp.where(i < nu[0], j, nf - 1), jnp.where(i < nu[0], nf - 1 - j, 0))

    grid_spec = pltpu.PrefetchScalarGridSpec(
        num_scalar_prefetch=2,
        grid=(n_blk, nf),
        in_specs=[pl.BlockSpec((MOE_ROWS, D), rows),
                  pl.BlockSpec((1, D, tf), lambda i, j, be, nu: (be[jnp.minimum(i, nu[0] - 1)], 0, chunk(i, j, nu))),
                  pl.BlockSpec((1, D, tf), lambda i, j, be, nu: (be[jnp.minimum(i, nu[0] - 1)], 0, chunk(i, j, nu))),
                  pl.BlockSpec((1, tf, D), lambda i, j, be, nu: (be[jnp.minimum(i, nu[0] - 1)], chunk(i, j, nu), 0))],
        out_specs=pl.BlockSpec((MOE_ROWS, D), lambda i, j, be, nu: (i, 0)),
        scratch_shapes=[pltpu.VMEM((MOE_ROWS, D), F32)],
    )
    return pl.pallas_call(
        _expert_kernel,
        grid_spec=grid_spec,
        out_shape=jax.ShapeDtypeStruct((n_rows, D), F32),
        compiler_params=_params(("arbitrary", "arbitrary")),
        name="moe_experts",
    )(blk_exp, n_used, xb, wg, wu, wd)


def _combine_kernel(dest_ref, dnext_ref, x_ref, gate_ref, yb_ref, g_ref, b_ref, o_ref, buf_ref, sem, *, tc):
    i = pl.program_id(0)
    n = pl.num_programs(0)

    def issue(d_ref, slot):
        def start(r, c):
            for kk in range(TOP_K):
                pltpu.make_async_copy(yb_ref.at[pl.ds(d_ref[0, 0, kk * tc + r], 1)],
                                      buf_ref.at[slot, kk, pl.ds(r, 1)], sem.at[slot]).start(priority=kk)
            return c
        lax.fori_loop(0, tc, start, 0, unroll=8)

    @pl.when(i == 0)
    def _():
        issue(dest_ref, 0)

    @pl.when(i + 1 < n)
    def _():
        issue(dnext_ref, (i + 1) % 2)

    slot = i % 2
    pltpu.make_async_copy(buf_ref.at[slot], buf_ref.at[slot], sem.at[slot]).wait()
    gate = gate_ref[...]
    ffn = buf_ref[slot, 0] * gate[:, 0:1] + buf_ref[slot, 1] * gate[:, 1:2]
    o_ref[...] = _layer_norm(DEEPNORM_ALPHA * x_ref[...] + ffn, g_ref[...], b_ref[...])


def _combine(x2, gates, yb, dest_blocks, ln_g, ln_b, tc=256):
    T, D = x2.shape
    n = T // tc
    full = lambda a: pl.BlockSpec(a.shape, lambda i: (0,) * a.ndim)
    smem = lambda imap: pl.BlockSpec((1, 1, TOP_K * tc), imap, memory_space=pltpu.SMEM)
    return pl.pallas_call(
        functools.partial(_combine_kernel, tc=tc),
        grid=(n,),
        in_specs=[smem(lambda i: (i, 0, 0)), smem(lambda i: (jnp.minimum(i + 1, n - 1), 0, 0)),
                  pl.BlockSpec((tc, D), lambda i: (i, 0)),
                  pl.BlockSpec((tc, LANES), lambda i: (i, 0)),
                  pl.BlockSpec(memory_space=pl.ANY), full(ln_g), full(ln_b)],
        out_specs=pl.BlockSpec((tc, D), lambda i: (i, 0)),
        out_shape=jax.ShapeDtypeStruct((T, D), F32),
        scratch_shapes=[pltpu.VMEM((2, TOP_K, tc, D), F32), pltpu.SemaphoreType.DMA((2,))],
        compiler_params=_params(("arbitrary",)),
        name="moe_combine",
    )(dest_blocks, dest_blocks, x2, gates, yb, ln_g, ln_b)


def _moe(x2, w_router, wg, wu, wd, ln_g, ln_b, tok=256):
    T, D = x2.shape
    w_router_pad = jnp.pad(w_router.astype(F32), ((0, 0), (0, LANES - N_EXPERTS)))
    meta, gates, counts = _router(x2, w_router_pad)
    counts = counts[0, :N_EXPERTS]
    padded = (counts + MOE_ROWS - 1) // MOE_ROWS * MOE_ROWS
    pends = jnp.cumsum(padded)
    pstarts = pends - padded
    dest = pstarts[meta[:, 0:2]] + meta[:, 2:4]
    dest_blocks = dest.reshape(T // tok, tok, 2).transpose(0, 2, 1).reshape(T // tok, 1, 2 * tok)
    n_blk = (T * TOP_K + MOE_ROWS - 1) // MOE_ROWS + N_EXPERTS
    n_rows = n_blk * MOE_ROWS
    blk_start = jnp.arange(n_blk, dtype=jnp.int32) * MOE_ROWS
    blk_exp = jnp.minimum(jnp.sum((blk_start[:, None] >= pends[None, :]).astype(jnp.int32), axis=1), N_EXPERTS - 1)
    n_used = (pends[-1:] // MOE_ROWS).astype(jnp.int32)
    tail = n_used + jnp.arange(N_EXPERTS, dtype=jnp.int32)
    clear_blocks = jnp.concatenate([jnp.maximum(pends // MOE_ROWS - 1, 0), jnp.minimum(tail, n_blk - 1),
                                    counts > 0, tail < n_blk]).astype(jnp.int32)
    xb = _dispatch(x2, dest_blocks, clear_blocks, n_rows, td=tok)
    yb = _experts(xb, blk_exp, n_used, wg, wu, wd)
    return _combine(x2, gates, yb, dest_blocks, ln_g, ln_b, tc=tok)


def _rot_half_cols(w):
    half = w.shape[-1] // 2
    return jnp.concatenate([-w[..., half:], w[..., :half]], axis=-1)


def _pack_w_in(w):
    D = w.shape[0]
    kr = w[:, 384:448]
    z64 = jnp.zeros((D, 64), w.dtype)
    w16 = jnp.concatenate([
        w[:, 448:2496],
        w[:, 2504:3016],
        w[:, 3016:3528],
        w[:, 4040:5064],
        w[:, 0:256],
        w[:, 256:384],
        kr, z64,
        _rot_half_cols(kr), z64,
    ], axis=1)
    w32 = jnp.concatenate([w[:, 3528:4040], w[:, 2496:2504], jnp.zeros((D, LANES - 8), w.dtype)], axis=1)
    return w16.astype(BF16), w32.astype(BF16), w[:, 5064:].astype(BF16)


def _pack_mla(w_uq, w_ukv):
    R = w_uq.shape[0]
    z64 = jnp.zeros((R, 64), w_uq.dtype)
    wq, wqr = [], []
    for h in range(MLA_HEADS):
        o = h * (MLA_NOPE + MLA_ROPE)
        rope_cols = w_uq[:, o + MLA_NOPE:o + MLA_NOPE + MLA_ROPE]
        wq += [w_uq[:, o:o + MLA_NOPE], rope_cols, z64]
        wqr += [_rot_half_cols(rope_cols), z64]
    kv = w_ukv.reshape(w_ukv.shape[0], MLA_HEADS, MLA_NOPE + MLA_V)
    wk = kv[:, :, :MLA_NOPE].reshape(w_ukv.shape[0], MLA_HEADS * MLA_NOPE)
    wv = kv[:, :, MLA_NOPE:].reshape(w_ukv.shape[0], MLA_HEADS * MLA_V)
    return (jnp.concatenate(wq, 1).astype(BF16), jnp.concatenate(wqr, 1).astype(BF16),
            wk.astype(BF16), wv.astype(BF16))


def _lane_row(v, offset=0):
    v = v.astype(F32)
    return jnp.pad(v, (offset, LANES - offset - v.shape[0]))[None, :]


def kernel(x, positions, w_in, mla_q_norm, mla_w_uq, mla_kv_norm, mla_w_ukv, gdn_conv, gdn_a_log, gdn_dt_bias, gdn_norm, pool_w, pool_scale, hgrn_lb_logits, hgrn_norm, w_branch, w_out, ln_mix_g, ln_mix_b, ffn_w_gate, ffn_w_up, ffn_w_down, moe_router, moe_w_gate, moe_w_up, moe_w_down, ln_ffn_g, ln_ffn_b):
    B, S, D = x.shape
    T = B * S
    x2 = x.reshape(T, D)

    half = MLA_ROPE // 2
    inv = ROPE_THETA ** (-jnp.arange(half, dtype=F32) / half)
    inv_row = jnp.concatenate([inv, inv, jnp.zeros((LANES - MLA_ROPE,), F32)])[None, :]
    cos_t, sin_t = _rope_tables(positions.reshape(T, 1).astype(jnp.int32), inv_row)

    p_lb = jax.nn.softmax(hgrn_lb_logits.astype(F32), axis=0)
    lower_bounds = jnp.cumsum(p_lb, axis=0) - p_lb[0]
    row2 = lambda v: v.astype(F32)[None, :]

    for l in range(DEPTH):
        w16, w32, w_gate = _pack_w_in(w_in[l])
        proj16, proj32 = _inproj(x2, w16, w32)
        wq, wqr, wk, wv = _pack_mla(mla_w_uq[l], mla_w_ukv[l])
        y_a = _mla(proj16, cos_t, sin_t, row2(mla_q_norm[l]), row2(mla_kv_norm[l]), wq, wqr, wk, wv, B, S)
        proj16_3 = proj16.reshape(B, S, PK16_WIDTH)
        proj32_3 = proj32.reshape(B, S, PK32_WIDTH)
        y_b = _gdn(proj16_3, proj32_3, gdn_conv[l].astype(F32), _lane_row(gdn_a_log[l]),
                   _lane_row(gdn_dt_bias[l]), row2(gdn_norm[l])).reshape(T, GDN_HEADS * GDN_DV)
        y_c = _pool(proj16, pool_w[l].astype(BF16), row2(pool_scale[l]), B, S)
        y_d = _hgrn(proj16_3, proj32_3, row2(lower_bounds[l]), row2(hgrn_norm[l])).reshape(T, HGRN_HEADS * HGRN_DV)
        x2 = _merge(x2, y_a, y_b, y_c, y_d, w_gate, w_branch[l].astype(BF16), w_out[l].astype(BF16),
                    row2(ln_mix_g[l]), row2(ln_mix_b[l]))
        j = l // 2
        if l % 2 == 0:
            x2 = _ffn(x2, ffn_w_gate[j].astype(BF16), ffn_w_up[j].astype(BF16), ffn_w_down[j].astype(BF16),
                      row2(ln_ffn_g[l]), row2(ln_ffn_b[l]))
        else:
            x2 = _moe(x2, moe_router[j], moe_w_gate[j].astype(BF16), moe_w_up[j].astype(BF16),
                      moe_w_down[j].astype(BF16), row2(ln_ffn_g[l]), row2(ln_ffn_b[l]))
    return x2.reshape(B, S, D)
```

```python
import functools
import math

import jax
import jax.numpy as jnp
from jax import lax
from jax.experimental import pallas as pl
from jax.experimental.pallas import tpu as pltpu

F32 = jnp.float32
BF16 = jnp.bfloat16

D_MODEL = 1024
DEPTH = 2
MLA_HEADS = 4
MLA_Q_RANK = 256
MLA_KV_RANK = 128
MLA_NOPE = 128
MLA_ROPE = 64
MLA_V = 128
ROPE_THETA = 10000.0
MASK_VALUE = -1e30
GDN_HEADS = 4
GDN_DK = 128
GDN_DV = 128
GDN_CONV = 4
GDN_CHUNK = 64
POOL_WINDOWS = (2, 4, 8, 16)
POOL_GROUP = 128
HGRN_HEADS = 4
HGRN_DK = 128
HGRN_DV = 128
N_BRANCH = 4
BRANCH_WIDTH = 512
D_FF = 2816
N_EXPERTS = 8
TOP_K = 2
D_FF_EXPERT = 3584
DEEPNORM_ALPHA = (2 * DEPTH) ** 0.25
LN_EPS = 1e-5
RMS_EPS = 1e-6
LOG2E = math.log2(math.e)

LANES = 128
MLA_HEAD_PAD = 256
SPAN = 256
HALF = SPAN // 2
MOE_ROWS = 512
VMEM_LIMIT = 56 * 1024 * 1024

PK_GQ, PK_GK, PK_GV, PK_GZ, PK_PU, PK_HQ, PK_HI, PK_HG = range(8)
PK_CQ = 16
PK_CKV = 34
PK_KRA = 35
PK_KRB = 36
PK16_WIDTH = 37 * LANES
PK32_HF = 0
PK32_AB = 4
PK32_WIDTH = 5 * LANES


def _params(sem, vmem=VMEM_LIMIT):
    return pltpu.CompilerParams(dimension_semantics=sem, vmem_limit_bytes=vmem)


def _dot(a, b):
    return jnp.dot(a, b, preferred_element_type=F32)


def _dot_nt(a, b):
    return lax.dot_general(a, b, (((1,), (1,)), ((), ())), preferred_element_type=F32)


def _dot_tn(a, b):
    return lax.dot_general(a, b, (((0,), (0,)), ((), ())), preferred_element_type=F32)


def _split3(x):
    hi = x.astype(BF16)
    r1 = x - hi.astype(F32)
    mid = r1.astype(BF16)
    lo = (r1 - mid.astype(F32)).astype(BF16)
    return hi, mid, lo


def _exact_left_dot(m01, x):
    hi, mid, lo = _split3(x)
    return _dot(m01, hi) + _dot(m01, mid) + _dot(m01, lo)


def _sigmoid(x):
    return 1.0 / (1.0 + jnp.exp(-x))


def _silu(x):
    return x * _sigmoid(x)


def _softplus(x):
    return jnp.maximum(x, 0.0) + jnp.log(1.0 + jnp.exp(-jnp.abs(x)))


def _layer_norm(x, g, b):
    mu = jnp.mean(x, -1, keepdims=True)
    xc = x - mu
    var = jnp.mean(xc * xc, -1, keepdims=True)
    return xc * lax.rsqrt(var + LN_EPS) * g + b


def _rms(x, g):
    return x * lax.rsqrt(jnp.mean(x * x, -1, keepdims=True) + RMS_EPS) * g


def _lane_blocks(s):
    return [s[:, c * LANES:(c + 1) * LANES] for c in range(s.shape[1] // LANES)]


def _row_max(s):
    return jnp.max(functools.reduce(jnp.maximum, _lane_blocks(s)), -1, keepdims=True)


def _row_sum(s):
    return jnp.sum(functools.reduce(jnp.add, _lane_blocks(s)), -1, keepdims=True)


def _inproj_kernel(x_ref, w16_ref, w32_ref, o16_ref, o32_ref):
    x16 = x_ref[...].astype(BF16)
    o16_ref[...] = _dot(x16, w16_ref[...]).astype(o16_ref.dtype)
    o32_ref[...] = _dot(x16, w32_ref[...])


def _inproj(x2, w16, w32, tm=512):
    T, D = x2.shape
    full = lambda a: pl.BlockSpec(a.shape, lambda i: (0, 0), pipeline_mode=pl.Buffered(1))
    return pl.pallas_call(
        _inproj_kernel,
        grid=(T // tm,),
        in_specs=[pl.BlockSpec((tm, D), lambda i: (i, 0)), full(w16), full(w32)],
        out_specs=[pl.BlockSpec((tm, w16.shape[1]), lambda i: (i, 0)),
                   pl.BlockSpec((tm, w32.shape[1]), lambda i: (i, 0))],
        out_shape=[jax.ShapeDtypeStruct((T, w16.shape[1]), BF16),
                   jax.ShapeDtypeStruct((T, w32.shape[1]), F32)],
        compiler_params=_params(("parallel",)),
        name="inproj",
    )(x2, w16, w32)


def _rope_table_kernel(pos_ref, inv_ref, c_ref, s_ref):
    ang = pos_ref[...].astype(F32) * inv_ref[...]
    c_ref[...] = jnp.cos(ang)
    s_ref[...] = jnp.sin(ang)


def _rope_tables(pos_col, inv_row, tm=1024):
    T = pos_col.shape[0]
    return pl.pallas_call(
        _rope_table_kernel,
        grid=(T // tm,),
        in_specs=[pl.BlockSpec((tm, 1), lambda i: (i, 0)),
                  pl.BlockSpec((1, LANES), lambda i: (0, 0))],
        out_specs=[pl.BlockSpec((tm, LANES), lambda i: (i, 0))] * 2,
        out_shape=[jax.ShapeDtypeStruct((T, LANES), F32)] * 2,
        compiler_params=_params(("parallel",)),
        name="rope_tables",
    )(pos_col, inv_row)


def _mla_prep_rows(rs, cq_ref, ckv_ref, kra_ref, krb_ref, c_ref, s_ref, gq_ref, gkv_ref,
                   wq_ref, wqr_ref, wk_ref, wv_ref, q_s, k_s, v_s):
    scale = (MLA_NOPE + MLA_ROPE) ** -0.5 * LOG2E
    cos = c_ref[rs, :]
    sin = s_ref[rs, :]
    nq = _rms(cq_ref[rs, :].astype(F32), gq_ref[...]).astype(BF16)
    q_pre = _dot(nq, wq_ref[...])
    q_rot = _dot(nq, wqr_ref[...])
    nkv = _rms(ckv_ref[rs, :].astype(F32), gkv_ref[...]).astype(BF16)
    k_nope = _dot(nkv, wk_ref[...])
    v_s[rs, :] = _dot(nkv, wv_ref[...]).astype(v_s.dtype)
    k_rope = (kra_ref[rs, :].astype(F32) * cos + krb_ref[rs, :].astype(F32) * sin).astype(k_s.dtype)
    for h in range(MLA_HEADS):
        o = h * MLA_HEAD_PAD
        q_s[rs, o:o + LANES] = (q_pre[:, o:o + LANES] * scale).astype(q_s.dtype)
        q_s[rs, o + LANES:o + 2 * LANES] = (
            (q_pre[:, o + LANES:o + 2 * LANES] * cos + q_rot[:, h * LANES:(h + 1) * LANES] * sin) * scale
        ).astype(q_s.dtype)
        k_s[rs, o:o + LANES] = k_nope[:, h * LANES:(h + 1) * LANES].astype(k_s.dtype)
        k_s[rs, o + LANES:o + 2 * LANES] = k_rope


def _attn_tile(q_s, k_s, v_s, o_ref, t, tq, causal):
    r0 = t * tq
    for h in range(MLA_HEADS):
        hs = slice(h * MLA_HEAD_PAD, (h + 1) * MLA_HEAD_PAD)
        vs = slice(h * MLA_V, (h + 1) * MLA_V)
        qh = q_s[r0:r0 + tq, hs]
        s_d = jnp.where(causal, _dot_nt(qh, k_s[r0:r0 + tq, hs]), MASK_VALUE)
        m = _row_max(s_d)
        if t > 0:
            s_o = _dot_nt(qh, k_s[0:r0, hs])
            m = jnp.maximum(m, _row_max(s_o))
        p_d = jnp.exp2(s_d - m)
        l = _row_sum(p_d)
        acc = _dot(p_d.astype(BF16), v_s[r0:r0 + tq, vs])
        if t > 0:
            p_o = jnp.exp2(s_o - m)
            l = l + _row_sum(p_o)
            acc = acc + _dot(p_o.astype(BF16), v_s[0:r0, vs])
        o_ref[:, vs] = (acc / l).astype(o_ref.dtype)


def _mla_kernel(*refs, seq, tq, tprep):
    prep_refs, o_ref, (q_s, k_s, v_s) = refs[:12], refs[12], refs[13:]
    i = pl.program_id(1)

    @pl.when(i == 0)
    def _():
        for c in range(seq // tprep):
            _mla_prep_rows(slice(c * tprep, (c + 1) * tprep), *prep_refs, q_s, k_s, v_s)

    row = lax.broadcasted_iota(jnp.int32, (tq, tq), 0)
    col = lax.broadcasted_iota(jnp.int32, (tq, tq), 1)
    causal = row >= col
    for t in range(seq // tq):
        pl.when(i == t)(functools.partial(_attn_tile, q_s, k_s, v_s, o_ref, t, tq, causal))


def _mla(proj, cos_t, sin_t, g_q, g_kv, wq, wqr, wk, wv, B, S, tq=256, tprep=512):
    T = proj.shape[0]
    nq = S // tq
    HW = MLA_HEADS * MLA_HEAD_PAD
    HV = MLA_HEADS * MLA_V
    seq = lambda w, j: pl.BlockSpec((S, w), lambda b, i, j=j: (b, j))
    full = lambda a: pl.BlockSpec(a.shape, lambda b, i: (0,) * a.ndim)
    return pl.pallas_call(
        functools.partial(_mla_kernel, seq=S, tq=tq, tprep=min(tprep, S)),
        grid=(B, nq),
        in_specs=[seq(256, PK_CQ), seq(128, PK_CKV), seq(128, PK_KRA), seq(128, PK_KRB),
                  seq(128, 0), seq(128, 0), full(g_q), full(g_kv), full(wq), full(wqr), full(wk), full(wv)],
        out_specs=pl.BlockSpec((tq, HV), lambda b, i: (b * nq + i, 0)),
        out_shape=jax.ShapeDtypeStruct((T, HV), BF16),
        scratch_shapes=[pltpu.VMEM((S, HW), BF16), pltpu.VMEM((S, HW), BF16), pltpu.VMEM((S, HV), BF16)],
        compiler_params=_params(("parallel", "arbitrary")),
        name="mla",
    )(proj, proj, proj, proj, cos_t, sin_t, g_q, g_kv, wq, wqr, wk, wv)


def _gdn_body(q_ref, k_ref, v_ref, z_ref, ab_ref, cw_ref, alog_ref, dtb_ref, ng_ref, o_ref,
              carry_ref, state_ref, *, nb):
    si = pl.program_id(1)
    HD = GDN_HEADS * GDN_DK
    NC = SPAN // GDN_CHUNK

    @pl.when(si == 0)
    def _():
        carry_ref[...] = jnp.zeros_like(carry_ref)
        state_ref[...] = jnp.zeros_like(state_ref)

    yield

    def conv_silu(x_ref, b, idx):
        x = x_ref[b].astype(F32)
        carry_ref[b, idx, 8:SPAN + 8, :] = x
        w4 = cw_ref[:, idx * HD:(idx + 1) * HD]
        y = x * w4[GDN_CONV - 1:GDN_CONV]
        for kk in range(1, GDN_CONV):
            y = y + carry_ref[b, idx, 8 - kk:8 - kk + SPAN, :] * w4[GDN_CONV - 1 - kk:GDN_CONV - kk]
        carry_ref[b, idx, 0:8, :] = x[SPAN - 8:SPAN]
        return _silu(y)

    row = lax.broadcasted_iota(jnp.int32, (SPAN, SPAN), 0)
    col = lax.broadcasted_iota(jnp.int32, (SPAN, SPAN), 1)
    same_chunk = (row // GDN_CHUNK) == (col // GDN_CHUNK)
    tril = same_chunk & (col <= row)
    stril = same_chunk & (col < row)
    eye = (row == col).astype(F32)
    tril16 = tril.astype(BF16)

    units = []
    for b in range(nb):
        q = conv_silu(q_ref, b, 0)
        k = conv_silu(k_ref, b, 1)
        v = conv_silu(v_ref, b, 2)
        ab = ab_ref[b]
        g_full = -jnp.exp(alog_ref[...]) * _softplus(ab + dtb_ref[...])
        beta_full = _sigmoid(ab)
        gc_full = _exact_left_dot(tril16, g_full)
        gc_t = gc_full.T
        for h in range(GDN_HEADS):
            sl = slice(h * GDN_DK, (h + 1) * GDN_DK)
            qh = q[:, sl]
            kh = k[:, sl]
            qh = qh * lax.rsqrt(jnp.sum(qh * qh, -1, keepdims=True) + RMS_EPS) * (GDN_DK ** -0.5)
            kh = kh * lax.rsqrt(jnp.sum(kh * kh, -1, keepdims=True) + RMS_EPS)
            gc = gc_full[:, h:h + 1]
            beta = beta_full[:, GDN_HEADS + h:GDN_HEADS + h + 1]
            decay = jnp.where(tril, jnp.exp(gc - gc_t[h:h + 1, :]), 0.0)
            k16 = kh.astype(BF16)
            a = jnp.where(stril, beta * _dot_nt(k16, k16) * decay, 0.0)
            qk16 = (_dot_nt(qh.astype(BF16), k16) * decay).astype(BF16)
            egc = jnp.exp(gc)
            units.append(dict(
                b=b, h=h, sl=sl, kh=kh, gc=gc, qk16=qk16, a=a,
                rhs=jnp.concatenate([v[:, sl] * beta, kh * (beta * egc)], axis=1).astype(BF16),
                qg=(qh * egc).astype(BF16)))
        yield

    xs = [eye - u["a"] for u in units]
    ps = [u["a"].astype(BF16) for u in units]
    for _ in range(5):
        ps = [_dot(p, p).astype(BF16) for p in ps]
        xs = [x + _dot(x.astype(BF16), p) for x, p in zip(xs, ps)]
        yield
    uws = [_dot(x.astype(BF16), u["rhs"]) for x, u in zip(xs, units)]
    yield

    states = [state_ref[u["b"], u["h"]] for u in units]
    o_state = [[] for _ in units]
    v_new = [[] for _ in units]
    for c in range(NC):
        cs = slice(c * GDN_CHUNK, (c + 1) * GDN_CHUNK)
        s16 = [s.astype(BF16) for s in states]
        vns = [uw[cs, :GDN_DV] - _dot(uw[cs, GDN_DV:].astype(BF16), s) for uw, s in zip(uws, s16)]
        for i, u in enumerate(units):
            o_state[i].append(_dot(u["qg"][cs], s16[i]))
            v_new[i].append(vns[i])
            g_last = u["gc"][(c + 1) * GDN_CHUNK - 1:(c + 1) * GDN_CHUNK, :]
            kd = (u["kh"][cs] * jnp.exp(g_last - u["gc"][cs])).astype(BF16)
            states[i] = states[i] * jnp.exp(g_last) + _dot_tn(kd, vns[i].astype(BF16))
        yield
    for i, u in enumerate(units):
        state_ref[u["b"], u["h"]] = states[i]
        o = jnp.concatenate(o_state[i], axis=0) + _dot(u["qk16"], jnp.concatenate(v_new[i], axis=0).astype(BF16))
        o = _rms(o, ng_ref[...]) * _silu(z_ref[u["b"], :, u["sl"]].astype(F32))
        o_ref[u["b"], :, u["sl"]] = o.astype(o_ref.dtype)
        if i % GDN_HEADS == GDN_HEADS - 1:
            yield


def _gdn_kernel(*refs, nb):
    for _ in _gdn_body(*refs, nb=nb):
        pass


def _gdn(proj16, proj32, conv_w, alog_row, dtb_row, norm_g, nb=2):
    B, S, _ = proj16.shape
    HD = GDN_HEADS * GDN_DK
    blk = lambda w, j: pl.BlockSpec((nb, SPAN, w), lambda b, s, j=j: (b, s, j))
    full = lambda a: pl.BlockSpec(a.shape, lambda b, s: (0,) * a.ndim)
    return pl.pallas_call(
        functools.partial(_gdn_kernel, nb=nb),
        grid=(B // nb, S // SPAN),
        in_specs=[blk(HD, PK_GQ), blk(HD, PK_GK), blk(HD, PK_GV), blk(HD, PK_GZ), blk(LANES, PK32_AB),
                  full(conv_w), full(alog_row), full(dtb_row), full(norm_g)],
        out_specs=blk(HD, 0),
        out_shape=jax.ShapeDtypeStruct((B, S, HD), BF16),
        scratch_shapes=[pltpu.VMEM((nb, 3, SPAN + 8, HD), F32), pltpu.VMEM((nb, GDN_HEADS, GDN_DK, GDN_DV), F32)],
        compiler_params=_params(("parallel", "arbitrary")),
        name="gdn",
    )(proj16, proj16, proj16, proj16, proj32, conv_w, alog_row, dtb_row, norm_g)


def _pool_tile(u_ref, w_ref, sc_ref, carry_ref, si, tp):
    G = POOL_GROUP

    @pl.when(si == 0)
    def _():
        carry_ref[...] = jnp.zeros_like(carry_ref)

    u = u_ref[...].astype(F32)
    ext = jnp.concatenate([carry_ref[...], u], axis=0)
    carry_ref[...] = u[tp - 16:tp]
    t = si * tp + lax.broadcasted_iota(jnp.int32, (tp, 1), 0)
    win_sum = ext
    outs = []
    shift = 1
    for gi, win in enumerate(POOL_WINDOWS):
        while shift < win:
            win_sum = win_sum + pltpu.roll(win_sum, shift, 0)
            shift *= 2
        cnt = jnp.minimum(t + 1, win).astype(F32)
        ug = u[:, gi * G:(gi + 1) * G]
        pooled = win_sum[16:, gi * G:(gi + 1) * G] / cnt - ug
        outs.append(_dot(pooled.astype(BF16), w_ref[gi]))
    return (jnp.concatenate(outs, axis=1) * sc_ref[...]).astype(BF16)


def _mid_broadcast(c, s):
    n = c.shape[0]
    if 2 * s >= 8:
        c3 = c.reshape(n // (2 * s), 2 * s, c.shape[1])
        mid = jnp.broadcast_to(c3[:, s - 1:s, :], c3.shape)
        return mid.reshape(c.shape)
    pos = lax.broadcasted_iota(jnp.int32, c.shape, 0) % (2 * s)
    out = c
    for p in range(2 * s):
        d = p - (s - 1)
        if d != 0:
            out = jnp.where(pos == p, pltpu.roll(c, d % n, 0), out)
    return out


def _hgrn_body(q_ref, f_ref, i_ref, g_ref, lb_ref, ng_ref, o_ref, state_ref, *, nb):
    si = pl.program_id(1)

    @pl.when(si == 0)
    def _():
        state_ref[...] = jnp.zeros_like(state_ref)

    yield

    row = lax.broadcasted_iota(jnp.int32, (SPAN, SPAN), 0)
    col = lax.broadcasted_iota(jnp.int32, (SPAN, SPAN), 1)
    lower16 = (col <= row).astype(BF16)
    heads = [slice(h * HGRN_DK, (h + 1) * HGRN_DK) for h in range(HGRN_HEADS)]
    halves = [slice(0, HALF), slice(HALF, SPAN)]
    ri = lax.broadcasted_iota(jnp.int32, (HALF, HALF), 0)
    ci = lax.broadcasted_iota(jnp.int32, (HALF, HALF), 1)
    rowl = lax.broadcasted_iota(jnp.int32, (SPAN, LANES), 0)
    lb = lb_ref[...]

    seqs = []
    for b in range(nb):
        q = _silu(q_ref[b].astype(F32))
        fp = f_ref[b]
        log_f = jnp.log(lb + (1.0 - lb) * _sigmoid(fp))
        k = (1.0 - lb) * _sigmoid(-fp)
        c = _exact_left_dot(lower16, log_f)
        q16 = q.astype(BF16)
        k16 = k.astype(BF16)
        att = [[jnp.where(ri == ci, _dot_nt(q16[r, sl], k16[r, sl]), 0.0) for sl in heads] for r in halves]
        seqs.append(dict(q=q, k=k, c=c, q16=q16, k16=k16, att=att, cross=None))
        yield

    s = 1
    while s < SPAN:
        sgn = jnp.where(((rowl // s) % 2) == 1, LOG2E, -LOG2E)
        if s < HALF:
            bi = ri // s
            bj = ci // s
            region = ((bi - bj) * 2 + (bj & 1)) == 2
        for sq in seqs:
            diff = sq["c"] - _mid_broadcast(sq["c"], s)
            w16 = jnp.concatenate([jnp.exp2(diff[:, sl] * sgn) for sl in heads], axis=1).astype(BF16)
            ql = sq["q16"] * w16
            kl = sq["k16"] * w16
            if s < HALF:
                for d, r in enumerate(halves):
                    for h, sl in enumerate(heads):
                        sq["att"][d][h] = jnp.where(region, _dot_nt(ql[r, sl], kl[r, sl]), sq["att"][d][h])
            else:
                sq["cross"] = [_dot_nt(ql[halves[1], sl], kl[halves[0], sl]) for sl in heads]
        s *= 2
        yield

    for b, sq in enumerate(seqs):
        c = sq["c"]
        c_last = c[SPAN - 1:SPAN, :]
        v16 = i_ref[b]
        qe = (sq["q"] * jnp.exp(c)).astype(BF16)
        ke = (sq["k"] * jnp.exp(c_last - c)).astype(BF16)
        e_last = jnp.exp(c_last)
        for h, sl in enumerate(heads):
            st = state_ref[b, h]
            v_lo = v16[halves[0], sl]
            v_hi = v16[halves[1], sl]
            o_lo = _dot(sq["att"][0][h].astype(BF16), v_lo)
            o_hi = _dot(sq["cross"][h].astype(BF16), v_lo) + _dot(sq["att"][1][h].astype(BF16), v_hi)
            o = jnp.concatenate([o_lo, o_hi], axis=0) + _dot_nt(qe[:, sl], st.astype(BF16))
            state_ref[b, h] = st * e_last[:, sl] + _dot_tn(v16[:, sl], ke[:, sl])
            o = _rms(o, ng_ref[...]) * _sigmoid(g_ref[b, :, sl].astype(F32))
            o_ref[b, :, sl] = o.astype(o_ref.dtype)
        yield


def _hgrn_kernel(*refs, nb):
    for _ in _hgrn_body(*refs, nb=nb):
        pass


def _recurrent_kernel(*refs, nb):
    g_in, h_in, (g_out, h_out), g_scr, h_scr = refs[:9], refs[9:15], refs[15:17], refs[17:19], refs[19:]
    bodies = [_gdn_body(*g_in, g_out, *g_scr, nb=nb), _hgrn_body(*h_in, h_out, *h_scr, nb=nb)]
    while bodies:
        for body in list(bodies):
            try:
                next(body)
            except StopIteration:
                bodies.remove(body)


def _recurrent(proj16, proj32, conv_w, alog_row, dtb_row, gdn_norm_g, lb_row, hgrn_norm_g, nb=2):
    B, S, _ = proj16.shape
    HD = GDN_HEADS * GDN_DK
    blk = lambda w, j: pl.BlockSpec((nb, SPAN, w), lambda b, s, j=j: (b, s, j))
    full = lambda a: pl.BlockSpec(a.shape, lambda b, s: (0,) * a.ndim)
    out = jax.ShapeDtypeStruct((B, S, HD), BF16)
    return pl.pallas_call(
        functools.partial(_recurrent_kernel, nb=nb),
        grid=(B // nb, S // SPAN),
        in_specs=[blk(HD, PK_GQ), blk(HD, PK_GK), blk(HD, PK_GV), blk(HD, PK_GZ), blk(LANES, PK32_AB),
                  full(conv_w), full(alog_row), full(dtb_row), full(gdn_norm_g),
                  blk(HD, PK_HQ), blk(HD, PK32_HF), blk(HD, PK_HI), blk(HD, PK_HG), full(lb_row), full(hgrn_norm_g)],
        out_specs=[blk(HD, 0), blk(HD, 0)],
        out_shape=[out, out],
        scratch_shapes=[pltpu.VMEM((nb, 3, SPAN + 8, HD), F32), pltpu.VMEM((nb, GDN_HEADS, GDN_DK, GDN_DV), F32),
                        pltpu.VMEM((nb, HGRN_HEADS, HGRN_DV, HGRN_DK), F32)],
        compiler_params=_params(("parallel", "arbitrary")),
        name="gdn_hgrn2",
    )(proj16, proj16, proj16, proj16, proj32, conv_w, alog_row, dtb_row, gdn_norm_g,
      proj16, proj32, proj16, proj16, lb_row, hgrn_norm_g)


def _hgrn(proj16, proj32, lb_row, norm_g, nb=2):
    B, S, _ = proj16.shape
    HD = HGRN_HEADS * HGRN_DK
    blk = lambda j: pl.BlockSpec((nb, SPAN, HD), lambda b, s, j=j: (b, s, j))
    full = lambda a: pl.BlockSpec(a.shape, lambda b, s: (0,) * a.ndim)
    return pl.pallas_call(
        functools.partial(_hgrn_kernel, nb=nb),
        grid=(B // nb, S // SPAN),
        in_specs=[blk(PK_HQ), blk(PK32_HF), blk(PK_HI), blk(PK_HG), full(lb_row), full(norm_g)],
        out_specs=blk(0),
        out_shape=jax.ShapeDtypeStruct((B, S, HD), BF16),
        scratch_shapes=[pltpu.VMEM((nb, HGRN_HEADS, HGRN_DV, HGRN_DK), F32)],
        compiler_params=_params(("parallel", "arbitrary")),
        name="hgrn2",
    )(proj16, proj32, proj16, proj16, lb_row, norm_g)


def _merge_kernel(x_ref, ya_ref, yb_ref, u_ref, yd_ref, pw_ref, psc_ref, wg_ref, wb_ref, wo_ref, g_ref, b_ref,
                  o_ref, carry_ref, *, tiles_per_seq, tm):
    yc = _pool_tile(u_ref, pw_ref, psc_ref, carry_ref, pl.program_id(0) % tiles_per_seq, tm)
    x = x_ref[...]
    x16 = x.astype(BF16)
    merged = None
    for m, y in enumerate((ya_ref[...], yb_ref[...], yc, yd_ref[...])):
        gate = _sigmoid(_dot(x16, wg_ref[:, m * D_MODEL:(m + 1) * D_MODEL]))
        term = gate * _dot(y, wb_ref[m])
        merged = term if merged is None else merged + term
    h = DEEPNORM_ALPHA * x + _dot(merged.astype(BF16), wo_ref[...])
    o_ref[...] = _layer_norm(h, g_ref[...], b_ref[...])


def _merge(x2, ya, yb, proj16, yd, pool_w, pool_scale, w_gate, w_branch, w_out, ln_g, ln_b, S, tm=512):
    T, D = x2.shape
    row = lambda w, j=0: pl.BlockSpec((tm, w), lambda i, j=j: (i, j))
    full = lambda a: pl.BlockSpec(a.shape, lambda i: (0,) * a.ndim, pipeline_mode=pl.Buffered(1))
    return pl.pallas_call(
        functools.partial(_merge_kernel, tiles_per_seq=S // tm, tm=tm),
        grid=(T // tm,),
        in_specs=[row(D), row(BRANCH_WIDTH), row(BRANCH_WIDTH), row(BRANCH_WIDTH, PK_PU), row(BRANCH_WIDTH),
                  full(pool_w), full(pool_scale), full(w_gate), full(w_branch), full(w_out), full(ln_g), full(ln_b)],
        out_specs=row(D),
        out_shape=jax.ShapeDtypeStruct((T, D), F32),
        scratch_shapes=[pltpu.VMEM((16, len(POOL_WINDOWS) * POOL_GROUP), F32)],
        compiler_params=_params(("arbitrary",)),
        name="merge",
    )(x2, ya, yb, proj16, yd, pool_w, pool_scale, w_gate, w_branch, w_out, ln_g, ln_b)


def _ffn_kernel(x_ref, wg_ref, wu_ref, wd_ref, g_ref, b_ref, o_ref):
    x = x_ref[...]
    x16 = x.astype(BF16)
    h = _silu(_dot(x16, wg_ref[...])) * _dot(x16, wu_ref[...])
    y = DEEPNORM_ALPHA * x + _dot(h.astype(BF16), wd_ref[...])
    o_ref[...] = _layer_norm(y, g_ref[...], b_ref[...])


def _ffn(x2, wg, wu, wd, ln_g, ln_b, tm=512):
    T, D = x2.shape
    row = pl.BlockSpec((tm, D), lambda i: (i, 0))
    full = lambda a: pl.BlockSpec(a.shape, lambda i: (0,) * a.ndim, pipeline_mode=pl.Buffered(1))
    return pl.pallas_call(
        _ffn_kernel,
        grid=(T // tm,),
        in_specs=[row, full(wg), full(wu), full(wd), full(ln_g), full(ln_b)],
        out_specs=row,
        out_shape=jax.ShapeDtypeStruct((T, D), F32),
        compiler_params=_params(("parallel",)),
        name="ffn_dense",
    )(x2, wg, wu, wd, ln_g, ln_b)


def _router_kernel(x_ref, w_ref, meta_ref, gate_ref, cnt_ref, base_ref, *, tm):
    i = pl.program_id(0)

    @pl.when(i == 0)
    def _():
        base_ref[...] = jnp.zeros_like(base_ref)

    xh, xm, _ = _split3(x_ref[...])
    wh, wm, _ = _split3(w_ref[...])
    logits = _dot(xh, wh) + _dot(xh, wm) + _dot(xm, wh)
    lane = lax.broadcasted_iota(jnp.int32, (tm, LANES), 1)
    neg = jnp.float32(-jnp.inf)
    logits = jnp.where(lane < N_EXPERTS, logits, neg)
    v1 = jnp.max(logits, -1, keepdims=True)
    e1 = jnp.min(jnp.where(logits == v1, lane, LANES), -1, keepdims=True)
    rest = jnp.where(lane == e1, neg, logits)
    v2 = jnp.max(rest, -1, keepdims=True)
    e2 = jnp.min(jnp.where(rest == v2, lane, LANES), -1, keepdims=True)
    ex = jnp.exp(v2 - v1)
    w1 = 1.0 / (1.0 + ex)
    w2 = ex / (1.0 + ex)
    onehot = ((lane == e1) | (lane == e2)).astype(F32)
    r = lax.broadcasted_iota(jnp.int32, (tm, tm), 0)
    cc = lax.broadcasted_iota(jnp.int32, (tm, tm), 1)
    before = _dot((cc < r).astype(BF16), onehot.astype(BF16)) + base_ref[0:1, :]
    rank1 = jnp.sum(jnp.where(lane == e1, before, 0.0), -1, keepdims=True)
    rank2 = jnp.sum(jnp.where(lane == e2, before, 0.0), -1, keepdims=True)
    total = base_ref[0:1, :] + jnp.sum(onehot, 0, keepdims=True)
    base_ref[...] = jnp.broadcast_to(total, base_ref.shape)
    cnt_ref[...] = jnp.broadcast_to(total, cnt_ref.shape).astype(jnp.int32)
    meta = jnp.where(lane == 0, e1, jnp.where(lane == 1, e2, jnp.where(
        lane == 2, rank1.astype(jnp.int32), jnp.where(lane == 3, rank2.astype(jnp.int32), 0))))
    meta_ref[...] = meta
    gate_ref[...] = jnp.where(lane == 0, w1, jnp.where(lane == 1, w2, 0.0))


def _router(x2, w_router_pad, tm=512):
    T, D = x2.shape
    return pl.pallas_call(
        functools.partial(_router_kernel, tm=tm),
        grid=(T // tm,),
        in_specs=[pl.BlockSpec((tm, D), lambda i: (i, 0)),
                  pl.BlockSpec(w_router_pad.shape, lambda i: (0, 0))],
        out_specs=[pl.BlockSpec((tm, LANES), lambda i: (i, 0)),
                   pl.BlockSpec((tm, LANES), lambda i: (i, 0)),
                   pl.BlockSpec((8, LANES), lambda i: (0, 0))],
        out_shape=[jax.ShapeDtypeStruct((T, LANES), jnp.int32),
                   jax.ShapeDtypeStruct((T, LANES), F32),
                   jax.ShapeDtypeStruct((8, LANES), jnp.int32)],
        scratch_shapes=[pltpu.VMEM((8, LANES), F32)],
        compiler_params=_params(("arbitrary",)),
        name="moe_router",
    )(x2, w_router_pad)


def _dispatch_kernel(zb_ref, dest_ref, x_ref, xb_ref, zero_ref, sem, zsem, *, td):
    @pl.when(pl.program_id(0) == 0)
    def _():
        zero_ref[...] = jnp.zeros_like(zero_ref)
        nz = 2 * N_EXPERTS
        clear = [pltpu.make_async_copy(zero_ref, xb_ref.at[pl.ds(zb_ref[z] * MOE_ROWS, MOE_ROWS)], zsem)
                 for z in range(nz)]
        for z, cp in enumerate(clear):
            pl.when(zb_ref[nz + z] > 0)(cp.start)
        for z, cp in enumerate(clear):
            pl.when(zb_ref[nz + z] > 0)(cp.wait)

    def start(g, c):
        for u in range(8):
            for slot in range(TOP_K):
                pltpu.make_async_copy(x_ref.at[g, pl.ds(u, 1)],
                                      xb_ref.at[pl.ds(dest_ref[0, 0, slot * td + g * 8 + u], 1)],
                                      sem).start(priority=slot)
        return c

    lax.fori_loop(0, td // 8, start, 0)
    pltpu.make_async_copy(xb_ref.at[pl.ds(0, TOP_K * td)], xb_ref.at[pl.ds(0, TOP_K * td)], sem).wait()


def _dispatch(x2, dest_blocks, clear_blocks, n_rows, td=256):
    T, D = x2.shape
    grid_spec = pltpu.PrefetchScalarGridSpec(
        num_scalar_prefetch=1,
        grid=(T // td,),
        in_specs=[pl.BlockSpec((1, 1, TOP_K * td), lambda i, zb: (i, 0, 0), memory_space=pltpu.SMEM),
                  pl.BlockSpec((td // 8, 8, D), lambda i, zb: (i, 0, 0))],
        out_specs=pl.BlockSpec(memory_space=pl.ANY),
        scratch_shapes=[pltpu.VMEM((MOE_ROWS, D), F32), pltpu.SemaphoreType.DMA(()), pltpu.SemaphoreType.DMA(())],
    )
    return pl.pallas_call(
        functools.partial(_dispatch_kernel, td=td),
        grid_spec=grid_spec,
        out_shape=jax.ShapeDtypeStruct((n_rows, D), F32),
        compiler_params=_params(("arbitrary",)),
        name="moe_dispatch",
    )(clear_blocks, dest_blocks, x2.reshape(T // 8, 8, D))


def _expert_kernel(blk_exp_ref, n_used_ref, x_ref, wg_ref, wu_ref, wd_ref, o_ref, acc_ref):
    i = pl.program_id(0)
    j = pl.program_id(1)

    @pl.when(i < n_used_ref[0])
    def _():
        x16 = x_ref[...].astype(BF16)
        h = _silu(_dot(x16, wg_ref[0])) * _dot(x16, wu_ref[0])
        y = _dot(h.astype(BF16), wd_ref[0])

        @pl.when(j == 0)
        def _():
            acc_ref[...] = y

        @pl.when(j > 0)
        def _():
            acc_ref[...] += y

        @pl.when(j == pl.num_programs(1) - 1)
        def _():
            o_ref[...] = acc_ref[...]

    @pl.when(i >= n_used_ref[0])
    def _():
        o_ref[...] = jnp.zeros_like(o_ref)


def _experts(xb, blk_exp, n_used, wg, wu, wd, tf=1792):
    n_rows, D = xb.shape
    n_blk = n_rows // MOE_ROWS
    nf = D_FF_EXPERT // tf

    def rows(i, j, be, nu):
        return (jnp.minimum(i, nu[0] - 1), 0)

    def chunk(i, j, nu):
        ic = jnp.minimum(i, nu[0] - 1)
        return jnp.where(ic % 2 == 0, jnp.where(i < nu[0], j, nf - 1), jnp.where(i < nu[0], nf - 1 - j, 0))

    grid_spec = pltpu.PrefetchScalarGridSpec(
        num_scalar_prefetch=2,
        grid=(n_blk, nf),
        in_specs=[pl.BlockSpec((MOE_ROWS, D), rows),
                  pl.BlockSpec((1, D, tf), lambda i, j, be, nu: (be[jnp.minimum(i, nu[0] - 1)], 0, chunk(i, j, nu))),
                  pl.BlockSpec((1, D, tf), lambda i, j, be, nu: (be[jnp.minimum(i, nu[0] - 1)], 0, chunk(i, j, nu))),
                  pl.BlockSpec((1, tf, D), lambda i, j, be, nu: (be[jnp.minimum(i, nu[0] - 1)], chunk(i, j, nu), 0))],
        out_specs=pl.BlockSpec((MOE_ROWS, D), lambda i, j, be, nu: (i, 0)),
        scratch_shapes=[pltpu.VMEM((MOE_ROWS, D), F32)],
    )
    return pl.pallas_call(
        _expert_kernel,
        grid_spec=grid_spec,
        out_shape=jax.ShapeDtypeStruct((n_rows, D), F32),
        compiler_params=_params(("arbitrary", "arbitrary")),
        name="moe_experts",
    )(blk_exp, n_used, xb, wg, wu, wd)


def _combine_kernel(dest_ref, dnext_ref, x_ref, gate_ref, yb_ref, g_ref, b_ref, o_ref, buf_ref, sem, *, tc):
    i = pl.program_id(0)
    n = pl.num_programs(0)

    def issue(d_ref, slot):
        def start(g, c):
            for u in range(8):
                for kk in range(TOP_K):
                    pltpu.make_async_copy(yb_ref.at[pl.ds(d_ref[0, 0, kk * tc + g * 8 + u], 1)],
                                          buf_ref.at[slot, kk, g, pl.ds(u, 1)], sem.at[slot]).start(priority=kk)
            return c
        lax.fori_loop(0, tc // 8, start, 0)

    @pl.when(i == 0)
    def _():
        issue(dest_ref, 0)

    @pl.when(i + 1 < n)
    def _():
        issue(dnext_ref, (i + 1) % 2)

    slot = i % 2
    pltpu.make_async_copy(buf_ref.at[slot], buf_ref.at[slot], sem.at[slot]).wait()
    gate = gate_ref[...]
    D = x_ref.shape[1]
    ffn = buf_ref[slot, 0].reshape(tc, D) * gate[:, 0:1] + buf_ref[slot, 1].reshape(tc, D) * gate[:, 1:2]
    o_ref[...] = _layer_norm(DEEPNORM_ALPHA * x_ref[...] + ffn, g_ref[...], b_ref[...])


def _combine(x2, gates, yb, dest_blocks, ln_g, ln_b, tc=256):
    T, D = x2.shape
    n = T // tc
    full = lambda a: pl.BlockSpec(a.shape, lambda i: (0,) * a.ndim)
    smem = lambda imap: pl.BlockSpec((1, 1, TOP_K * tc), imap, memory_space=pltpu.SMEM)
    return pl.pallas_call(
        functools.partial(_combine_kernel, tc=tc),
        grid=(n,),
        in_specs=[smem(lambda i: (i, 0, 0)), smem(lambda i: (jnp.minimum(i + 1, n - 1), 0, 0)),
                  pl.BlockSpec((tc, D), lambda i: (i, 0)),
                  pl.BlockSpec((tc, LANES), lambda i: (i, 0)),
                  pl.BlockSpec(memory_space=pl.ANY), full(ln_g), full(ln_b)],
        out_specs=pl.BlockSpec((tc, D), lambda i: (i, 0)),
        out_shape=jax.ShapeDtypeStruct((T, D), F32),
        scratch_shapes=[pltpu.VMEM((2, TOP_K, tc // 8, 8, D), F32), pltpu.SemaphoreType.DMA((2,))],
        compiler_params=_params(("arbitrary",)),
        name="moe_combine",
    )(dest_blocks, dest_blocks, x2, gates, yb, ln_g, ln_b)


def _moe(x2, w_router, wg, wu, wd, ln_g, ln_b, tok=512):
    T, D = x2.shape
    w_router_pad = jnp.pad(w_router.astype(F32), ((0, 0), (0, LANES - N_EXPERTS)))
    meta, gates, counts = _router(x2, w_router_pad)
    counts = counts[0, :N_EXPERTS]
    padded = (counts + MOE_ROWS - 1) // MOE_ROWS * MOE_ROWS
    pends = jnp.cumsum(padded)
    pstarts = pends - padded
    dest = pstarts[meta[:, 0:2]] + meta[:, 2:4]
    dest_blocks = dest.reshape(T // tok, tok, 2).transpose(0, 2, 1).reshape(T // tok, 1, 2 * tok)
    n_blk = (T * TOP_K + MOE_ROWS - 1) // MOE_ROWS + N_EXPERTS
    n_rows = n_blk * MOE_ROWS
    blk_start = jnp.arange(n_blk, dtype=jnp.int32) * MOE_ROWS
    blk_exp = jnp.minimum(jnp.sum((blk_start[:, None] >= pends[None, :]).astype(jnp.int32), axis=1), N_EXPERTS - 1)
    n_used = (pends[-1:] // MOE_ROWS).astype(jnp.int32)
    tail = n_used + jnp.arange(N_EXPERTS, dtype=jnp.int32)
    clear_blocks = jnp.concatenate([jnp.maximum(pends // MOE_ROWS - 1, 0), jnp.minimum(tail, n_blk - 1),
                                    counts > 0, tail < n_blk]).astype(jnp.int32)
    xb = _dispatch(x2, dest_blocks, clear_blocks, n_rows, td=tok)
    yb = _experts(xb, blk_exp, n_used, wg, wu, wd)
    return _combine(x2, gates, yb, dest_blocks, ln_g, ln_b, tc=tok)


def _rot_half_cols(w):
    half = w.shape[-1] // 2
    return jnp.concatenate([-w[..., half:], w[..., :half]], axis=-1)


def _pack_w_in(w):
    D = w.shape[0]
    kr = w[:, 384:448]
    z64 = jnp.zeros((D, 64), w.dtype)
    w16 = jnp.concatenate([
        w[:, 448:2496],
        w[:, 2504:3016],
        w[:, 3016:3528],
        w[:, 4040:5064],
        w[:, 0:256],
        w[:, 256:384],
        kr, z64,
        _rot_half_cols(kr), z64,
    ], axis=1)
    w32 = jnp.concatenate([w[:, 3528:4040], w[:, 2496:2504], jnp.zeros((D, LANES - 8), w.dtype)], axis=1)
    return w16.astype(BF16), w32.astype(BF16), w[:, 5064:].astype(BF16)


def _pack_mla(w_uq, w_ukv):
    R = w_uq.shape[0]
    z64 = jnp.zeros((R, 64), w_uq.dtype)
    wq, wqr = [], []
    for h in range(MLA_HEADS):
        o = h * (MLA_NOPE + MLA_ROPE)
        rope_cols = w_uq[:, o + MLA_NOPE:o + MLA_NOPE + MLA_ROPE]
        wq += [w_uq[:, o:o + MLA_NOPE], rope_cols, z64]
        wqr += [_rot_half_cols(rope_cols), z64]
    kv = w_ukv.reshape(w_ukv.shape[0], MLA_HEADS, MLA_NOPE + MLA_V)
    wk = kv[:, :, :MLA_NOPE].reshape(w_ukv.shape[0], MLA_HEADS * MLA_NOPE)
    wv = kv[:, :, MLA_NOPE:].reshape(w_ukv.shape[0], MLA_HEADS * MLA_V)
    return (jnp.concatenate(wq, 1).astype(BF16), jnp.concatenate(wqr, 1).astype(BF16),
            wk.astype(BF16), wv.astype(BF16))


def _lane_row(v, offset=0):
    v = v.astype(F32)
    return jnp.pad(v, (offset, LANES - offset - v.shape[0]))[None, :]


def kernel(x, positions, w_in, mla_q_norm, mla_w_uq, mla_kv_norm, mla_w_ukv, gdn_conv, gdn_a_log, gdn_dt_bias, gdn_norm, pool_w, pool_scale, hgrn_lb_logits, hgrn_norm, w_branch, w_out, ln_mix_g, ln_mix_b, ffn_w_gate, ffn_w_up, ffn_w_down, moe_router, moe_w_gate, moe_w_up, moe_w_down, ln_ffn_g, ln_ffn_b):
    B, S, D = x.shape
    T = B * S
    x2 = x.reshape(T, D)

    half = MLA_ROPE // 2
    inv = ROPE_THETA ** (-jnp.arange(half, dtype=F32) / half)
    inv_row = jnp.concatenate([inv, inv, jnp.zeros((LANES - MLA_ROPE,), F32)])[None, :]
    cos_t, sin_t = _rope_tables(positions.reshape(T, 1).astype(jnp.int32), inv_row)

    p_lb = jax.nn.softmax(hgrn_lb_logits.astype(F32), axis=0)
    lower_bounds = jnp.cumsum(p_lb, axis=0) - p_lb[0]
    row2 = lambda v: v.astype(F32)[None, :]

    for l in range(DEPTH):
        w16, w32, w_gate = _pack_w_in(w_in[l])
        proj16, proj32 = _inproj(x2, w16, w32)
        wq, wqr, wk, wv = _pack_mla(mla_w_uq[l], mla_w_ukv[l])
        y_a = _mla(proj16, cos_t, sin_t, row2(mla_q_norm[l]), row2(mla_kv_norm[l]), wq, wqr, wk, wv, B, S)
        proj16_3 = proj16.reshape(B, S, PK16_WIDTH)
        proj32_3 = proj32.reshape(B, S, PK32_WIDTH)
        y_b, y_d = _recurrent(proj16_3, proj32_3, gdn_conv[l].astype(F32), _lane_row(gdn_a_log[l]),
                              _lane_row(gdn_dt_bias[l]), row2(gdn_norm[l]), row2(lower_bounds[l]), row2(hgrn_norm[l]))
        y_b = y_b.reshape(T, GDN_HEADS * GDN_DV)
        y_d = y_d.reshape(T, HGRN_HEADS * HGRN_DV)
        x2 = _merge(x2, y_a, y_b, proj16, y_d, pool_w[l].astype(BF16), row2(pool_scale[l]), w_gate,
                    w_branch[l].astype(BF16), w_out[l].astype(BF16), row2(ln_mix_g[l]), row2(ln_mix_b[l]), S)
        j = l // 2
        if l % 2 == 0:
            x2 = _ffn(x2, ffn_w_gate[j].astype(BF16), ffn_w_up[j].astype(BF16), ffn_w_down[j].astype(BF16),
                      row2(ln_ffn_g[l]), row2(ln_ffn_b[l]))
        else:
            x2 = _moe(x2, moe_router[j], moe_w_gate[j].astype(BF16), moe_w_up[j].astype(BF16),
                      moe_w_down[j].astype(BF16), row2(ln_ffn_g[l]), row2(ln_ffn_b[l]))
    return x2.reshape(B, S, D)
```

```python
import functools
import math

import jax
import jax.numpy as jnp
from jax import lax
from jax.experimental import pallas as pl
from jax.experimental.pallas import tpu as pltpu

F32 = jnp.float32
BF16 = jnp.bfloat16

D_MODEL = 1024
DEPTH = 2
MLA_HEADS = 4
MLA_Q_RANK = 256
MLA_KV_RANK = 128
MLA_NOPE = 128
MLA_ROPE = 64
MLA_V = 128
ROPE_THETA = 10000.0
MASK_VALUE = -1e30
GDN_HEADS = 4
GDN_DK = 128
GDN_DV = 128
GDN_CONV = 4
GDN_CHUNK = 64
POOL_WINDOWS = (2, 4, 8, 16)
POOL_GROUP = 128
HGRN_HEADS = 4
HGRN_DK = 128
HGRN_DV = 128
N_BRANCH = 4
BRANCH_WIDTH = 512
D_FF = 2816
N_EXPERTS = 8
TOP_K = 2
D_FF_EXPERT = 3584
DEEPNORM_ALPHA = (2 * DEPTH) ** 0.25
LN_EPS = 1e-5
RMS_EPS = 1e-6
LOG2E = math.log2(math.e)

LANES = 128
MLA_HEAD_PAD = 256
SPAN = 256
HALF = SPAN // 2
MOE_ROWS = 512
VMEM_LIMIT = 56 * 1024 * 1024

PK_GQ, PK_GK, PK_GV, PK_GZ, PK_PU, PK_HQ, PK_HI, PK_HG = range(8)
PK_CQ = 16
PK_CKV = 34
PK_KRA = 35
PK_KRB = 36
PK16_WIDTH = 37 * LANES
PK32_HF = 0
PK32_AB = 4
PK32_WIDTH = 5 * LANES


def _params(sem, vmem=VMEM_LIMIT):
    return pltpu.CompilerParams(dimension_semantics=sem, vmem_limit_bytes=vmem)


def _dot(a, b):
    return jnp.dot(a, b, preferred_element_type=F32)


def _dot_nt(a, b):
    return lax.dot_general(a, b, (((1,), (1,)), ((), ())), preferred_element_type=F32)


def _dot_tn(a, b):
    return lax.dot_general(a, b, (((0,), (0,)), ((), ())), preferred_element_type=F32)


def _split3(x):
    hi = x.astype(BF16)
    r1 = x - hi.astype(F32)
    mid = r1.astype(BF16)
    lo = (r1 - mid.astype(F32)).astype(BF16)
    return hi, mid, lo


def _exact_left_dot(m01, x):
    hi, mid, lo = _split3(x)
    return _dot(m01, hi) + _dot(m01, mid) + _dot(m01, lo)


def _sigmoid(x):
    return 1.0 / (1.0 + jnp.exp(-x))


def _silu(x):
    return x * _sigmoid(x)


def _softplus(x):
    return jnp.maximum(x, 0.0) + jnp.log(1.0 + jnp.exp(-jnp.abs(x)))


def _layer_norm(x, g, b):
    mu = jnp.mean(x, -1, keepdims=True)
    xc = x - mu
    var = jnp.mean(xc * xc, -1, keepdims=True)
    return xc * lax.rsqrt(var + LN_EPS) * g + b


def _rms(x, g):
    return x * lax.rsqrt(jnp.mean(x * x, -1, keepdims=True) + RMS_EPS) * g


def _lane_blocks(s):
    return [s[:, c * LANES:(c + 1) * LANES] for c in range(s.shape[1] // LANES)]


def _row_max(s):
    return jnp.max(functools.reduce(jnp.maximum, _lane_blocks(s)), -1, keepdims=True)


def _row_sum(s):
    return jnp.sum(functools.reduce(jnp.add, _lane_blocks(s)), -1, keepdims=True)


def _inproj_kernel(x_ref, w16_ref, w32_ref, o16_ref, o32_ref):
    x16 = x_ref[...].astype(BF16)
    o16_ref[...] = _dot(x16, w16_ref[...]).astype(o16_ref.dtype)
    o32_ref[...] = _dot(x16, w32_ref[...])


def _inproj(x2, w16, w32, tm=512):
    T, D = x2.shape
    full = lambda a: pl.BlockSpec(a.shape, lambda i: (0, 0), pipeline_mode=pl.Buffered(1))
    return pl.pallas_call(
        _inproj_kernel,
        grid=(T // tm,),
        in_specs=[pl.BlockSpec((tm, D), lambda i: (i, 0)), full(w16), full(w32)],
        out_specs=[pl.BlockSpec((tm, w16.shape[1]), lambda i: (i, 0)),
                   pl.BlockSpec((tm, w32.shape[1]), lambda i: (i, 0))],
        out_shape=[jax.ShapeDtypeStruct((T, w16.shape[1]), BF16),
                   jax.ShapeDtypeStruct((T, w32.shape[1]), F32)],
        compiler_params=_params(("parallel",)),
        name="inproj",
    )(x2, w16, w32)


def _rope_table_kernel(pos_ref, inv_ref, off_ref, c_ref, s_ref):
    cs = jnp.cos(pos_ref[...].astype(F32) * inv_ref[...] + off_ref[...])
    c_ref[...] = cs
    s_ref[...] = pltpu.roll(cs, LANES // 2, 1)


def _rope_tables(pos_col, inv_row, off_row, tm=1024):
    T = pos_col.shape[0]
    return pl.pallas_call(
        _rope_table_kernel,
        grid=(T // tm,),
        in_specs=[pl.BlockSpec((tm, 1), lambda i: (i, 0)),
                  pl.BlockSpec((1, LANES), lambda i: (0, 0)),
                  pl.BlockSpec((1, LANES), lambda i: (0, 0))],
        out_specs=[pl.BlockSpec((tm, LANES), lambda i: (i, 0))] * 2,
        out_shape=[jax.ShapeDtypeStruct((T, LANES), F32)] * 2,
        compiler_params=_params(("parallel",)),
        name="rope_tables",
    )(pos_col, inv_row, off_row)


def _mla_prep_rows(rs, cq_ref, ckv_ref, kra_ref, krb_ref, c_ref, s_ref, gq_ref, gkv_ref,
                   wq_ref, wqr_ref, wk_ref, wv_ref, q_s, k_s, v_s):
    scale = (MLA_NOPE + MLA_ROPE) ** -0.5 * LOG2E
    cos = c_ref[rs, :]
    sin = s_ref[rs, :]
    nq = _rms(cq_ref[rs, :].astype(F32), gq_ref[...]).astype(BF16)
    q_pre = _dot(nq, wq_ref[...])
    q_rot = _dot(nq, wqr_ref[...])
    nkv = _rms(ckv_ref[rs, :].astype(F32), gkv_ref[...]).astype(BF16)
    k_nope = _dot(nkv, wk_ref[...])
    v_s[rs, :] = _dot(nkv, wv_ref[...]).astype(v_s.dtype)
    k_rope = (kra_ref[rs, :].astype(F32) * cos + krb_ref[rs, :].astype(F32) * sin).astype(k_s.dtype)
    for h in range(MLA_HEADS):
        o = h * MLA_HEAD_PAD
        q_s[rs, o:o + LANES] = (q_pre[:, o:o + LANES] * scale).astype(q_s.dtype)
        q_s[rs, o + LANES:o + 2 * LANES] = (
            (q_pre[:, o + LANES:o + 2 * LANES] * cos + q_rot[:, h * LANES:(h + 1) * LANES] * sin) * scale
        ).astype(q_s.dtype)
        k_s[rs, o:o + LANES] = k_nope[:, h * LANES:(h + 1) * LANES].astype(k_s.dtype)
        k_s[rs, o + LANES:o + 2 * LANES] = k_rope


def _attn_tile(q_s, k_s, v_s, o_ref, t, tq, causal):
    r0 = t * tq
    for h in range(MLA_HEADS):
        hs = slice(h * MLA_HEAD_PAD, (h + 1) * MLA_HEAD_PAD)
        vs = slice(h * MLA_V, (h + 1) * MLA_V)
        qh = q_s[r0:r0 + tq, hs]
        s_d = jnp.where(causal, _dot_nt(qh, k_s[r0:r0 + tq, hs]), MASK_VALUE)
        m = _row_max(s_d)
        if t > 0:
            s_o = _dot_nt(qh, k_s[0:r0, hs])
            m = jnp.maximum(m, _row_max(s_o))
        p_d = jnp.exp2(s_d - m)
        l = _row_sum(p_d)
        acc = _dot(p_d.astype(BF16), v_s[r0:r0 + tq, vs])
        if t > 0:
            p_o = jnp.exp2(s_o - m)
            l = l + _row_sum(p_o)
            acc = acc + _dot(p_o.astype(BF16), v_s[0:r0, vs])
        o_ref[:, vs] = (acc / l).astype(o_ref.dtype)


def _mla_kernel(*refs, seq, tq, tprep):
    prep_refs, o_ref, (q_s, k_s, v_s) = refs[:12], refs[12], refs[13:]
    i = pl.program_id(1)

    @pl.when(i == 0)
    def _():
        for c in range(seq // tprep):
            _mla_prep_rows(slice(c * tprep, (c + 1) * tprep), *prep_refs, q_s, k_s, v_s)

    row = lax.broadcasted_iota(jnp.int32, (tq, tq), 0)
    col = lax.broadcasted_iota(jnp.int32, (tq, tq), 1)
    causal = row >= col
    for t in range(seq // tq):
        pl.when(i == t)(functools.partial(_attn_tile, q_s, k_s, v_s, o_ref, t, tq, causal))


def _mla(proj, cos_t, sin_t, g_q, g_kv, wq, wqr, wk, wv, B, S, tq=256, tprep=512):
    T = proj.shape[0]
    nq = S // tq
    HW = MLA_HEADS * MLA_HEAD_PAD
    HV = MLA_HEADS * MLA_V
    seq = lambda w, j: pl.BlockSpec((S, w), lambda b, i, j=j: (b, j))
    full = lambda a: pl.BlockSpec(a.shape, lambda b, i: (0,) * a.ndim)
    return pl.pallas_call(
        functools.partial(_mla_kernel, seq=S, tq=tq, tprep=min(tprep, S)),
        grid=(B, nq),
        in_specs=[seq(256, PK_CQ), seq(128, PK_CKV), seq(128, PK_KRA), seq(128, PK_KRB),
                  seq(128, 0), seq(128, 0), full(g_q), full(g_kv), full(wq), full(wqr), full(wk), full(wv)],
        out_specs=pl.BlockSpec((tq, HV), lambda b, i: (b * nq + i, 0)),
        out_shape=jax.ShapeDtypeStruct((T, HV), BF16),
        scratch_shapes=[pltpu.VMEM((S, HW), BF16), pltpu.VMEM((S, HW), BF16), pltpu.VMEM((S, HV), BF16)],
        compiler_params=_params(("parallel", "arbitrary")),
        name="mla",
    )(proj, proj, proj, proj, cos_t, sin_t, g_q, g_kv, wq, wqr, wk, wv)


def _gdn_body(q_ref, k_ref, v_ref, z_ref, ab_ref, cw_ref, sh_ref, alog_ref, dtb_ref, ng_ref, o_ref,
              carry_ref, state_ref, *, nb):
    si = pl.program_id(1)
    HD = GDN_HEADS * GDN_DK
    NC = SPAN // GDN_CHUNK

    @pl.when(si == 0)
    def _():
        carry_ref[...] = jnp.zeros_like(carry_ref)
        state_ref[...] = jnp.zeros_like(state_ref)

    yield

    def conv_silu(x_ref, b, idx):
        x16 = x_ref[b]
        w16 = cw_ref[:, idx * HD:(idx + 1) * HD].astype(BF16)
        taps = [w16[GDN_CONV - 1 - kk:GDN_CONV - kk] for kk in range(GDN_CONV)]
        y = _dot(sh_ref[...], jnp.concatenate([x16 * tap for tap in taps], axis=0))
        r8 = lax.broadcasted_iota(jnp.int32, (8, HD), 0)
        head = y[0:8]
        for kk in range(1, GDN_CONV):
            term = (pltpu.roll(carry_ref[b, idx], kk, 0).astype(BF16) * taps[kk]).astype(F32)
            head = head + jnp.where(r8 < kk, term, 0.0)
        carry_ref[b, idx] = x16[SPAN - 8:SPAN].astype(F32)
        return _silu(jnp.concatenate([head, y[8:]], axis=0))

    row = lax.broadcasted_iota(jnp.int32, (SPAN, SPAN), 0)
    col = lax.broadcasted_iota(jnp.int32, (SPAN, SPAN), 1)
    same_chunk = (row // GDN_CHUNK) == (col // GDN_CHUNK)
    tril = same_chunk & (col <= row)
    stril = same_chunk & (col < row)
    eye = (row == col).astype(F32)
    tril16 = tril.astype(BF16)

    units = []
    for b in range(nb):
        q = conv_silu(q_ref, b, 0)
        k = conv_silu(k_ref, b, 1)
        v = conv_silu(v_ref, b, 2)
        ab = ab_ref[b]
        g_full = -jnp.exp(alog_ref[...]) * _softplus(ab + dtb_ref[...])
        beta_full = _sigmoid(ab)
        gc_full = _exact_left_dot(tril16, g_full)
        gc_t = gc_full.T
        for h in range(GDN_HEADS):
            sl = slice(h * GDN_DK, (h + 1) * GDN_DK)
            qh = q[:, sl]
            kh = k[:, sl]
            qh = qh * lax.rsqrt(jnp.sum(qh * qh, -1, keepdims=True) + RMS_EPS) * (GDN_DK ** -0.5)
            kh = kh * lax.rsqrt(jnp.sum(kh * kh, -1, keepdims=True) + RMS_EPS)
            gc = gc_full[:, h:h + 1]
            beta = beta_full[:, GDN_HEADS + h:GDN_HEADS + h + 1]
            decay = jnp.where(tril, jnp.exp(gc - gc_t[h:h + 1, :]), 0.0)
            k16 = kh.astype(BF16)
            a = jnp.where(stril, beta * _dot_nt(k16, k16) * decay, 0.0)
            qk16 = (_dot_nt(qh.astype(BF16), k16) * decay).astype(BF16)
            egc = jnp.exp(gc)
            units.append(dict(
                b=b, h=h, sl=sl, kh=kh, gc=gc, qk16=qk16, a=a,
                rhs=jnp.concatenate([v[:, sl] * beta, kh * (beta * egc)], axis=1).astype(BF16),
                qg=(qh * egc).astype(BF16)))
        yield

    xs = [eye - u["a"] for u in units]
    ps = [u["a"].astype(BF16) for u in units]
    for _ in range(5):
        ps = [_dot(p, p).astype(BF16) for p in ps]
        xs = [x + _dot(x.astype(BF16), p) for x, p in zip(xs, ps)]
        yield
    uws = [_dot(x.astype(BF16), u["rhs"]) for x, u in zip(xs, units)]
    yield

    states = [state_ref[u["b"], u["h"]] for u in units]
    o_state = [[] for _ in units]
    v_new = [[] for _ in units]
    for c in range(NC):
        cs = slice(c * GDN_CHUNK, (c + 1) * GDN_CHUNK)
        s16 = [s.astype(BF16) for s in states]
        vns = [uw[cs, :GDN_DV] - _dot(uw[cs, GDN_DV:].astype(BF16), s) for uw, s in zip(uws, s16)]
        for i, u in enumerate(units):
            o_state[i].append(_dot(u["qg"][cs], s16[i]))
            v_new[i].append(vns[i])
            g_last = u["gc"][(c + 1) * GDN_CHUNK - 1:(c + 1) * GDN_CHUNK, :]
            kd = (u["kh"][cs] * jnp.exp(g_last - u["gc"][cs])).astype(BF16)
            states[i] = states[i] * jnp.exp(g_last) + _dot_tn(kd, vns[i].astype(BF16))
        yield
    for i, u in enumerate(units):
        state_ref[u["b"], u["h"]] = states[i]
        o = jnp.concatenate(o_state[i], axis=0) + _dot(u["qk16"], jnp.concatenate(v_new[i], axis=0).astype(BF16))
        o = _rms(o, ng_ref[...]) * _silu(z_ref[u["b"], :, u["sl"]].astype(F32))
        o_ref[u["b"], :, u["sl"]] = o.astype(o_ref.dtype)
        if i % GDN_HEADS == GDN_HEADS - 1:
            yield


def _gdn_kernel(*refs, nb):
    for _ in _gdn_body(*refs, nb=nb):
        pass


def _gdn(proj16, proj32, conv_w, shift, alog_row, dtb_row, norm_g, nb=2):
    B, S, _ = proj16.shape
    HD = GDN_HEADS * GDN_DK
    blk = lambda w, j: pl.BlockSpec((nb, SPAN, w), lambda b, s, j=j: (b, s, j))
    full = lambda a: pl.BlockSpec(a.shape, lambda b, s: (0,) * a.ndim)
    return pl.pallas_call(
        functools.partial(_gdn_kernel, nb=nb),
        grid=(B // nb, S // SPAN),
        in_specs=[blk(HD, PK_GQ), blk(HD, PK_GK), blk(HD, PK_GV), blk(HD, PK_GZ), blk(LANES, PK32_AB),
                  full(conv_w), full(shift), full(alog_row), full(dtb_row), full(norm_g)],
        out_specs=blk(HD, 0),
        out_shape=jax.ShapeDtypeStruct((B, S, HD), BF16),
        scratch_shapes=[pltpu.VMEM((nb, 3, 8, HD), F32), pltpu.VMEM((nb, GDN_HEADS, GDN_DK, GDN_DV), F32)],
        compiler_params=_params(("parallel", "arbitrary")),
        name="gdn",
    )(proj16, proj16, proj16, proj16, proj32, conv_w, shift, alog_row, dtb_row, norm_g)


def _pool_tile(u_ref, w_ref, sc_ref, carry_ref, si, tp):
    G = POOL_GROUP

    @pl.when(si == 0)
    def _():
        carry_ref[...] = jnp.zeros_like(carry_ref)

    u = u_ref[...].astype(F32)
    ext = jnp.concatenate([carry_ref[...], u], axis=0)
    carry_ref[...] = u[tp - 16:tp]
    t = si * tp + lax.broadcasted_iota(jnp.int32, (tp, 1), 0)
    win_sum = ext
    outs = []
    shift = 1
    for gi, win in enumerate(POOL_WINDOWS):
        while shift < win:
            win_sum = win_sum + pltpu.roll(win_sum, shift, 0)
            shift *= 2
        cnt = jnp.minimum(t + 1, win).astype(F32)
        ug = u[:, gi * G:(gi + 1) * G]
        pooled = win_sum[16:, gi * G:(gi + 1) * G] / cnt - ug
        outs.append(_dot(pooled.astype(BF16), w_ref[gi]))
    return (jnp.concatenate(outs, axis=1) * sc_ref[...]).astype(BF16)


def _mid_broadcast(c, s):
    n = c.shape[0]
    if 2 * s >= 8:
        c3 = c.reshape(n // (2 * s), 2 * s, c.shape[1])
        mid = jnp.broadcast_to(c3[:, s - 1:s, :], c3.shape)
        return mid.reshape(c.shape)
    pos = lax.broadcasted_iota(jnp.int32, c.shape, 0) % (2 * s)
    out = c
    for p in range(2 * s):
        d = p - (s - 1)
        if d != 0:
            out = jnp.where(pos == p, pltpu.roll(c, d % n, 0), out)
    return out


def _hgrn_body(q_ref, f_ref, i_ref, g_ref, lb_ref, ng_ref, o_ref, state_ref, *, nb):
    si = pl.program_id(1)

    @pl.when(si == 0)
    def _():
        state_ref[...] = jnp.zeros_like(state_ref)

    yield

    row = lax.broadcasted_iota(jnp.int32, (SPAN, SPAN), 0)
    col = lax.broadcasted_iota(jnp.int32, (SPAN, SPAN), 1)
    lower16 = (col <= row).astype(BF16)
    heads = [slice(h * HGRN_DK, (h + 1) * HGRN_DK) for h in range(HGRN_HEADS)]
    halves = [slice(0, HALF), slice(HALF, SPAN)]
    ri = lax.broadcasted_iota(jnp.int32, (HALF, HALF), 0)
    ci = lax.broadcasted_iota(jnp.int32, (HALF, HALF), 1)
    rowl = lax.broadcasted_iota(jnp.int32, (SPAN, LANES), 0)
    lb = lb_ref[...]

    seqs = []
    for b in range(nb):
        q = _silu(q_ref[b].astype(F32))
        fp = f_ref[b]
        log_f = jnp.log(lb + (1.0 - lb) * _sigmoid(fp))
        k = (1.0 - lb) * _sigmoid(-fp)
        c = _exact_left_dot(lower16, log_f)
        q16 = q.astype(BF16)
        k16 = k.astype(BF16)
        att = [[jnp.where(ri == ci, _dot_nt(q16[r, sl], k16[r, sl]), 0.0) for sl in heads] for r in halves]
        seqs.append(dict(q=q, k=k, c=c, q16=q16, k16=k16, att=att, cross=None))
        yield

    s = 1
    while s < SPAN:
        sgn = jnp.where(((rowl // s) % 2) == 1, LOG2E, -LOG2E)
        if s < HALF:
            bi = ri // s
            bj = ci // s
            region = ((bi - bj) * 2 + (bj & 1)) == 2
        for sq in seqs:
            diff = sq["c"] - _mid_broadcast(sq["c"], s)
            w16 = jnp.concatenate([jnp.exp2(diff[:, sl] * sgn) for sl in heads], axis=1).astype(BF16)
            ql = sq["q16"] * w16
            kl = sq["k16"] * w16
            if s < HALF:
                for d, r in enumerate(halves):
                    for h, sl in enumerate(heads):
                        sq["att"][d][h] = jnp.where(region, _dot_nt(ql[r, sl], kl[r, sl]), sq["att"][d][h])
            else:
                sq["cross"] = [_dot_nt(ql[halves[1], sl], kl[halves[0], sl]) for sl in heads]
        s *= 2
        yield

    for b, sq in enumerate(seqs):
        c = sq["c"]
        c_last = c[SPAN - 1:SPAN, :]
        v16 = i_ref[b]
        qe = (sq["q"] * jnp.exp(c)).astype(BF16)
        ke = (sq["k"] * jnp.exp(c_last - c)).astype(BF16)
        e_last = jnp.exp(c_last)
        for h, sl in enumerate(heads):
            st = state_ref[b, h]
            v_lo = v16[halves[0], sl]
            v_hi = v16[halves[1], sl]
            o_lo = _dot(sq["att"][0][h].astype(BF16), v_lo)
            o_hi = _dot(sq["cross"][h].astype(BF16), v_lo) + _dot(sq["att"][1][h].astype(BF16), v_hi)
            o = jnp.concatenate([o_lo, o_hi], axis=0) + _dot_nt(qe[:, sl], st.astype(BF16))
            state_ref[b, h] = st * e_last[:, sl] + _dot_tn(v16[:, sl], ke[:, sl])
            o = _rms(o, ng_ref[...]) * _sigmoid(g_ref[b, :, sl].astype(F32))
            o_ref[b, :, sl] = o.astype(o_ref.dtype)
        yield


def _hgrn_kernel(*refs, nb):
    for _ in _hgrn_body(*refs, nb=nb):
        pass


def _recurrent_kernel(*refs, nb):
    g_in, h_in, (g_out, h_out), g_scr, h_scr = refs[:10], refs[10:16], refs[16:18], refs[18:20], refs[20:]
    bodies = [_gdn_body(*g_in, g_out, *g_scr, nb=nb), _hgrn_body(*h_in, h_out, *h_scr, nb=nb)]
    while bodies:
        for body in list(bodies):
            try:
                next(body)
            except StopIteration:
                bodies.remove(body)


def _recurrent(proj16, proj32, conv_w, shift, alog_row, dtb_row, gdn_norm_g, lb_row, hgrn_norm_g, nb=2):
    B, S, _ = proj16.shape
    HD = GDN_HEADS * GDN_DK
    blk = lambda w, j: pl.BlockSpec((nb, SPAN, w), lambda b, s, j=j: (b, s, j))
    full = lambda a: pl.BlockSpec(a.shape, lambda b, s: (0,) * a.ndim)
    out = jax.ShapeDtypeStruct((B, S, HD), BF16)
    return pl.pallas_call(
        functools.partial(_recurrent_kernel, nb=nb),
        grid=(B // nb, S // SPAN),
        in_specs=[blk(HD, PK_GQ), blk(HD, PK_GK), blk(HD, PK_GV), blk(HD, PK_GZ), blk(LANES, PK32_AB),
                  full(conv_w), full(shift), full(alog_row), full(dtb_row), full(gdn_norm_g),
                  blk(HD, PK_HQ), blk(HD, PK32_HF), blk(HD, PK_HI), blk(HD, PK_HG), full(lb_row), full(hgrn_norm_g)],
        out_specs=[blk(HD, 0), blk(HD, 0)],
        out_shape=[out, out],
        scratch_shapes=[pltpu.VMEM((nb, 3, 8, HD), F32), pltpu.VMEM((nb, GDN_HEADS, GDN_DK, GDN_DV), F32),
                        pltpu.VMEM((nb, HGRN_HEADS, HGRN_DV, HGRN_DK), F32)],
        compiler_params=_params(("parallel", "arbitrary")),
        name="gdn_hgrn2",
    )(proj16, proj16, proj16, proj16, proj32, conv_w, shift, alog_row, dtb_row, gdn_norm_g,
      proj16, proj32, proj16, proj16, lb_row, hgrn_norm_g)


def _hgrn(proj16, proj32, lb_row, norm_g, nb=2):
    B, S, _ = proj16.shape
    HD = HGRN_HEADS * HGRN_DK
    blk = lambda j: pl.BlockSpec((nb, SPAN, HD), lambda b, s, j=j: (b, s, j))
    full = lambda a: pl.BlockSpec(a.shape, lambda b, s: (0,) * a.ndim)
    return pl.pallas_call(
        functools.partial(_hgrn_kernel, nb=nb),
        grid=(B // nb, S // SPAN),
        in_specs=[blk(PK_HQ), blk(PK32_HF), blk(PK_HI), blk(PK_HG), full(lb_row), full(norm_g)],
        out_specs=blk(0),
        out_shape=jax.ShapeDtypeStruct((B, S, HD), BF16),
        scratch_shapes=[pltpu.VMEM((nb, HGRN_HEADS, HGRN_DV, HGRN_DK), F32)],
        compiler_params=_params(("parallel", "arbitrary")),
        name="hgrn2",
    )(proj16, proj32, proj16, proj16, lb_row, norm_g)


def _merge_kernel(x_ref, ya_ref, yb_ref, u_ref, yd_ref, pw_ref, psc_ref, wg_ref, wb_ref, wo_ref, g_ref, b_ref,
                  o_ref, carry_ref, *, tiles_per_seq, tm):
    yc = _pool_tile(u_ref, pw_ref, psc_ref, carry_ref, pl.program_id(0) % tiles_per_seq, tm)
    x = x_ref[...]
    x16 = x.astype(BF16)
    merged = None
    for m, y in enumerate((ya_ref[...], yb_ref[...], yc, yd_ref[...])):
        gate = _sigmoid(_dot(x16, wg_ref[:, m * D_MODEL:(m + 1) * D_MODEL]))
        term = gate * _dot(y, wb_ref[m])
        merged = term if merged is None else merged + term
    h = DEEPNORM_ALPHA * x + _dot(merged.astype(BF16), wo_ref[...])
    o_ref[...] = _layer_norm(h, g_ref[...], b_ref[...])


def _merge(x2, ya, yb, proj16, yd, pool_w, pool_scale, w_gate, w_branch, w_out, ln_g, ln_b, S, tm=512):
    T, D = x2.shape
    row = lambda w, j=0: pl.BlockSpec((tm, w), lambda i, j=j: (i, j))
    full = lambda a: pl.BlockSpec(a.shape, lambda i: (0,) * a.ndim, pipeline_mode=pl.Buffered(1))
    return pl.pallas_call(
        functools.partial(_merge_kernel, tiles_per_seq=S // tm, tm=tm),
        grid=(T // tm,),
        in_specs=[row(D), row(BRANCH_WIDTH), row(BRANCH_WIDTH), row(BRANCH_WIDTH, PK_PU), row(BRANCH_WIDTH),
                  full(pool_w), full(pool_scale), full(w_gate), full(w_branch), full(w_out), full(ln_g), full(ln_b)],
        out_specs=row(D),
        out_shape=jax.ShapeDtypeStruct((T, D), F32),
        scratch_shapes=[pltpu.VMEM((16, len(POOL_WINDOWS) * POOL_GROUP), F32)],
        compiler_params=_params(("arbitrary",)),
        name="merge",
    )(x2, ya, yb, proj16, yd, pool_w, pool_scale, w_gate, w_branch, w_out, ln_g, ln_b)


def _ffn_kernel(x_ref, wg_ref, wu_ref, wd_ref, g_ref, b_ref, o_ref):
    x = x_ref[...]
    x16 = x.astype(BF16)
    h = _silu(_dot(x16, wg_ref[...])) * _dot(x16, wu_ref[...])
    y = DEEPNORM_ALPHA * x + _dot(h.astype(BF16), wd_ref[...])
    o_ref[...] = _layer_norm(y, g_ref[...], b_ref[...])


def _ffn(x2, wg, wu, wd, ln_g, ln_b, tm=512):
    T, D = x2.shape
    row = pl.BlockSpec((tm, D), lambda i: (i, 0))
    full = lambda a: pl.BlockSpec(a.shape, lambda i: (0,) * a.ndim, pipeline_mode=pl.Buffered(1))
    return pl.pallas_call(
        _ffn_kernel,
        grid=(T // tm,),
        in_specs=[row, full(wg), full(wu), full(wd), full(ln_g), full(ln_b)],
        out_specs=row,
        out_shape=jax.ShapeDtypeStruct((T, D), F32),
        compiler_params=_params(("parallel",)),
        name="ffn_dense",
    )(x2, wg, wu, wd, ln_g, ln_b)


def _router_kernel(x_ref, w_ref, meta_ref, gate_ref, cnt_ref, base_ref, *, tm):
    i = pl.program_id(0)

    @pl.when(i == 0)
    def _():
        base_ref[...] = jnp.zeros_like(base_ref)

    xh, xm, _ = _split3(x_ref[...])
    wh, wm, _ = _split3(w_ref[...])
    logits = _dot(xh, wh) + _dot(xh, wm) + _dot(xm, wh)
    lane = lax.broadcasted_iota(jnp.int32, (tm, LANES), 1)
    neg = jnp.float32(-jnp.inf)
    logits = jnp.where(lane < N_EXPERTS, logits, neg)
    v1 = jnp.max(logits, -1, keepdims=True)
    e1 = jnp.min(jnp.where(logits == v1, lane, LANES), -1, keepdims=True)
    rest = jnp.where(lane == e1, neg, logits)
    v2 = jnp.max(rest, -1, keepdims=True)
    e2 = jnp.min(jnp.where(rest == v2, lane, LANES), -1, keepdims=True)
    ex = jnp.exp(v2 - v1)
    w1 = 1.0 / (1.0 + ex)
    w2 = ex / (1.0 + ex)
    onehot = ((lane == e1) | (lane == e2)).astype(F32)
    r = lax.broadcasted_iota(jnp.int32, (tm, tm), 0)
    cc = lax.broadcasted_iota(jnp.int32, (tm, tm), 1)
    before = _dot((cc < r).astype(BF16), onehot.astype(BF16)) + base_ref[0:1, :]
    rank1 = jnp.sum(jnp.where(lane == e1, before, 0.0), -1, keepdims=True)
    rank2 = jnp.sum(jnp.where(lane == e2, before, 0.0), -1, keepdims=True)
    total = base_ref[0:1, :] + jnp.sum(onehot, 0, keepdims=True)
    base_ref[...] = jnp.broadcast_to(total, base_ref.shape)
    cnt_ref[...] = jnp.broadcast_to(total, cnt_ref.shape).astype(jnp.int32)
    meta = jnp.where(lane == 0, e1, jnp.where(lane == 1, e2, jnp.where(
        lane == 2, rank1.astype(jnp.int32), jnp.where(lane == 3, rank2.astype(jnp.int32), 0))))
    meta_ref[...] = meta
    gate_ref[...] = jnp.where(lane == 0, w1, jnp.where(lane == 1, w2, 0.0))


def _router(x2, w_router_pad, tm=512):
    T, D = x2.shape
    return pl.pallas_call(
        functools.partial(_router_kernel, tm=tm),
        grid=(T // tm,),
        in_specs=[pl.BlockSpec((tm, D), lambda i: (i, 0)),
                  pl.BlockSpec(w_router_pad.shape, lambda i: (0, 0))],
        out_specs=[pl.BlockSpec((tm, LANES), lambda i: (i, 0)),
                   pl.BlockSpec((tm, LANES), lambda i: (i, 0)),
                   pl.BlockSpec((8, LANES), lambda i: (0, 0))],
        out_shape=[jax.ShapeDtypeStruct((T, LANES), jnp.int32),
                   jax.ShapeDtypeStruct((T, LANES), F32),
                   jax.ShapeDtypeStruct((8, LANES), jnp.int32)],
        scratch_shapes=[pltpu.VMEM((8, LANES), F32)],
        compiler_params=_params(("arbitrary",)),
        name="moe_router",
    )(x2, w_router_pad)


def _dispatch_kernel(zb_ref, dest_ref, x_ref, xb_ref, zero_ref, sem, zsem, *, td):
    @pl.when(pl.program_id(0) == 0)
    def _():
        zero_ref[...] = jnp.zeros_like(zero_ref)
        nz = 2 * N_EXPERTS
        clear = [pltpu.make_async_copy(zero_ref, xb_ref.at[pl.ds(zb_ref[z] * MOE_ROWS, MOE_ROWS)], zsem)
                 for z in range(nz)]
        for z, cp in enumerate(clear):
            pl.when(zb_ref[nz + z] > 0)(cp.start)
        for z, cp in enumerate(clear):
            pl.when(zb_ref[nz + z] > 0)(cp.wait)

    def start(g, c):
        for u in range(8):
            for slot in range(TOP_K):
                pltpu.make_async_copy(x_ref.at[g, pl.ds(u, 1)],
                                      xb_ref.at[pl.ds(dest_ref[0, 0, (g * 8 + u) * TOP_K + slot], 1)],
                                      sem).start(priority=slot)
        return c

    lax.fori_loop(0, td // 8, start, 0)
    pltpu.make_async_copy(xb_ref.at[pl.ds(0, TOP_K * td)], xb_ref.at[pl.ds(0, TOP_K * td)], sem).wait()


def _dispatch(x2, dest_blocks, clear_blocks, n_rows, td=256):
    T, D = x2.shape
    grid_spec = pltpu.PrefetchScalarGridSpec(
        num_scalar_prefetch=1,
        grid=(T // td,),
        in_specs=[pl.BlockSpec((1, 1, TOP_K * td), lambda i, zb: (i, 0, 0), memory_space=pltpu.SMEM),
                  pl.BlockSpec((td // 8, 8, D), lambda i, zb: (i, 0, 0))],
        out_specs=pl.BlockSpec(memory_space=pl.ANY),
        scratch_shapes=[pltpu.VMEM((MOE_ROWS, D), F32), pltpu.SemaphoreType.DMA(()), pltpu.SemaphoreType.DMA(())],
    )
    return pl.pallas_call(
        functools.partial(_dispatch_kernel, td=td),
        grid_spec=grid_spec,
        out_shape=jax.ShapeDtypeStruct((n_rows, D), F32),
        compiler_params=_params(("arbitrary",)),
        name="moe_dispatch",
    )(clear_blocks, dest_blocks, x2.reshape(T // 8, 8, D))


def _expert_kernel(blk_exp_ref, n_used_ref, x_ref, wg_ref, wu_ref, wd_ref, o_ref, acc_ref):
    i = pl.program_id(0)
    j = pl.program_id(1)

    @pl.when(i < n_used_ref[0])
    def _():
        x16 = x_ref[...].astype(BF16)
        h = _silu(_dot(x16, wg_ref[0])) * _dot(x16, wu_ref[0])
        y = _dot(h.astype(BF16), wd_ref[0])

        @pl.when(j == 0)
        def _():
            acc_ref[...] = y

        @pl.when(j > 0)
        def _():
            acc_ref[...] += y

        @pl.when(j == pl.num_programs(1) - 1)
        def _():
            o_ref[...] = acc_ref[...]

    @pl.when(i >= n_used_ref[0])
    def _():
        o_ref[...] = jnp.zeros_like(o_ref)


def _experts(xb, blk_exp, n_used, wg, wu, wd, tf=1792):
    n_rows, D = xb.shape
    n_blk = n_rows // MOE_ROWS
    nf = D_FF_EXPERT // tf

    def rows(i, j, be, nu):
        return (jnp.minimum(i, nu[0] - 1), 0)

    def chunk(i, j, nu):
        ic = jnp.minimum(i, nu[0] - 1)
        return jnp.where(ic % 2 == 0, jnp.where(i < nu[0], j, nf - 1), jnp.where(i < nu[0], nf - 1 - j, 0))

    grid_spec = pltpu.PrefetchScalarGridSpec(
        num_scalar_prefetch=2,
        grid=(n_blk, nf),
        in_specs=[pl.BlockSpec((MOE_ROWS, D), rows),
                  pl.BlockSpec((1, D, tf), lambda i, j, be, nu: (be[jnp.minimum(i, nu[0] - 1)], 0, chunk(i, j, nu))),
                  pl.BlockSpec((1, D, tf), lambda i, j, be, nu: (be[jnp.minimum(i, nu[0] - 1)], 0, chunk(i, j, nu))),
                  pl.BlockSpec((1, tf, D), lambda i, j, be, nu: (be[jnp.minimum(i, nu[0] - 1)], chunk(i, j, nu), 0))],
        out_specs=pl.BlockSpec((MOE_ROWS, D), lambda i, j, be, nu: (i, 0)),
        scratch_shapes=[pltpu.VMEM((MOE_ROWS, D), F32)],
    )
    return pl.pallas_call(
        _expert_kernel,
        grid_spec=grid_spec,
        out_shape=jax.ShapeDtypeStruct((n_rows, D), F32),
        compiler_params=_params(("arbitrary", "arbitrary")),
        name="moe_experts",
    )(blk_exp, n_used, xb, wg, wu, wd)


def _combine_kernel(dest_ref, dnext_ref, x_ref, gate_ref, yb_ref, g_ref, b_ref, o_ref, buf_ref, sem, *, tc):
    i = pl.program_id(0)
    n = pl.num_programs(0)

    def issue(d_ref, slot):
        def start(g, c):
            for u in range(8):
                for kk in range(TOP_K):
                    pltpu.make_async_copy(yb_ref.at[pl.ds(d_ref[0, 0, (g * 8 + u) * TOP_K + kk], 1)],
                                          buf_ref.at[slot, kk, g, pl.ds(u, 1)], sem.at[slot]).start(priority=kk)
            return c
        lax.fori_loop(0, tc // 8, start, 0)

    @pl.when(i == 0)
    def _():
        issue(dest_ref, 0)

    @pl.when(i + 1 < n)
    def _():
        issue(dnext_ref, (i + 1) % 2)

    slot = i % 2
    pltpu.make_async_copy(buf_ref.at[slot], buf_ref.at[slot], sem.at[slot]).wait()
    gate = gate_ref[...]
    D = x_ref.shape[1]
    ffn = buf_ref[slot, 0].reshape(tc, D) * gate[:, 0:1] + buf_ref[slot, 1].reshape(tc, D) * gate[:, 1:2]
    o_ref[...] = _layer_norm(DEEPNORM_ALPHA * x_ref[...] + ffn, g_ref[...], b_ref[...])


def _combine(x2, gates, yb, dest_blocks, ln_g, ln_b, tc=256):
    T, D = x2.shape
    n = T // tc
    full = lambda a: pl.BlockSpec(a.shape, lambda i: (0,) * a.ndim)
    smem = lambda imap: pl.BlockSpec((1, 1, TOP_K * tc), imap, memory_space=pltpu.SMEM)
    return pl.pallas_call(
        functools.partial(_combine_kernel, tc=tc),
        grid=(n,),
        in_specs=[smem(lambda i: (i, 0, 0)), smem(lambda i: (jnp.minimum(i + 1, n - 1), 0, 0)),
                  pl.BlockSpec((tc, D), lambda i: (i, 0)),
                  pl.BlockSpec((tc, LANES), lambda i: (i, 0)),
                  pl.BlockSpec(memory_space=pl.ANY), full(ln_g), full(ln_b)],
        out_specs=pl.BlockSpec((tc, D), lambda i: (i, 0)),
        out_shape=jax.ShapeDtypeStruct((T, D), F32),
        scratch_shapes=[pltpu.VMEM((2, TOP_K, tc // 8, 8, D), F32), pltpu.SemaphoreType.DMA((2,))],
        compiler_params=_params(("arbitrary",)),
        name="moe_combine",
    )(dest_blocks, dest_blocks, x2, gates, yb, ln_g, ln_b)


def _moe(x2, w_router, wg, wu, wd, ln_g, ln_b, tok=512):
    T, D = x2.shape
    w_router_pad = jnp.pad(w_router.astype(F32), ((0, 0), (0, LANES - N_EXPERTS)))
    meta, gates, counts = _router(x2, w_router_pad)
    counts = counts[0, :N_EXPERTS]
    padded = (counts + MOE_ROWS - 1) // MOE_ROWS * MOE_ROWS
    pends = jnp.cumsum(padded)
    pstarts = pends - padded
    dest = pstarts[meta[:, 0:2]] + meta[:, 2:4]
    dest_blocks = dest.reshape(T // tok, 1, TOP_K * tok)
    n_blk = (T * TOP_K + MOE_ROWS - 1) // MOE_ROWS + N_EXPERTS
    n_rows = n_blk * MOE_ROWS
    blk_start = jnp.arange(n_blk, dtype=jnp.int32) * MOE_ROWS
    blk_exp = jnp.minimum(jnp.sum((blk_start[:, None] >= pends[None, :]).astype(jnp.int32), axis=1), N_EXPERTS - 1)
    n_used = (pends[-1:] // MOE_ROWS).astype(jnp.int32)
    tail = n_used + jnp.arange(N_EXPERTS, dtype=jnp.int32)
    clear_blocks = jnp.concatenate([jnp.maximum(pends // MOE_ROWS - 1, 0), jnp.minimum(tail, n_blk - 1),
                                    counts > 0, tail < n_blk]).astype(jnp.int32)
    xb = _dispatch(x2, dest_blocks, clear_blocks, n_rows, td=tok)
    yb = _experts(xb, blk_exp, n_used, wg, wu, wd)
    return _combine(x2, gates, yb, dest_blocks, ln_g, ln_b, tc=tok)


def _rot_half_cols(w):
    half = w.shape[-1] // 2
    return jnp.concatenate([-w[..., half:], w[..., :half]], axis=-1)


def _pack_w_in(w):
    D = w.shape[0]
    kr = w[:, 384:448]
    z64 = jnp.zeros((D, 64), w.dtype)
    w16 = jnp.concatenate([
        w[:, 448:2496],
        w[:, 2504:3016],
        w[:, 3016:3528],
        w[:, 4040:5064],
        w[:, 0:256],
        w[:, 256:384],
        kr, z64,
        _rot_half_cols(kr), z64,
    ], axis=1)
    w32 = jnp.concatenate([w[:, 3528:4040], w[:, 2496:2504], jnp.zeros((D, LANES - 8), w.dtype)], axis=1)
    return w16.astype(BF16), w32.astype(BF16), w[:, 5064:].astype(BF16)


def _pack_mla(w_uq, w_ukv):
    R = w_uq.shape[0]
    z64 = jnp.zeros((R, 64), w_uq.dtype)
    wq, wqr = [], []
    for h in range(MLA_HEADS):
        o = h * (MLA_NOPE + MLA_ROPE)
        rope_cols = w_uq[:, o + MLA_NOPE:o + MLA_NOPE + MLA_ROPE]
        wq += [w_uq[:, o:o + MLA_NOPE], rope_cols, z64]
        wqr += [_rot_half_cols(rope_cols), z64]
    kv = w_ukv.reshape(w_ukv.shape[0], MLA_HEADS, MLA_NOPE + MLA_V)
    wk = kv[:, :, :MLA_NOPE].reshape(w_ukv.shape[0], MLA_HEADS * MLA_NOPE)
    wv = kv[:, :, MLA_NOPE:].reshape(w_ukv.shape[0], MLA_HEADS * MLA_V)
    return (jnp.concatenate(wq, 1).astype(BF16), jnp.concatenate(wqr, 1).astype(BF16),
            wk.astype(BF16), wv.astype(BF16))


def _lane_row(v, offset=0):
    v = v.astype(F32)
    return jnp.pad(v, (offset, LANES - offset - v.shape[0]))[None, :]


def kernel(x, positions, w_in, mla_q_norm, mla_w_uq, mla_kv_norm, mla_w_ukv, gdn_conv, gdn_a_log, gdn_dt_bias, gdn_norm, pool_w, pool_scale, hgrn_lb_logits, hgrn_norm, w_branch, w_out, ln_mix_g, ln_mix_b, ffn_w_gate, ffn_w_up, ffn_w_down, moe_router, moe_w_gate, moe_w_up, moe_w_down, ln_ffn_g, ln_ffn_b):
    B, S, D = x.shape
    T = B * S
    x2 = x.reshape(T, D)

    half = MLA_ROPE // 2
    inv = ROPE_THETA ** (-jnp.arange(half, dtype=F32) / half)
    inv_row = jnp.concatenate([inv, inv, inv, inv])[None, :]
    off_row = jnp.concatenate([jnp.zeros((MLA_ROPE,), F32), jnp.full((LANES - MLA_ROPE,), -0.5 * math.pi, F32)])[None, :]
    cos_t, sin_t = _rope_tables(positions.reshape(T, 1).astype(jnp.int32), inv_row, off_row)

    p_lb = jax.nn.softmax(hgrn_lb_logits.astype(F32), axis=0)
    lower_bounds = jnp.cumsum(p_lb, axis=0) - p_lb[0]
    row2 = lambda v: v.astype(F32)[None, :]
    tt = jnp.arange(SPAN)
    conv_shift = jnp.concatenate([(tt[None, :] == tt[:, None] - kk) for kk in range(GDN_CONV)], axis=1).astype(BF16)

    for l in range(DEPTH):
        w16, w32, w_gate = _pack_w_in(w_in[l])
        proj16, proj32 = _inproj(x2, w16, w32)
        wq, wqr, wk, wv = _pack_mla(mla_w_uq[l], mla_w_ukv[l])
        y_a = _mla(proj16, cos_t, sin_t, row2(mla_q_norm[l]), row2(mla_kv_norm[l]), wq, wqr, wk, wv, B, S)
        proj16_3 = proj16.reshape(B, S, PK16_WIDTH)
        proj32_3 = proj32.reshape(B, S, PK32_WIDTH)
        y_b, y_d = _recurrent(proj16_3, proj32_3, gdn_conv[l].astype(F32), conv_shift, _lane_row(gdn_a_log[l]),
                              _lane_row(gdn_dt_bias[l]), row2(gdn_norm[l]), row2(lower_bounds[l]), row2(hgrn_norm[l]))
        y_b = y_b.reshape(T, GDN_HEADS * GDN_DV)
        y_d = y_d.reshape(T, HGRN_HEADS * HGRN_DV)
        x2 = _merge(x2, y_a, y_b, proj16, y_d, pool_w[l].astype(BF16), row2(pool_scale[l]), w_gate,
                    w_branch[l].astype(BF16), w_out[l].astype(BF16), row2(ln_mix_g[l]), row2(ln_mix_b[l]), S)
        j = l // 2
        if l % 2 == 0:
            x2 = _ffn(x2, ffn_w_gate[j].astype(BF16), ffn_w_up[j].astype(BF16), ffn_w_down[j].astype(BF16),
                      row2(ln_ffn_g[l]), row2(ln_ffn_b[l]))
        else:
            x2 = _moe(x2, moe_router[j], moe_w_gate[j].astype(BF16), moe_w_up[j].astype(BF16),
                      moe_w_down[j].astype(BF16), row2(ln_ffn_g[l]), row2(ln_ffn_b[l]))
    return x2.reshape(B, S, D)
```

```python
import functools
import math

import jax
import jax.numpy as jnp
from jax import lax
from jax.experimental import pallas as pl
from jax.experimental.pallas import tpu as pltpu

F32 = jnp.float32
BF16 = jnp.bfloat16

D_MODEL = 1024
DEPTH = 2
MLA_HEADS = 4
MLA_Q_RANK = 256
MLA_KV_RANK = 128
MLA_NOPE = 128
MLA_ROPE = 64
MLA_V = 128
ROPE_THETA = 10000.0
MASK_VALUE = -1e30
GDN_HEADS = 4
GDN_DK = 128
GDN_DV = 128
GDN_CONV = 4
GDN_CHUNK = 64
POOL_WINDOWS = (2, 4, 8, 16)
POOL_GROUP = 128
HGRN_HEADS = 4
HGRN_DK = 128
HGRN_DV = 128
N_BRANCH = 4
BRANCH_WIDTH = 512
D_FF = 2816
N_EXPERTS = 8
TOP_K = 2
D_FF_EXPERT = 3584
DEEPNORM_ALPHA = (2 * DEPTH) ** 0.25
LN_EPS = 1e-5
RMS_EPS = 1e-6
LOG2E = math.log2(math.e)

LANES = 128
MLA_HEAD_PAD = 256
SPAN = 256
HALF = SPAN // 2
MOE_ROWS = 512
VMEM_LIMIT = 56 * 1024 * 1024

PK_GQ, PK_GK, PK_GV, PK_GZ, PK_PU, PK_HQ, PK_HI, PK_HG = range(8)
PK_CQ = 16
PK_CKV = 34
PK_KRA = 35
PK_KRB = 36
PK16_WIDTH = 37 * LANES
PK32_HF = 0
PK32_AB = 4
PK32_WIDTH = 5 * LANES


def _params(sem, vmem=VMEM_LIMIT):
    return pltpu.CompilerParams(dimension_semantics=sem, vmem_limit_bytes=vmem)


def _dot(a, b):
    return jnp.dot(a, b, preferred_element_type=F32)


def _dot_nt(a, b):
    return lax.dot_general(a, b, (((1,), (1,)), ((), ())), preferred_element_type=F32)


def _dot_tn(a, b):
    return lax.dot_general(a, b, (((0,), (0,)), ((), ())), preferred_element_type=F32)


def _split3(x):
    hi = x.astype(BF16)
    r1 = x - hi.astype(F32)
    mid = r1.astype(BF16)
    lo = (r1 - mid.astype(F32)).astype(BF16)
    return hi, mid, lo


def _exact_left_dot(m01, x):
    hi, mid, lo = _split3(x)
    return _dot(m01, hi) + _dot(m01, mid) + _dot(m01, lo)


def _sigmoid(x):
    return 1.0 / (1.0 + jnp.exp(-x))


def _silu(x):
    return x * _sigmoid(x)


def _softplus(x):
    return jnp.maximum(x, 0.0) + jnp.log(1.0 + jnp.exp(-jnp.abs(x)))


def _layer_norm(x, g, b):
    mu = jnp.mean(x, -1, keepdims=True)
    xc = x - mu
    var = jnp.mean(xc * xc, -1, keepdims=True)
    return xc * lax.rsqrt(var + LN_EPS) * g + b


def _rms(x, g):
    return x * lax.rsqrt(jnp.mean(x * x, -1, keepdims=True) + RMS_EPS) * g


def _lane_blocks(s):
    return [s[:, c * LANES:(c + 1) * LANES] for c in range(s.shape[1] // LANES)]


def _row_max(s):
    return jnp.max(functools.reduce(jnp.maximum, _lane_blocks(s)), -1, keepdims=True)


def _row_sum(s):
    return jnp.sum(functools.reduce(jnp.add, _lane_blocks(s)), -1, keepdims=True)


def _inproj_kernel(x_ref, w16_ref, w32_ref, o16_ref, o32_ref):
    x16 = x_ref[...].astype(BF16)
    o16_ref[...] = _dot(x16, w16_ref[...]).astype(o16_ref.dtype)
    o32_ref[...] = _dot(x16, w32_ref[...])


def _inproj(x2, w16, w32, tm=512):
    T, D = x2.shape
    full = lambda a: pl.BlockSpec(a.shape, lambda i: (0, 0), pipeline_mode=pl.Buffered(1))
    return pl.pallas_call(
        _inproj_kernel,
        grid=(T // tm,),
        in_specs=[pl.BlockSpec((tm, D), lambda i: (i, 0)), full(w16), full(w32)],
        out_specs=[pl.BlockSpec((tm, w16.shape[1]), lambda i: (i, 0)),
                   pl.BlockSpec((tm, w32.shape[1]), lambda i: (i, 0))],
        out_shape=[jax.ShapeDtypeStruct((T, w16.shape[1]), BF16),
                   jax.ShapeDtypeStruct((T, w32.shape[1]), F32)],
        compiler_params=_params(("parallel",)),
        name="inproj",
    )(x2, w16, w32)


def _rope_table_kernel(pos_ref, inv_ref, off_ref, c_ref, s_ref):
    cs = jnp.cos(pos_ref[...].astype(F32) * inv_ref[...] + off_ref[...])
    c_ref[...] = cs
    s_ref[...] = pltpu.roll(cs, LANES // 2, 1)


def _rope_tables(pos_lanes, inv_row, off_row, tm=1024):
    T = pos_lanes.shape[0]
    return pl.pallas_call(
        _rope_table_kernel,
        grid=(T // tm,),
        in_specs=[pl.BlockSpec((tm, LANES), lambda i: (i, 0)),
                  pl.BlockSpec((1, LANES), lambda i: (0, 0)),
                  pl.BlockSpec((1, LANES), lambda i: (0, 0))],
        out_specs=[pl.BlockSpec((tm, LANES), lambda i: (i, 0))] * 2,
        out_shape=[jax.ShapeDtypeStruct((T, LANES), F32)] * 2,
        compiler_params=_params(("parallel",)),
        name="rope_tables",
    )(pos_lanes, inv_row, off_row)


def _mla_prep_rows(rs, cq_ref, ckv_ref, kra_ref, krb_ref, c_ref, s_ref, gq_ref, gkv_ref,
                   wq_ref, wqr_ref, wk_ref, wv_ref, q_s, k_s, v_s):
    scale = (MLA_NOPE + MLA_ROPE) ** -0.5 * LOG2E
    cos = c_ref[rs, :]
    sin = s_ref[rs, :]
    nq = _rms(cq_ref[rs, :].astype(F32), gq_ref[...]).astype(BF16)
    q_pre = _dot(nq, wq_ref[...])
    q_rot = _dot(nq, wqr_ref[...])
    nkv = _rms(ckv_ref[rs, :].astype(F32), gkv_ref[...]).astype(BF16)
    k_nope = _dot(nkv, wk_ref[...])
    v_s[rs, :] = _dot(nkv, wv_ref[...]).astype(v_s.dtype)
    k_rope = (kra_ref[rs, :].astype(F32) * cos + krb_ref[rs, :].astype(F32) * sin).astype(k_s.dtype)
    for h in range(MLA_HEADS):
        o = h * MLA_HEAD_PAD
        q_s[rs, o:o + LANES] = (q_pre[:, o:o + LANES] * scale).astype(q_s.dtype)
        q_s[rs, o + LANES:o + 2 * LANES] = (
            (q_pre[:, o + LANES:o + 2 * LANES] * cos + q_rot[:, h * LANES:(h + 1) * LANES] * sin) * scale
        ).astype(q_s.dtype)
        k_s[rs, o:o + LANES] = k_nope[:, h * LANES:(h + 1) * LANES].astype(k_s.dtype)
        k_s[rs, o + LANES:o + 2 * LANES] = k_rope


def _attn_tile(q_s, k_s, v_s, o_ref, t, tq, causal):
    r0 = t * tq
    for h in range(MLA_HEADS):
        hs = slice(h * MLA_HEAD_PAD, (h + 1) * MLA_HEAD_PAD)
        vs = slice(h * MLA_V, (h + 1) * MLA_V)
        qh = q_s[r0:r0 + tq, hs]
        s_d = jnp.where(causal, _dot_nt(qh, k_s[r0:r0 + tq, hs]), MASK_VALUE)
        m = _row_max(s_d)
        if t > 0:
            s_o = _dot_nt(qh, k_s[0:r0, hs])
            m = jnp.maximum(m, _row_max(s_o))
        p_d = jnp.exp2(s_d - m)
        l = _row_sum(p_d)
        acc = _dot(p_d.astype(BF16), v_s[r0:r0 + tq, vs])
        if t > 0:
            p_o = jnp.exp2(s_o - m)
            l = l + _row_sum(p_o)
            acc = acc + _dot(p_o.astype(BF16), v_s[0:r0, vs])
        o_ref[:, vs] = (acc / l).astype(o_ref.dtype)


def _mla_kernel(*refs, seq, tq, tprep):
    prep_refs, o_ref, (q_s, k_s, v_s) = refs[:12], refs[12], refs[13:]
    i = pl.program_id(1)

    @pl.when(i == 0)
    def _():
        for c in range(seq // tprep):
            _mla_prep_rows(slice(c * tprep, (c + 1) * tprep), *prep_refs, q_s, k_s, v_s)

    row = lax.broadcasted_iota(jnp.int32, (tq, tq), 0)
    col = lax.broadcasted_iota(jnp.int32, (tq, tq), 1)
    causal = row >= col
    for t in range(seq // tq):
        pl.when(i == t)(functools.partial(_attn_tile, q_s, k_s, v_s, o_ref, t, tq, causal))


def _mla(proj, cos_t, sin_t, g_q, g_kv, wq, wqr, wk, wv, B, S, tq=256, tprep=512):
    T = proj.shape[0]
    nq = S // tq
    HW = MLA_HEADS * MLA_HEAD_PAD
    HV = MLA_HEADS * MLA_V
    seq = lambda w, j: pl.BlockSpec((S, w), lambda b, i, j=j: (b, j))
    full = lambda a: pl.BlockSpec(a.shape, lambda b, i: (0,) * a.ndim)
    return pl.pallas_call(
        functools.partial(_mla_kernel, seq=S, tq=tq, tprep=min(tprep, S)),
        grid=(B, nq),
        in_specs=[seq(256, PK_CQ), seq(128, PK_CKV), seq(128, PK_KRA), seq(128, PK_KRB),
                  seq(128, 0), seq(128, 0), full(g_q), full(g_kv), full(wq), full(wqr), full(wk), full(wv)],
        out_specs=pl.BlockSpec((tq, HV), lambda b, i: (b * nq + i, 0)),
        out_shape=jax.ShapeDtypeStruct((T, HV), BF16),
        scratch_shapes=[pltpu.VMEM((S, HW), BF16), pltpu.VMEM((S, HW), BF16), pltpu.VMEM((S, HV), BF16)],
        compiler_params=_params(("parallel", "arbitrary")),
        name="mla",
    )(proj, proj, proj, proj, cos_t, sin_t, g_q, g_kv, wq, wqr, wk, wv)


def _gdn_body(q_ref, k_ref, v_ref, z_ref, ab_ref, cw_ref, sh_ref, alog_ref, dtb_ref, ng_ref, o_ref,
              carry_ref, state_ref, *, nb):
    si = pl.program_id(1)
    HD = GDN_HEADS * GDN_DK
    NC = SPAN // GDN_CHUNK

    @pl.when(si == 0)
    def _():
        carry_ref[...] = jnp.zeros_like(carry_ref)
        state_ref[...] = jnp.zeros_like(state_ref)

    yield

    def conv_silu(x_ref, b, idx):
        x16 = x_ref[b]
        w16 = cw_ref[:, idx * HD:(idx + 1) * HD].astype(BF16)
        taps = [w16[GDN_CONV - 1 - kk:GDN_CONV - kk] for kk in range(GDN_CONV)]
        y = _dot(sh_ref[...], jnp.concatenate([x16 * tap for tap in taps], axis=0))
        r8 = lax.broadcasted_iota(jnp.int32, (8, HD), 0)
        head = y[0:8]
        for kk in range(1, GDN_CONV):
            term = (pltpu.roll(carry_ref[b, idx], kk, 0).astype(BF16) * taps[kk]).astype(F32)
            head = head + jnp.where(r8 < kk, term, 0.0)
        carry_ref[b, idx] = x16[SPAN - 8:SPAN].astype(F32)
        return _silu(jnp.concatenate([head, y[8:]], axis=0))

    row = lax.broadcasted_iota(jnp.int32, (SPAN, SPAN), 0)
    col = lax.broadcasted_iota(jnp.int32, (SPAN, SPAN), 1)
    same_chunk = (row // GDN_CHUNK) == (col // GDN_CHUNK)
    tril = same_chunk & (col <= row)
    stril = same_chunk & (col < row)
    eye = (row == col).astype(F32)
    tril16 = tril.astype(BF16)

    units = []
    for b in range(nb):
        q = conv_silu(q_ref, b, 0)
        k = conv_silu(k_ref, b, 1)
        v = conv_silu(v_ref, b, 2)
        ab = ab_ref[b]
        g_full = -jnp.exp(alog_ref[...]) * _softplus(ab + dtb_ref[...])
        beta_full = _sigmoid(ab)
        gc_full = _exact_left_dot(tril16, g_full)
        gc_t = gc_full.T
        for h in range(GDN_HEADS):
            sl = slice(h * GDN_DK, (h + 1) * GDN_DK)
            qh = q[:, sl]
            kh = k[:, sl]
            qh = qh * lax.rsqrt(jnp.sum(qh * qh, -1, keepdims=True) + RMS_EPS) * (GDN_DK ** -0.5)
            kh = kh * lax.rsqrt(jnp.sum(kh * kh, -1, keepdims=True) + RMS_EPS)
            gc = gc_full[:, h:h + 1]
            beta = beta_full[:, GDN_HEADS + h:GDN_HEADS + h + 1]
            decay = jnp.where(tril, jnp.exp(gc - gc_t[h:h + 1, :]), 0.0)
            k16 = kh.astype(BF16)
            a = jnp.where(stril, beta * _dot_nt(k16, k16) * decay, 0.0)
            qk16 = (_dot_nt(qh.astype(BF16), k16) * decay).astype(BF16)
            egc = jnp.exp(gc)
            units.append(dict(
                b=b, h=h, sl=sl, kh=kh, gc=gc, qk16=qk16, a=a,
                rhs=jnp.concatenate([v[:, sl] * beta, kh * (beta * egc)], axis=1).astype(BF16),
                qg=(qh * egc).astype(BF16)))
        yield

    xs = [eye - u["a"] for u in units]
    ps = [u["a"].astype(BF16) for u in units]
    for _ in range(5):
        ps = [_dot(p, p).astype(BF16) for p in ps]
        xs = [x + _dot(x.astype(BF16), p) for x, p in zip(xs, ps)]
        yield
    uws = [_dot(x.astype(BF16), u["rhs"]) for x, u in zip(xs, units)]
    yield

    states = [state_ref[u["b"], u["h"]] for u in units]
    o_state = [[] for _ in units]
    v_new = [[] for _ in units]
    for c in range(NC):
        cs = slice(c * GDN_CHUNK, (c + 1) * GDN_CHUNK)
        s16 = [s.astype(BF16) for s in states]
        vns = [uw[cs, :GDN_DV] - _dot(uw[cs, GDN_DV:].astype(BF16), s) for uw, s in zip(uws, s16)]
        for i, u in enumerate(units):
            o_state[i].append(_dot(u["qg"][cs], s16[i]))
            v_new[i].append(vns[i])
            g_last = u["gc"][(c + 1) * GDN_CHUNK - 1:(c + 1) * GDN_CHUNK, :]
            kd = (u["kh"][cs] * jnp.exp(g_last - u["gc"][cs])).astype(BF16)
            states[i] = states[i] * jnp.exp(g_last) + _dot_tn(kd, vns[i].astype(BF16))
        yield
    for i, u in enumerate(units):
        state_ref[u["b"], u["h"]] = states[i]
        o = jnp.concatenate(o_state[i], axis=0) + _dot(u["qk16"], jnp.concatenate(v_new[i], axis=0).astype(BF16))
        o = _rms(o, ng_ref[...]) * _silu(z_ref[u["b"], :, u["sl"]].astype(F32))
        o_ref[u["b"], :, u["sl"]] = o.astype(o_ref.dtype)
        if i % GDN_HEADS == GDN_HEADS - 1:
            yield


def _gdn_kernel(*refs, nb):
    for _ in _gdn_body(*refs, nb=nb):
        pass


def _gdn(proj16, proj32, conv_w, shift, alog_row, dtb_row, norm_g, nb=2):
    B, S, _ = proj16.shape
    HD = GDN_HEADS * GDN_DK
    blk = lambda w, j: pl.BlockSpec((nb, SPAN, w), lambda b, s, j=j: (b, s, j))
    full = lambda a: pl.BlockSpec(a.shape, lambda b, s: (0,) * a.ndim)
    return pl.pallas_call(
        functools.partial(_gdn_kernel, nb=nb),
        grid=(B // nb, S // SPAN),
        in_specs=[blk(HD, PK_GQ), blk(HD, PK_GK), blk(HD, PK_GV), blk(HD, PK_GZ), blk(LANES, PK32_AB),
                  full(conv_w), full(shift), full(alog_row), full(dtb_row), full(norm_g)],
        out_specs=blk(HD, 0),
        out_shape=jax.ShapeDtypeStruct((B, S, HD), BF16),
        scratch_shapes=[pltpu.VMEM((nb, 3, 8, HD), F32), pltpu.VMEM((nb, GDN_HEADS, GDN_DK, GDN_DV), F32)],
        compiler_params=_params(("parallel", "arbitrary")),
        name="gdn",
    )(proj16, proj16, proj16, proj16, proj32, conv_w, shift, alog_row, dtb_row, norm_g)


def _pool_tile(u_ref, w_ref, sc_ref, carry_ref, si, tp):
    G = POOL_GROUP

    @pl.when(si == 0)
    def _():
        carry_ref[...] = jnp.zeros_like(carry_ref)

    u = u_ref[...].astype(F32)
    ext = jnp.concatenate([carry_ref[...], u], axis=0)
    carry_ref[...] = u[tp - 16:tp]
    t = si * tp + lax.broadcasted_iota(jnp.int32, (tp, 1), 0)
    win_sum = ext
    outs = []
    shift = 1
    for gi, win in enumerate(POOL_WINDOWS):
        while shift < win:
            win_sum = win_sum + pltpu.roll(win_sum, shift, 0)
            shift *= 2
        cnt = jnp.minimum(t + 1, win).astype(F32)
        ug = u[:, gi * G:(gi + 1) * G]
        pooled = win_sum[16:, gi * G:(gi + 1) * G] / cnt - ug
        outs.append(_dot(pooled.astype(BF16), w_ref[gi]))
    return (jnp.concatenate(outs, axis=1) * sc_ref[...]).astype(BF16)


def _mid_broadcast(c, s):
    n = c.shape[0]
    if 2 * s >= 8:
        c3 = c.reshape(n // (2 * s), 2 * s, c.shape[1])
        mid = jnp.broadcast_to(c3[:, s - 1:s, :], c3.shape)
        return mid.reshape(c.shape)
    pos = lax.broadcasted_iota(jnp.int32, c.shape, 0) % (2 * s)
    out = c
    for p in range(2 * s):
        d = p - (s - 1)
        if d != 0:
            out = jnp.where(pos == p, pltpu.roll(c, d % n, 0), out)
    return out


def _hgrn_body(q_ref, f_ref, i_ref, g_ref, lb_ref, ng_ref, o_ref, state_ref, *, nb):
    si = pl.program_id(1)

    @pl.when(si == 0)
    def _():
        state_ref[...] = jnp.zeros_like(state_ref)

    yield

    row = lax.broadcasted_iota(jnp.int32, (SPAN, SPAN), 0)
    col = lax.broadcasted_iota(jnp.int32, (SPAN, SPAN), 1)
    lower16 = (col <= row).astype(BF16)
    heads = [slice(h * HGRN_DK, (h + 1) * HGRN_DK) for h in range(HGRN_HEADS)]
    halves = [slice(0, HALF), slice(HALF, SPAN)]
    ri = lax.broadcasted_iota(jnp.int32, (HALF, HALF), 0)
    ci = lax.broadcasted_iota(jnp.int32, (HALF, HALF), 1)
    rowl = lax.broadcasted_iota(jnp.int32, (SPAN, LANES), 0)
    lb = lb_ref[...]

    seqs = []
    for b in range(nb):
        q = _silu(q_ref[b].astype(F32))
        fp = f_ref[b]
        log_f = jnp.log(lb + (1.0 - lb) * _sigmoid(fp))
        k = (1.0 - lb) * _sigmoid(-fp)
        c = _exact_left_dot(lower16, log_f)
        q16 = q.astype(BF16)
        k16 = k.astype(BF16)
        att = [[jnp.where(ri == ci, _dot_nt(q16[r, sl], k16[r, sl]), 0.0) for sl in heads] for r in halves]
        seqs.append(dict(q=q, k=k, c=c, q16=q16, k16=k16, att=att, cross=None))
        yield

    s = 1
    while s < SPAN:
        sgn = jnp.where(((rowl // s) % 2) == 1, LOG2E, -LOG2E)
        if s < HALF:
            bi = ri // s
            bj = ci // s
            region = ((bi - bj) * 2 + (bj & 1)) == 2
        for sq in seqs:
            diff = sq["c"] - _mid_broadcast(sq["c"], s)
            w16 = jnp.concatenate([jnp.exp2(diff[:, sl] * sgn) for sl in heads], axis=1).astype(BF16)
            ql = sq["q16"] * w16
            kl = sq["k16"] * w16
            if s < HALF:
                for d, r in enumerate(halves):
                    for h, sl in enumerate(heads):
                        sq["att"][d][h] = jnp.where(region, _dot_nt(ql[r, sl], kl[r, sl]), sq["att"][d][h])
            else:
                sq["cross"] = [_dot_nt(ql[halves[1], sl], kl[halves[0], sl]) for sl in heads]
        s *= 2
        yield

    for b, sq in enumerate(seqs):
        c = sq["c"]
        c_last = c[SPAN - 1:SPAN, :]
        v16 = i_ref[b]
        qe = (sq["q"] * jnp.exp(c)).astype(BF16)
        ke = (sq["k"] * jnp.exp(c_last - c)).astype(BF16)
        e_last = jnp.exp(c_last)
        for h, sl in enumerate(heads):
            st = state_ref[b, h]
            v_lo = v16[halves[0], sl]
            v_hi = v16[halves[1], sl]
            o_lo = _dot(sq["att"][0][h].astype(BF16), v_lo)
            o_hi = _dot(sq["cross"][h].astype(BF16), v_lo) + _dot(sq["att"][1][h].astype(BF16), v_hi)
            o = jnp.concatenate([o_lo, o_hi], axis=0) + _dot_nt(qe[:, sl], st.astype(BF16))
            state_ref[b, h] = st * e_last[:, sl] + _dot_tn(v16[:, sl], ke[:, sl])
            o = _rms(o, ng_ref[...]) * _sigmoid(g_ref[b, :, sl].astype(F32))
            o_ref[b, :, sl] = o.astype(o_ref.dtype)
        yield


def _hgrn_kernel(*refs, nb):
    for _ in _hgrn_body(*refs, nb=nb):
        pass


def _recurrent_kernel(*refs, nb):
    g_in, h_in, (g_out, h_out), g_scr, h_scr = refs[:10], refs[10:16], refs[16:18], refs[18:20], refs[20:]
    bodies = [_gdn_body(*g_in, g_out, *g_scr, nb=nb), _hgrn_body(*h_in, h_out, *h_scr, nb=nb)]
    while bodies:
        for body in list(bodies):
            try:
                next(body)
            except StopIteration:
                bodies.remove(body)


def _recurrent(proj16, proj32, conv_w, shift, alog_row, dtb_row, gdn_norm_g, lb_row, hgrn_norm_g, nb=2):
    B, S, _ = proj16.shape
    HD = GDN_HEADS * GDN_DK
    blk = lambda w, j: pl.BlockSpec((nb, SPAN, w), lambda b, s, j=j: (b, s, j))
    full = lambda a: pl.BlockSpec(a.shape, lambda b, s: (0,) * a.ndim)
    out = jax.ShapeDtypeStruct((B, S, HD), BF16)
    return pl.pallas_call(
        functools.partial(_recurrent_kernel, nb=nb),
        grid=(B // nb, S // SPAN),
        in_specs=[blk(HD, PK_GQ), blk(HD, PK_GK), blk(HD, PK_GV), blk(HD, PK_GZ), blk(LANES, PK32_AB),
                  full(conv_w), full(shift), full(alog_row), full(dtb_row), full(gdn_norm_g),
                  blk(HD, PK_HQ), blk(HD, PK32_HF), blk(HD, PK_HI), blk(HD, PK_HG), full(lb_row), full(hgrn_norm_g)],
        out_specs=[blk(HD, 0), blk(HD, 0)],
        out_shape=[out, out],
        scratch_shapes=[pltpu.VMEM((nb, 3, 8, HD), F32), pltpu.VMEM((nb, GDN_HEADS, GDN_DK, GDN_DV), F32),
                        pltpu.VMEM((nb, HGRN_HEADS, HGRN_DV, HGRN_DK), F32)],
        compiler_params=_params(("parallel", "arbitrary")),
        name="gdn_hgrn2",
    )(proj16, proj16, proj16, proj16, proj32, conv_w, shift, alog_row, dtb_row, gdn_norm_g,
      proj16, proj32, proj16, proj16, lb_row, hgrn_norm_g)


def _hgrn(proj16, proj32, lb_row, norm_g, nb=2):
    B, S, _ = proj16.shape
    HD = HGRN_HEADS * HGRN_DK
    blk = lambda j: pl.BlockSpec((nb, SPAN, HD), lambda b, s, j=j: (b, s, j))
    full = lambda a: pl.BlockSpec(a.shape, lambda b, s: (0,) * a.ndim)
    return pl.pallas_call(
        functools.partial(_hgrn_kernel, nb=nb),
        grid=(B // nb, S // SPAN),
        in_specs=[blk(PK_HQ), blk(PK32_HF), blk(PK_HI), blk(PK_HG), full(lb_row), full(norm_g)],
        out_specs=blk(0),
        out_shape=jax.ShapeDtypeStruct((B, S, HD), BF16),
        scratch_shapes=[pltpu.VMEM((nb, HGRN_HEADS, HGRN_DV, HGRN_DK), F32)],
        compiler_params=_params(("parallel", "arbitrary")),
        name="hgrn2",
    )(proj16, proj32, proj16, proj16, lb_row, norm_g)


def _merge_kernel(x_ref, ya_ref, yb_ref, u_ref, yd_ref, pw_ref, psc_ref, wg_ref, wb_ref, wo_ref, g_ref, b_ref,
                  o_ref, carry_ref, *, tiles_per_seq, tm):
    yc = _pool_tile(u_ref, pw_ref, psc_ref, carry_ref, pl.program_id(0) % tiles_per_seq, tm)
    x = x_ref[...]
    x16 = x.astype(BF16)
    merged = None
    for m, y in enumerate((ya_ref[...], yb_ref[...], yc, yd_ref[...])):
        gate = _sigmoid(_dot(x16, wg_ref[:, m * D_MODEL:(m + 1) * D_MODEL]))
        term = gate * _dot(y, wb_ref[m])
        merged = term if merged is None else merged + term
    h = DEEPNORM_ALPHA * x + _dot(merged.astype(BF16), wo_ref[...])
    o_ref[...] = _layer_norm(h, g_ref[...], b_ref[...])


def _merge(x2, ya, yb, proj16, yd, pool_w, pool_scale, w_gate, w_branch, w_out, ln_g, ln_b, S, tm=512):
    T, D = x2.shape
    row = lambda w, j=0: pl.BlockSpec((tm, w), lambda i, j=j: (i, j))
    full = lambda a: pl.BlockSpec(a.shape, lambda i: (0,) * a.ndim, pipeline_mode=pl.Buffered(1))
    return pl.pallas_call(
        functools.partial(_merge_kernel, tiles_per_seq=S // tm, tm=tm),
        grid=(T // tm,),
        in_specs=[row(D), row(BRANCH_WIDTH), row(BRANCH_WIDTH), row(BRANCH_WIDTH, PK_PU), row(BRANCH_WIDTH),
                  full(pool_w), full(pool_scale), full(w_gate), full(w_branch), full(w_out), full(ln_g), full(ln_b)],
        out_specs=row(D),
        out_shape=jax.ShapeDtypeStruct((T, D), F32),
        scratch_shapes=[pltpu.VMEM((16, len(POOL_WINDOWS) * POOL_GROUP), F32)],
        compiler_params=_params(("arbitrary",)),
        name="merge",
    )(x2, ya, yb, proj16, yd, pool_w, pool_scale, w_gate, w_branch, w_out, ln_g, ln_b)


def _ffn_kernel(x_ref, wg_ref, wu_ref, wd_ref, g_ref, b_ref, o_ref):
    x = x_ref[...]
    x16 = x.astype(BF16)
    h = _silu(_dot(x16, wg_ref[...])) * _dot(x16, wu_ref[...])
    y = DEEPNORM_ALPHA * x + _dot(h.astype(BF16), wd_ref[...])
    o_ref[...] = _layer_norm(y, g_ref[...], b_ref[...])


def _ffn(x2, wg, wu, wd, ln_g, ln_b, tm=512):
    T, D = x2.shape
    row = pl.BlockSpec((tm, D), lambda i: (i, 0))
    full = lambda a: pl.BlockSpec(a.shape, lambda i: (0,) * a.ndim, pipeline_mode=pl.Buffered(1))
    return pl.pallas_call(
        _ffn_kernel,
        grid=(T // tm,),
        in_specs=[row, full(wg), full(wu), full(wd), full(ln_g), full(ln_b)],
        out_specs=row,
        out_shape=jax.ShapeDtypeStruct((T, D), F32),
        compiler_params=_params(("parallel",)),
        name="ffn_dense",
    )(x2, wg, wu, wd, ln_g, ln_b)


def _router_kernel(x_ref, w_ref, meta_ref, gate_ref, cnt_ref, base_ref, *, tm):
    i = pl.program_id(0)

    @pl.when(i == 0)
    def _():
        base_ref[...] = jnp.zeros_like(base_ref)

    xh, xm, _ = _split3(x_ref[...])
    wh, wm, _ = _split3(w_ref[...])
    logits = _dot(xh, wh) + _dot(xh, wm) + _dot(xm, wh)
    lane = lax.broadcasted_iota(jnp.int32, (tm, LANES), 1)
    neg = jnp.float32(-jnp.inf)
    logits = jnp.where(lane < N_EXPERTS, logits, neg)
    v1 = jnp.max(logits, -1, keepdims=True)
    e1 = jnp.min(jnp.where(logits == v1, lane, LANES), -1, keepdims=True)
    rest = jnp.where(lane == e1, neg, logits)
    v2 = jnp.max(rest, -1, keepdims=True)
    e2 = jnp.min(jnp.where(rest == v2, lane, LANES), -1, keepdims=True)
    ex = jnp.exp(v2 - v1)
    w1 = 1.0 / (1.0 + ex)
    w2 = ex / (1.0 + ex)
    onehot = ((lane == e1) | (lane == e2)).astype(F32)
    r = lax.broadcasted_iota(jnp.int32, (tm, tm), 0)
    cc = lax.broadcasted_iota(jnp.int32, (tm, tm), 1)
    before = _dot((cc < r).astype(BF16), onehot.astype(BF16)) + base_ref[0:1, :]
    rank1 = jnp.sum(jnp.where(lane == e1, before, 0.0), -1, keepdims=True)
    rank2 = jnp.sum(jnp.where(lane == e2, before, 0.0), -1, keepdims=True)
    total = base_ref[0:1, :] + jnp.sum(onehot, 0, keepdims=True)
    base_ref[...] = jnp.broadcast_to(total, base_ref.shape)
    cnt_ref[...] = jnp.broadcast_to(total, cnt_ref.shape).astype(jnp.int32)
    meta = jnp.where(lane == 0, e1, jnp.where(lane == 1, e2, jnp.where(
        lane == 2, rank1.astype(jnp.int32), jnp.where(lane == 3, rank2.astype(jnp.int32), 0))))
    meta_ref[...] = meta[:, 0:meta_ref.shape[1]]
    gate_ref[...] = jnp.where(lane == 0, w1, jnp.where(lane == 1, w2, 0.0))


def _router(x2, w_router_pad, tm=512):
    T, D = x2.shape
    return pl.pallas_call(
        functools.partial(_router_kernel, tm=tm),
        grid=(T // tm,),
        in_specs=[pl.BlockSpec((tm, D), lambda i: (i, 0)),
                  pl.BlockSpec(w_router_pad.shape, lambda i: (0, 0))],
        out_specs=[pl.BlockSpec((tm, 8), lambda i: (i, 0)),
                   pl.BlockSpec((tm, LANES), lambda i: (i, 0)),
                   pl.BlockSpec((8, LANES), lambda i: (0, 0))],
        out_shape=[jax.ShapeDtypeStruct((T, 8), jnp.int32),
                   jax.ShapeDtypeStruct((T, LANES), F32),
                   jax.ShapeDtypeStruct((8, LANES), jnp.int32)],
        scratch_shapes=[pltpu.VMEM((8, LANES), F32)],
        compiler_params=_params(("arbitrary",)),
        name="moe_router",
    )(x2, w_router_pad)


def _dispatch_kernel(zb_ref, dest_ref, x_ref, xb_ref, zero_ref, sem, zsem, *, td):
    @pl.when(pl.program_id(0) == 0)
    def _():
        zero_ref[...] = jnp.zeros_like(zero_ref)
        nz = 2 * N_EXPERTS
        clear = [pltpu.make_async_copy(zero_ref, xb_ref.at[pl.ds(zb_ref[z] * MOE_ROWS, MOE_ROWS)], zsem)
                 for z in range(nz)]
        for z, cp in enumerate(clear):
            pl.when(zb_ref[nz + z] > 0)(cp.start)
        for z, cp in enumerate(clear):
            pl.when(zb_ref[nz + z] > 0)(cp.wait)

    def start(g, c):
        for u in range(8):
            for slot in range(TOP_K):
                pltpu.make_async_copy(x_ref.at[g, pl.ds(u, 1)],
                                      xb_ref.at[pl.ds(dest_ref[0, 0, (g * 8 + u) * TOP_K + slot], 1)],
                                      sem).start(priority=slot)
        return c

    lax.fori_loop(0, td // 8, start, 0)
    pltpu.make_async_copy(xb_ref.at[pl.ds(0, TOP_K * td)], xb_ref.at[pl.ds(0, TOP_K * td)], sem).wait()


def _dispatch(x2, dest_blocks, clear_blocks, n_rows, td=256):
    T, D = x2.shape
    grid_spec = pltpu.PrefetchScalarGridSpec(
        num_scalar_prefetch=1,
        grid=(T // td,),
        in_specs=[pl.BlockSpec((1, 1, TOP_K * td), lambda i, zb: (i, 0, 0), memory_space=pltpu.SMEM),
                  pl.BlockSpec((td // 8, 8, D), lambda i, zb: (i, 0, 0))],
        out_specs=pl.BlockSpec(memory_space=pl.ANY),
        scratch_shapes=[pltpu.VMEM((MOE_ROWS, D), F32), pltpu.SemaphoreType.DMA(()), pltpu.SemaphoreType.DMA(())],
    )
    return pl.pallas_call(
        functools.partial(_dispatch_kernel, td=td),
        grid_spec=grid_spec,
        out_shape=jax.ShapeDtypeStruct((n_rows, D), F32),
        compiler_params=_params(("arbitrary",)),
        name="moe_dispatch",
    )(clear_blocks, dest_blocks, x2.reshape(T // 8, 8, D))


def _expert_kernel(blk_exp_ref, n_used_ref, x_ref, wg_ref, wu_ref, wd_ref, o_ref, acc_ref):
    i = pl.program_id(0)
    j = pl.program_id(1)

    @pl.when(i < n_used_ref[0])
    def _():
        x16 = x_ref[...].astype(BF16)
        h = _silu(_dot(x16, wg_ref[0])) * _dot(x16, wu_ref[0])
        y = _dot(h.astype(BF16), wd_ref[0])

        @pl.when(j == 0)
        def _():
            acc_ref[...] = y

        @pl.when(j > 0)
        def _():
            acc_ref[...] += y

        @pl.when(j == pl.num_programs(1) - 1)
        def _():
            o_ref[...] = acc_ref[...]

    @pl.when(i >= n_used_ref[0])
    def _():
        o_ref[...] = jnp.zeros_like(o_ref)


def _experts(xb, blk_exp, n_used, wg, wu, wd, tf=1792):
    n_rows, D = xb.shape
    n_blk = n_rows // MOE_ROWS
    nf = D_FF_EXPERT // tf

    def rows(i, j, be, nu):
        return (jnp.minimum(i, nu[0] - 1), 0)

    def chunk(i, j, nu):
        ic = jnp.minimum(i, nu[0] - 1)
        return jnp.where(ic % 2 == 0, jnp.where(i < nu[0], j, nf - 1), jnp.where(i < nu[0], nf - 1 - j, 0))

    grid_spec = pltpu.PrefetchScalarGridSpec(
        num_scalar_prefetch=2,
        grid=(n_blk, nf),
        in_specs=[pl.BlockSpec((MOE_ROWS, D), rows),
                  pl.BlockSpec((1, D, tf), lambda i, j, be, nu: (be[jnp.minimum(i, nu[0] - 1)], 0, chunk(i, j, nu))),
                  pl.BlockSpec((1, D, tf), lambda i, j, be, nu: (be[jnp.minimum(i, nu[0] - 1)], 0, chunk(i, j, nu))),
                  pl.BlockSpec((1, tf, D), lambda i, j, be, nu: (be[jnp.minimum(i, nu[0] - 1)], chunk(i, j, nu), 0))],
        out_specs=pl.BlockSpec((MOE_ROWS, D), lambda i, j, be, nu: (i, 0)),
        scratch_shapes=[pltpu.VMEM((MOE_ROWS, D), F32)],
    )
    return pl.pallas_call(
        _expert_kernel,
        grid_spec=grid_spec,
        out_shape=jax.ShapeDtypeStruct((n_rows, D), F32),
        compiler_params=_params(("arbitrary", "arbitrary")),
        name="moe_experts",
    )(blk_exp, n_used, xb, wg, wu, wd)


def _combine_kernel(dest_ref, dnext_ref, x_ref, gate_ref, yb_ref, g_ref, b_ref, o_ref, buf_ref, sem, *, tc):
    i = pl.program_id(0)
    n = pl.num_programs(0)

    def issue(d_ref, slot):
        def start(g, c):
            for u in range(8):
                for kk in range(TOP_K):
                    pltpu.make_async_copy(yb_ref.at[pl.ds(d_ref[0, 0, (g * 8 + u) * TOP_K + kk], 1)],
                                          buf_ref.at[slot, kk, g, pl.ds(u, 1)], sem.at[slot]).start(priority=kk)
            return c
        lax.fori_loop(0, tc // 8, start, 0)

    @pl.when(i == 0)
    def _():
        issue(dest_ref, 0)

    @pl.when(i + 1 < n)
    def _():
        issue(dnext_ref, (i + 1) % 2)

    slot = i % 2
    pltpu.make_async_copy(buf_ref.at[slot], buf_ref.at[slot], sem.at[slot]).wait()
    gate = gate_ref[...]
    D = x_ref.shape[1]
    ffn = buf_ref[slot, 0].reshape(tc, D) * gate[:, 0:1] + buf_ref[slot, 1].reshape(tc, D) * gate[:, 1:2]
    o_ref[...] = _layer_norm(DEEPNORM_ALPHA * x_ref[...] + ffn, g_ref[...], b_ref[...])


def _combine(x2, gates, yb, dest_blocks, ln_g, ln_b, tc=256):
    T, D = x2.shape
    n = T // tc
    full = lambda a: pl.BlockSpec(a.shape, lambda i: (0,) * a.ndim)
    smem = lambda imap: pl.BlockSpec((1, 1, TOP_K * tc), imap, memory_space=pltpu.SMEM)
    return pl.pallas_call(
        functools.partial(_combine_kernel, tc=tc),
        grid=(n,),
        in_specs=[smem(lambda i: (i, 0, 0)), smem(lambda i: (jnp.minimum(i + 1, n - 1), 0, 0)),
                  pl.BlockSpec((tc, D), lambda i: (i, 0)),
                  pl.BlockSpec((tc, LANES), lambda i: (i, 0)),
                  pl.BlockSpec(memory_space=pl.ANY), full(ln_g), full(ln_b)],
        out_specs=pl.BlockSpec((tc, D), lambda i: (i, 0)),
        out_shape=jax.ShapeDtypeStruct((T, D), F32),
        scratch_shapes=[pltpu.VMEM((2, TOP_K, tc // 8, 8, D), F32), pltpu.SemaphoreType.DMA((2,))],
        compiler_params=_params(("arbitrary",)),
        name="moe_combine",
    )(dest_blocks, dest_blocks, x2, gates, yb, ln_g, ln_b)


def _moe(x2, w_router, wg, wu, wd, ln_g, ln_b, tok=512):
    T, D = x2.shape
    w_router_pad = jnp.pad(w_router.astype(F32), ((0, 0), (0, LANES - N_EXPERTS)))
    meta, gates, counts = _router(x2, w_router_pad)
    counts = counts[0, :N_EXPERTS]
    padded = (counts + MOE_ROWS - 1) // MOE_ROWS * MOE_ROWS
    pends = jnp.cumsum(padded)
    pstarts = pends - padded
    dest = pstarts[meta[:, 0:2]] + meta[:, 2:4]
    dest_blocks = dest.reshape(T // tok, 1, TOP_K * tok)
    n_blk = (T * TOP_K + MOE_ROWS - 1) // MOE_ROWS + N_EXPERTS
    n_rows = n_blk * MOE_ROWS
    blk_start = jnp.arange(n_blk, dtype=jnp.int32) * MOE_ROWS
    blk_exp = jnp.minimum(jnp.sum((blk_start[:, None] >= pends[None, :]).astype(jnp.int32), axis=1), N_EXPERTS - 1)
    n_used = (pends[-1:] // MOE_ROWS).astype(jnp.int32)
    tail = n_used + jnp.arange(N_EXPERTS, dtype=jnp.int32)
    clear_blocks = jnp.concatenate([jnp.maximum(pends // MOE_ROWS - 1, 0), jnp.minimum(tail, n_blk - 1),
                                    counts > 0, tail < n_blk]).astype(jnp.int32)
    xb = _dispatch(x2, dest_blocks, clear_blocks, n_rows, td=tok)
    yb = _experts(xb, blk_exp, n_used, wg, wu, wd)
    return _combine(x2, gates, yb, dest_blocks, ln_g, ln_b, tc=tok)


def _rot_half_cols(w):
    half = w.shape[-1] // 2
    return jnp.concatenate([-w[..., half:], w[..., :half]], axis=-1)


def _pack_w_in(w):
    D = w.shape[0]
    kr = w[:, 384:448]
    z64 = jnp.zeros((D, 64), w.dtype)
    w16 = jnp.concatenate([
        w[:, 448:2496],
        w[:, 2504:3016],
        w[:, 3016:3528],
        w[:, 4040:5064],
        w[:, 0:256],
        w[:, 256:384],
        kr, z64,
        _rot_half_cols(kr), z64,
    ], axis=1)
    w32 = jnp.concatenate([w[:, 3528:4040], w[:, 2496:2504], jnp.zeros((D, LANES - 8), w.dtype)], axis=1)
    return w16.astype(BF16), w32.astype(BF16), w[:, 5064:].astype(BF16)


def _pack_mla(w_uq, w_ukv):
    R = w_uq.shape[0]
    z64 = jnp.zeros((R, 64), w_uq.dtype)
    wq, wqr = [], []
    for h in range(MLA_HEADS):
        o = h * (MLA_NOPE + MLA_ROPE)
        rope_cols = w_uq[:, o + MLA_NOPE:o + MLA_NOPE + MLA_ROPE]
        wq += [w_uq[:, o:o + MLA_NOPE], rope_cols, z64]
        wqr += [_rot_half_cols(rope_cols), z64]
    kv = w_ukv.reshape(w_ukv.shape[0], MLA_HEADS, MLA_NOPE + MLA_V)
    wk = kv[:, :, :MLA_NOPE].reshape(w_ukv.shape[0], MLA_HEADS * MLA_NOPE)
    wv = kv[:, :, MLA_NOPE:].reshape(w_ukv.shape[0], MLA_HEADS * MLA_V)
    return (jnp.concatenate(wq, 1).astype(BF16), jnp.concatenate(wqr, 1).astype(BF16),
            wk.astype(BF16), wv.astype(BF16))


def _lane_row(v, offset=0):
    v = v.astype(F32)
    return jnp.pad(v, (offset, LANES - offset - v.shape[0]))[None, :]


def kernel(x, positions, w_in, mla_q_norm, mla_w_uq, mla_kv_norm, mla_w_ukv, gdn_conv, gdn_a_log, gdn_dt_bias, gdn_norm, pool_w, pool_scale, hgrn_lb_logits, hgrn_norm, w_branch, w_out, ln_mix_g, ln_mix_b, ffn_w_gate, ffn_w_up, ffn_w_down, moe_router, moe_w_gate, moe_w_up, moe_w_down, ln_ffn_g, ln_ffn_b):
    B, S, D = x.shape
    T = B * S
    x2 = x.reshape(T, D)

    half = MLA_ROPE // 2
    inv = ROPE_THETA ** (-jnp.arange(half, dtype=F32) / half)
    inv_row = jnp.concatenate([inv, inv, inv, inv])[None, :]
    off_row = jnp.concatenate([jnp.zeros((MLA_ROPE,), F32), jnp.full((LANES - MLA_ROPE,), -0.5 * math.pi, F32)])[None, :]
    pos_lanes = jnp.broadcast_to(positions.reshape(T, 1).astype(jnp.int32), (T, LANES))
    cos_t, sin_t = _rope_tables(pos_lanes, inv_row, off_row)

    p_lb = jax.nn.softmax(hgrn_lb_logits.astype(F32), axis=0)
    lower_bounds = jnp.cumsum(p_lb, axis=0) - p_lb[0]
    row2 = lambda v: v.astype(F32)[None, :]
    tt = jnp.arange(SPAN)
    conv_shift = jnp.concatenate([(tt[None, :] == tt[:, None] - kk) for kk in range(GDN_CONV)], axis=1).astype(BF16)

    for l in range(DEPTH):
        w16, w32, w_gate = _pack_w_in(w_in[l])
        proj16, proj32 = _inproj(x2, w16, w32)
        wq, wqr, wk, wv = _pack_mla(mla_w_uq[l], mla_w_ukv[l])
        y_a = _mla(proj16, cos_t, sin_t, row2(mla_q_norm[l]), row2(mla_kv_norm[l]), wq, wqr, wk, wv, B, S)
        proj16_3 = proj16.reshape(B, S, PK16_WIDTH)
        proj32_3 = proj32.reshape(B, S, PK32_WIDTH)
        y_b, y_d = _recurrent(proj16_3, proj32_3, gdn_conv[l].astype(F32), conv_shift, _lane_row(gdn_a_log[l]),
                              _lane_row(gdn_dt_bias[l]), row2(gdn_norm[l]), row2(lower_bounds[l]), row2(hgrn_norm[l]))
        y_b = y_b.reshape(T, GDN_HEADS * GDN_DV)
        y_d = y_d.reshape(T, HGRN_HEADS * HGRN_DV)
        x2 = _merge(x2, y_a, y_b, proj16, y_d, pool_w[l].astype(BF16), row2(pool_scale[l]), w_gate,
                    w_branch[l].astype(BF16), w_out[l].astype(BF16), row2(ln_mix_g[l]), row2(ln_mix_b[l]), S)
        j = l // 2
        if l % 2 == 0:
            x2 = _ffn(x2, ffn_w_gate[j].astype(BF16), ffn_w_up[j].astype(BF16), ffn_w_down[j].astype(BF16),
                      row2(ln_ffn_g[l]), row2(ln_ffn_b[l]))
        else:
            x2 = _moe(x2, moe_router[j], moe_w_gate[j].astype(BF16), moe_w_up[j].astype(BF16),
                      moe_w_down[j].astype(BF16), row2(ln_ffn_g[l]), row2(ln_ffn_b[l]))
    return x2.reshape(B, S, D)
```

```python
import functools
import math

import jax
import jax.numpy as jnp
from jax import lax
from jax.experimental import pallas as pl
from jax.experimental.pallas import tpu as pltpu

F32 = jnp.float32
BF16 = jnp.bfloat16

D_MODEL = 1024
DEPTH = 2
MLA_HEADS = 4
MLA_Q_RANK = 256
MLA_KV_RANK = 128
MLA_NOPE = 128
MLA_ROPE = 64
MLA_V = 128
ROPE_THETA = 10000.0
MASK_VALUE = -1e30
GDN_HEADS = 4
GDN_DK = 128
GDN_DV = 128
GDN_CONV = 4
GDN_CHUNK = 64
POOL_WINDOWS = (2, 4, 8, 16)
POOL_GROUP = 128
HGRN_HEADS = 4
HGRN_DK = 128
HGRN_DV = 128
N_BRANCH = 4
BRANCH_WIDTH = 512
D_FF = 2816
N_EXPERTS = 8
TOP_K = 2
D_FF_EXPERT = 3584
DEEPNORM_ALPHA = (2 * DEPTH) ** 0.25
LN_EPS = 1e-5
RMS_EPS = 1e-6
LOG2E = math.log2(math.e)

LANES = 128
MLA_HEAD_PAD = 256
SPAN = 256
HALF = SPAN // 2
MOE_ROWS = 512
VMEM_LIMIT = 56 * 1024 * 1024

PK_GQ, PK_GK, PK_GV, PK_GZ, PK_PU, PK_HQ, PK_HI, PK_HG = range(8)
PK_CQ = 16
PK_CKV = 34
PK_KRA = 35
PK_KRB = 36
PK16_WIDTH = 37 * LANES
PK32_HF = 0
PK32_AB = 4
PK32_WIDTH = 5 * LANES


def _params(sem, vmem=VMEM_LIMIT):
    return pltpu.CompilerParams(dimension_semantics=sem, vmem_limit_bytes=vmem)


def _dot(a, b):
    return jnp.dot(a, b, preferred_element_type=F32)


def _dot_nt(a, b):
    return lax.dot_general(a, b, (((1,), (1,)), ((), ())), preferred_element_type=F32)


def _dot_tn(a, b):
    return lax.dot_general(a, b, (((0,), (0,)), ((), ())), preferred_element_type=F32)


def _split3(x):
    hi = x.astype(BF16)
    r1 = x - hi.astype(F32)
    mid = r1.astype(BF16)
    lo = (r1 - mid.astype(F32)).astype(BF16)
    return hi, mid, lo


def _exact_left_dot(m01, x):
    hi, mid, lo = _split3(x)
    return _dot(m01, hi) + _dot(m01, mid) + _dot(m01, lo)


def _sigmoid(x):
    return 1.0 / (1.0 + jnp.exp(-x))


def _silu(x):
    return x * _sigmoid(x)


def _softplus(x):
    return jnp.maximum(x, 0.0) + jnp.log(1.0 + jnp.exp(-jnp.abs(x)))


def _layer_norm(x, g, b):
    mu = jnp.mean(x, -1, keepdims=True)
    xc = x - mu
    var = jnp.mean(xc * xc, -1, keepdims=True)
    return xc * lax.rsqrt(var + LN_EPS) * g + b


def _rms(x, g):
    return x * lax.rsqrt(jnp.mean(x * x, -1, keepdims=True) + RMS_EPS) * g


def _lane_blocks(s):
    return [s[:, c * LANES:(c + 1) * LANES] for c in range(s.shape[1] // LANES)]


def _row_max(s):
    return jnp.max(functools.reduce(jnp.maximum, _lane_blocks(s)), -1, keepdims=True)


def _row_sum(s):
    return jnp.sum(functools.reduce(jnp.add, _lane_blocks(s)), -1, keepdims=True)


def _inproj_kernel(x_ref, w16_ref, w32_ref, o16_ref, o32_ref):
    x16 = x_ref[...].astype(BF16)
    o16_ref[...] = _dot(x16, w16_ref[...]).astype(o16_ref.dtype)
    o32_ref[...] = _dot(x16, w32_ref[...])


def _inproj(x2, w16, w32, tm=512):
    T, D = x2.shape
    full = lambda a: pl.BlockSpec(a.shape, lambda i: (0, 0), pipeline_mode=pl.Buffered(1))
    return pl.pallas_call(
        _inproj_kernel,
        grid=(T // tm,),
        in_specs=[pl.BlockSpec((tm, D), lambda i: (i, 0)), full(w16), full(w32)],
        out_specs=[pl.BlockSpec((tm, w16.shape[1]), lambda i: (i, 0)),
                   pl.BlockSpec((tm, w32.shape[1]), lambda i: (i, 0))],
        out_shape=[jax.ShapeDtypeStruct((T, w16.shape[1]), BF16),
                   jax.ShapeDtypeStruct((T, w32.shape[1]), F32)],
        compiler_params=_params(("parallel",)),
        name="inproj",
    )(x2, w16, w32)


def _rope_table_kernel(pos_ref, inv_ref, off_ref, c_ref, s_ref):
    cs = jnp.cos(pos_ref[...].astype(F32) * inv_ref[...] + off_ref[...])
    c_ref[...] = cs
    s_ref[...] = pltpu.roll(cs, LANES // 2, 1)


def _rope_tables(pos_col, inv_row, off_row, tm=1024):
    T = pos_col.shape[0]
    return pl.pallas_call(
        _rope_table_kernel,
        grid=(T // tm,),
        in_specs=[pl.BlockSpec((tm, 1), lambda i: (i, 0)),
                  pl.BlockSpec((1, LANES), lambda i: (0, 0)),
                  pl.BlockSpec((1, LANES), lambda i: (0, 0))],
        out_specs=[pl.BlockSpec((tm, LANES), lambda i: (i, 0))] * 2,
        out_shape=[jax.ShapeDtypeStruct((T, LANES), F32)] * 2,
        compiler_params=_params(("parallel",)),
        name="rope_tables",
    )(pos_col, inv_row, off_row)


def _mla_prep_rows(rs, cq_ref, ckv_ref, kra_ref, krb_ref, c_ref, s_ref, gq_ref, gkv_ref,
                   wq_ref, wqr_ref, wk_ref, wv_ref, q_s, k_s, v_s):
    scale = (MLA_NOPE + MLA_ROPE) ** -0.5 * LOG2E
    cos = c_ref[rs, :]
    sin = s_ref[rs, :]
    nq = _rms(cq_ref[rs, :].astype(F32), gq_ref[...]).astype(BF16)
    q_pre = _dot(nq, wq_ref[...])
    q_rot = _dot(nq, wqr_ref[...])
    nkv = _rms(ckv_ref[rs, :].astype(F32), gkv_ref[...]).astype(BF16)
    k_nope = _dot(nkv, wk_ref[...])
    v_s[rs, :] = _dot(nkv, wv_ref[...]).astype(v_s.dtype)
    k_rope = (kra_ref[rs, :].astype(F32) * cos + krb_ref[rs, :].astype(F32) * sin).astype(k_s.dtype)
    for h in range(MLA_HEADS):
        o = h * MLA_HEAD_PAD
        q_s[rs, o:o + LANES] = (q_pre[:, o:o + LANES] * scale).astype(q_s.dtype)
        q_s[rs, o + LANES:o + 2 * LANES] = (
            (q_pre[:, o + LANES:o + 2 * LANES] * cos + q_rot[:, h * LANES:(h + 1) * LANES] * sin) * scale
        ).astype(q_s.dtype)
        k_s[rs, o:o + LANES] = k_nope[:, h * LANES:(h + 1) * LANES].astype(k_s.dtype)
        k_s[rs, o + LANES:o + 2 * LANES] = k_rope


def _attn_tile(q_s, k_s, v_s, o_ref, t, tq, causal):
    r0 = t * tq
    for h in range(MLA_HEADS):
        hs = slice(h * MLA_HEAD_PAD, (h + 1) * MLA_HEAD_PAD)
        vs = slice(h * MLA_V, (h + 1) * MLA_V)
        qh = q_s[r0:r0 + tq, hs]
        s_d = jnp.where(causal, _dot_nt(qh, k_s[r0:r0 + tq, hs]), MASK_VALUE)
        m = _row_max(s_d)
        if t > 0:
            s_o = _dot_nt(qh, k_s[0:r0, hs])
            m = jnp.maximum(m, _row_max(s_o))
        p_d = jnp.exp2(s_d - m)
        l = _row_sum(p_d)
        acc = _dot(p_d.astype(BF16), v_s[r0:r0 + tq, vs])
        if t > 0:
            p_o = jnp.exp2(s_o - m)
            l = l + _row_sum(p_o)
            acc = acc + _dot(p_o.astype(BF16), v_s[0:r0, vs])
        o_ref[:, vs] = (acc / l).astype(o_ref.dtype)


def _mla_kernel(*refs, seq, tq, tprep):
    prep_refs, o_ref, (q_s, k_s, v_s) = refs[:12], refs[12], refs[13:]
    i = pl.program_id(1)

    @pl.when(i == 0)
    def _():
        for c in range(seq // tprep):
            _mla_prep_rows(slice(c * tprep, (c + 1) * tprep), *prep_refs, q_s, k_s, v_s)

    row = lax.broadcasted_iota(jnp.int32, (tq, tq), 0)
    col = lax.broadcasted_iota(jnp.int32, (tq, tq), 1)
    causal = row >= col
    for t in range(seq // tq):
        pl.when(i == t)(functools.partial(_attn_tile, q_s, k_s, v_s, o_ref, t, tq, causal))


def _mla(proj, cos_t, sin_t, g_q, g_kv, wq, wqr, wk, wv, B, S, tq=256, tprep=512):
    T = proj.shape[0]
    nq = S // tq
    HW = MLA_HEADS * MLA_HEAD_PAD
    HV = MLA_HEADS * MLA_V
    seq = lambda w, j: pl.BlockSpec((S, w), lambda b, i, j=j: (b, j))
    full = lambda a: pl.BlockSpec(a.shape, lambda b, i: (0,) * a.ndim)
    return pl.pallas_call(
        functools.partial(_mla_kernel, seq=S, tq=tq, tprep=min(tprep, S)),
        grid=(B, nq),
        in_specs=[seq(256, PK_CQ), seq(128, PK_CKV), seq(128, PK_KRA), seq(128, PK_KRB),
                  seq(128, 0), seq(128, 0), full(g_q), full(g_kv), full(wq), full(wqr), full(wk), full(wv)],
        out_specs=pl.BlockSpec((tq, HV), lambda b, i: (b * nq + i, 0)),
        out_shape=jax.ShapeDtypeStruct((T, HV), BF16),
        scratch_shapes=[pltpu.VMEM((S, HW), BF16), pltpu.VMEM((S, HW), BF16), pltpu.VMEM((S, HV), BF16)],
        compiler_params=_params(("parallel", "arbitrary")),
        name="mla",
    )(proj, proj, proj, proj, cos_t, sin_t, g_q, g_kv, wq, wqr, wk, wv)


def _gdn_body(q_ref, k_ref, v_ref, z_ref, ab_ref, cw_ref, sh_ref, alog_ref, dtb_ref, ng_ref, o_ref,
              carry_ref, state_ref, *, nb):
    si = pl.program_id(1)
    HD = GDN_HEADS * GDN_DK
    NC = SPAN // GDN_CHUNK

    @pl.when(si == 0)
    def _():
        carry_ref[...] = jnp.zeros_like(carry_ref)
        state_ref[...] = jnp.zeros_like(state_ref)

    yield

    def conv_silu(x_ref, b, idx):
        x16 = x_ref[b]
        w16 = cw_ref[:, idx * HD:(idx + 1) * HD].astype(BF16)
        taps = [w16[GDN_CONV - 1 - kk:GDN_CONV - kk] for kk in range(GDN_CONV)]
        y = _dot(sh_ref[...], jnp.concatenate([x16 * tap for tap in taps], axis=0))
        r8 = lax.broadcasted_iota(jnp.int32, (8, HD), 0)
        head = y[0:8]
        for kk in range(1, GDN_CONV):
            term = (pltpu.roll(carry_ref[b, idx], kk, 0).astype(BF16) * taps[kk]).astype(F32)
            head = head + jnp.where(r8 < kk, term, 0.0)
        carry_ref[b, idx] = x16[SPAN - 8:SPAN].astype(F32)
        return _silu(jnp.concatenate([head, y[8:]], axis=0))

    row = lax.broadcasted_iota(jnp.int32, (SPAN, SPAN), 0)
    col = lax.broadcasted_iota(jnp.int32, (SPAN, SPAN), 1)
    same_chunk = (row // GDN_CHUNK) == (col // GDN_CHUNK)
    tril = same_chunk & (col <= row)
    stril = same_chunk & (col < row)
    eye = (row == col).astype(F32)
    tril16 = tril.astype(BF16)

    units = []
    for b in range(nb):
        q = conv_silu(q_ref, b, 0)
        k = conv_silu(k_ref, b, 1)
        v = conv_silu(v_ref, b, 2)
        ab = ab_ref[b]
        g_full = -jnp.exp(alog_ref[...]) * _softplus(ab + dtb_ref[...])
        beta_full = _sigmoid(ab)
        gc_full = _exact_left_dot(tril16, g_full)
        gc_t = gc_full.T
        for h in range(GDN_HEADS):
            sl = slice(h * GDN_DK, (h + 1) * GDN_DK)
            qh = q[:, sl]
            kh = k[:, sl]
            qh = qh * lax.rsqrt(jnp.sum(qh * qh, -1, keepdims=True) + RMS_EPS) * (GDN_DK ** -0.5)
            kh = kh * lax.rsqrt(jnp.sum(kh * kh, -1, keepdims=True) + RMS_EPS)
            gc = gc_full[:, h:h + 1]
            beta = beta_full[:, GDN_HEADS + h:GDN_HEADS + h + 1]
            decay = jnp.where(tril, jnp.exp(gc - gc_t[h:h + 1, :]), 0.0)
            k16 = kh.astype(BF16)
            a = jnp.where(stril, beta * _dot_nt(k16, k16) * decay, 0.0)
            qk16 = (_dot_nt(qh.astype(BF16), k16) * decay).astype(BF16)
            egc = jnp.exp(gc)
            units.append(dict(
                b=b, h=h, sl=sl, kh=kh, gc=gc, qk16=qk16, a=a,
                rhs=jnp.concatenate([v[:, sl] * beta, kh * (beta * egc)], axis=1).astype(BF16),
                qg=(qh * egc).astype(BF16)))
        yield

    xs = [eye - u["a"] for u in units]
    ps = [u["a"].astype(BF16) for u in units]
    for _ in range(5):
        ps = [_dot(p, p).astype(BF16) for p in ps]
        xs = [x + _dot(x.astype(BF16), p) for x, p in zip(xs, ps)]
        yield
    uws = [_dot(x.astype(BF16), u["rhs"]) for x, u in zip(xs, units)]
    yield

    states = [state_ref[u["b"], u["h"]] for u in units]
    o_state = [[] for _ in units]
    v_new = [[] for _ in units]
    for c in range(NC):
        cs = slice(c * GDN_CHUNK, (c + 1) * GDN_CHUNK)
        s16 = [s.astype(BF16) for s in states]
        vns = [uw[cs, :GDN_DV] - _dot(uw[cs, GDN_DV:].astype(BF16), s) for uw, s in zip(uws, s16)]
        for i, u in enumerate(units):
            o_state[i].append(_dot(u["qg"][cs], s16[i]))
            v_new[i].append(vns[i])
            g_last = u["gc"][(c + 1) * GDN_CHUNK - 1:(c + 1) * GDN_CHUNK, :]
            kd = (u["kh"][cs] * jnp.exp(g_last - u["gc"][cs])).astype(BF16)
            states[i] = states[i] * jnp.exp(g_last) + _dot_tn(kd, vns[i].astype(BF16))
        yield
    for i, u in enumerate(units):
        state_ref[u["b"], u["h"]] = states[i]
        o = jnp.concatenate(o_state[i], axis=0) + _dot(u["qk16"], jnp.concatenate(v_new[i], axis=0).astype(BF16))
        o = _rms(o, ng_ref[...]) * _silu(z_ref[u["b"], :, u["sl"]].astype(F32))
        o_ref[u["b"], :, u["sl"]] = o.astype(o_ref.dtype)
        if i % GDN_HEADS == GDN_HEADS - 1:
            yield


def _gdn_kernel(*refs, nb):
    for _ in _gdn_body(*refs, nb=nb):
        pass


def _gdn(proj16, proj32, conv_w, shift, alog_row, dtb_row, norm_g, nb=2):
    B, S, _ = proj16.shape
    HD = GDN_HEADS * GDN_DK
    blk = lambda w, j: pl.BlockSpec((nb, SPAN, w), lambda b, s, j=j: (b, s, j))
    full = lambda a: pl.BlockSpec(a.shape, lambda b, s: (0,) * a.ndim)
    return pl.pallas_call(
        functools.partial(_gdn_kernel, nb=nb),
        grid=(B // nb, S // SPAN),
        in_specs=[blk(HD, PK_GQ), blk(HD, PK_GK), blk(HD, PK_GV), blk(HD, PK_GZ), blk(LANES, PK32_AB),
                  full(conv_w), full(shift), full(alog_row), full(dtb_row), full(norm_g)],
        out_specs=blk(HD, 0),
        out_shape=jax.ShapeDtypeStruct((B, S, HD), BF16),
        scratch_shapes=[pltpu.VMEM((nb, 3, 8, HD), F32), pltpu.VMEM((nb, GDN_HEADS, GDN_DK, GDN_DV), F32)],
        compiler_params=_params(("parallel", "arbitrary")),
        name="gdn",
    )(proj16, proj16, proj16, proj16, proj32, conv_w, shift, alog_row, dtb_row, norm_g)


def _pool_tile(u_ref, w_ref, sc_ref, carry_ref, si, tp):
    G = POOL_GROUP

    @pl.when(si == 0)
    def _():
        carry_ref[...] = jnp.zeros_like(carry_ref)

    u = u_ref[...].astype(F32)
    ext = jnp.concatenate([carry_ref[...], u], axis=0)
    carry_ref[...] = u[tp - 16:tp]
    t = si * tp + lax.broadcasted_iota(jnp.int32, (tp, 1), 0)
    win_sum = ext
    outs = []
    shift = 1
    for gi, win in enumerate(POOL_WINDOWS):
        while shift < win:
            win_sum = win_sum + pltpu.roll(win_sum, shift, 0)
            shift *= 2
        cnt = jnp.minimum(t + 1, win).astype(F32)
        ug = u[:, gi * G:(gi + 1) * G]
        pooled = win_sum[16:, gi * G:(gi + 1) * G] / cnt - ug
        outs.append(_dot(pooled.astype(BF16), w_ref[gi]))
    return (jnp.concatenate(outs, axis=1) * sc_ref[...]).astype(BF16)


def _mid_broadcast(c, s):
    n = c.shape[0]
    if 2 * s >= 8:
        c3 = c.reshape(n // (2 * s), 2 * s, c.shape[1])
        mid = jnp.broadcast_to(c3[:, s - 1:s, :], c3.shape)
        return mid.reshape(c.shape)
    pos = lax.broadcasted_iota(jnp.int32, c.shape, 0) % (2 * s)
    out = c
    for p in range(2 * s):
        d = p - (s - 1)
        if d != 0:
            out = jnp.where(pos == p, pltpu.roll(c, d % n, 0), out)
    return out


def _hgrn_body(q_ref, f_ref, i_ref, g_ref, lb_ref, ng_ref, o_ref, state_ref, *, nb):
    si = pl.program_id(1)

    @pl.when(si == 0)
    def _():
        state_ref[...] = jnp.zeros_like(state_ref)

    yield

    row = lax.broadcasted_iota(jnp.int32, (SPAN, SPAN), 0)
    col = lax.broadcasted_iota(jnp.int32, (SPAN, SPAN), 1)
    lower16 = (col <= row).astype(BF16)
    heads = [slice(h * HGRN_DK, (h + 1) * HGRN_DK) for h in range(HGRN_HEADS)]
    halves = [slice(0, HALF), slice(HALF, SPAN)]
    ri = lax.broadcasted_iota(jnp.int32, (HALF, HALF), 0)
    ci = lax.broadcasted_iota(jnp.int32, (HALF, HALF), 1)
    rowl = lax.broadcasted_iota(jnp.int32, (SPAN, LANES), 0)
    lb = lb_ref[...]

    seqs = []
    for b in range(nb):
        q = _silu(q_ref[b].astype(F32))
        fp = f_ref[b]
        log_f = jnp.log(lb + (1.0 - lb) * _sigmoid(fp))
        k = (1.0 - lb) * _sigmoid(-fp)
        c = _exact_left_dot(lower16, log_f)
        q16 = q.astype(BF16)
        k16 = k.astype(BF16)
        att = [[jnp.where(ri == ci, _dot_nt(q16[r, sl], k16[r, sl]), 0.0) for sl in heads] for r in halves]
        seqs.append(dict(q=q, k=k, c=c, q16=q16, k16=k16, att=att, cross=None))
        yield

    s = 1
    while s < SPAN:
        sgn = jnp.where(((rowl // s) % 2) == 1, LOG2E, -LOG2E)
        if s < HALF:
            bi = ri // s
            bj = ci // s
            region = ((bi - bj) * 2 + (bj & 1)) == 2
        for sq in seqs:
            diff = sq["c"] - _mid_broadcast(sq["c"], s)
            w16 = jnp.concatenate([jnp.exp2(diff[:, sl] * sgn) for sl in heads], axis=1).astype(BF16)
            ql = sq["q16"] * w16
            kl = sq["k16"] * w16
            if s < HALF:
                for d, r in enumerate(halves):
                    for h, sl in enumerate(heads):
                        sq["att"][d][h] = jnp.where(region, _dot_nt(ql[r, sl], kl[r, sl]), sq["att"][d][h])
            else:
                sq["cross"] = [_dot_nt(ql[halves[1], sl], kl[halves[0], sl]) for sl in heads]
        s *= 2
        yield

    for b, sq in enumerate(seqs):
        c = sq["c"]
        c_last = c[SPAN - 1:SPAN, :]
        v16 = i_ref[b]
        qe = (sq["q"] * jnp.exp(c)).astype(BF16)
        ke = (sq["k"] * jnp.exp(c_last - c)).astype(BF16)
        e_last = jnp.exp(c_last)
        for h, sl in enumerate(heads):
            st = state_ref[b, h]
            v_lo = v16[halves[0], sl]
            v_hi = v16[halves[1], sl]
            o_lo = _dot(sq["att"][0][h].astype(BF16), v_lo)
            o_hi = _dot(sq["cross"][h].astype(BF16), v_lo) + _dot(sq["att"][1][h].astype(BF16), v_hi)
            o = jnp.concatenate([o_lo, o_hi], axis=0) + _dot_nt(qe[:, sl], st.astype(BF16))
            state_ref[b, h] = st * e_last[:, sl] + _dot_tn(v16[:, sl], ke[:, sl])
            o = _rms(o, ng_ref[...]) * _sigmoid(g_ref[b, :, sl].astype(F32))
            o_ref[b, :, sl] = o.astype(o_ref.dtype)
        yield


def _hgrn_kernel(*refs, nb):
    for _ in _hgrn_body(*refs, nb=nb):
        pass


def _recurrent_kernel(*refs, nb):
    g_in, h_in, (g_out, h_out), g_scr, h_scr = refs[:10], refs[10:16], refs[16:18], refs[18:20], refs[20:]
    bodies = [_gdn_body(*g_in, g_out, *g_scr, nb=nb), _hgrn_body(*h_in, h_out, *h_scr, nb=nb)]
    while bodies:
        for body in list(bodies):
            try:
                next(body)
            except StopIteration:
                bodies.remove(body)


def _recurrent(proj16, proj32, conv_w, shift, alog_row, dtb_row, gdn_norm_g, lb_row, hgrn_norm_g, nb=2):
    B, S, _ = proj16.shape
    HD = GDN_HEADS * GDN_DK
    blk = lambda w, j: pl.BlockSpec((nb, SPAN, w), lambda b, s, j=j: (b, s, j))
    full = lambda a: pl.BlockSpec(a.shape, lambda b, s: (0,) * a.ndim)
    out = jax.ShapeDtypeStruct((B, S, HD), BF16)
    return pl.pallas_call(
        functools.partial(_recurrent_kernel, nb=nb),
        grid=(B // nb, S // SPAN),
        in_specs=[blk(HD, PK_GQ), blk(HD, PK_GK), blk(HD, PK_GV), blk(HD, PK_GZ), blk(LANES, PK32_AB),
                  full(conv_w), full(shift), full(alog_row), full(dtb_row), full(gdn_norm_g),
                  blk(HD, PK_HQ), blk(HD, PK32_HF), blk(HD, PK_HI), blk(HD, PK_HG), full(lb_row), full(hgrn_norm_g)],
        out_specs=[blk(HD, 0), blk(HD, 0)],
        out_shape=[out, out],
        scratch_shapes=[pltpu.VMEM((nb, 3, 8, HD), F32), pltpu.VMEM((nb, GDN_HEADS, GDN_DK, GDN_DV), F32),
                        pltpu.VMEM((nb, HGRN_HEADS, HGRN_DV, HGRN_DK), F32)],
        compiler_params=_params(("parallel", "arbitrary")),
        name="gdn_hgrn2",
    )(proj16, proj16, proj16, proj16, proj32, conv_w, shift, alog_row, dtb_row, gdn_norm_g,
      proj16, proj32, proj16, proj16, lb_row, hgrn_norm_g)


def _hgrn(proj16, proj32, lb_row, norm_g, nb=2):
    B, S, _ = proj16.shape
    HD = HGRN_HEADS * HGRN_DK
    blk = lambda j: pl.BlockSpec((nb, SPAN, HD), lambda b, s, j=j: (b, s, j))
    full = lambda a: pl.BlockSpec(a.shape, lambda b, s: (0,) * a.ndim)
    return pl.pallas_call(
        functools.partial(_hgrn_kernel, nb=nb),
        grid=(B // nb, S // SPAN),
        in_specs=[blk(PK_HQ), blk(PK32_HF), blk(PK_HI), blk(PK_HG), full(lb_row), full(norm_g)],
        out_specs=blk(0),
        out_shape=jax.ShapeDtypeStruct((B, S, HD), BF16),
        scratch_shapes=[pltpu.VMEM((nb, HGRN_HEADS, HGRN_DV, HGRN_DK), F32)],
        compiler_params=_params(("parallel", "arbitrary")),
        name="hgrn2",
    )(proj16, proj32, proj16, proj16, lb_row, norm_g)


def _merge_kernel(x_ref, ya_ref, yb_ref, u_ref, yd_ref, pw_ref, psc_ref, wg_ref, wb_ref, wo_ref, g_ref, b_ref,
                  o_ref, carry_ref, *, tiles_per_seq, tm):
    yc = _pool_tile(u_ref, pw_ref, psc_ref, carry_ref, pl.program_id(0) % tiles_per_seq, tm)
    x = x_ref[...]
    x16 = x.astype(BF16)
    merged = None
    for m, y in enumerate((ya_ref[...], yb_ref[...], yc, yd_ref[...])):
        gate = _sigmoid(_dot(x16, wg_ref[:, m * D_MODEL:(m + 1) * D_MODEL]))
        term = gate * _dot(y, wb_ref[m])
        merged = term if merged is None else merged + term
    h = DEEPNORM_ALPHA * x + _dot(merged.astype(BF16), wo_ref[...])
    o_ref[...] = _layer_norm(h, g_ref[...], b_ref[...])


def _merge(x2, ya, yb, proj16, yd, pool_w, pool_scale, w_gate, w_branch, w_out, ln_g, ln_b, S, tm=512):
    T, D = x2.shape
    row = lambda w, j=0: pl.BlockSpec((tm, w), lambda i, j=j: (i, j))
    full = lambda a: pl.BlockSpec(a.shape, lambda i: (0,) * a.ndim, pipeline_mode=pl.Buffered(1))
    return pl.pallas_call(
        functools.partial(_merge_kernel, tiles_per_seq=S // tm, tm=tm),
        grid=(T // tm,),
        in_specs=[row(D), row(BRANCH_WIDTH), row(BRANCH_WIDTH), row(BRANCH_WIDTH, PK_PU), row(BRANCH_WIDTH),
                  full(pool_w), full(pool_scale), full(w_gate), full(w_branch), full(w_out), full(ln_g), full(ln_b)],
        out_specs=row(D),
        out_shape=jax.ShapeDtypeStruct((T, D), F32),
        scratch_shapes=[pltpu.VMEM((16, len(POOL_WINDOWS) * POOL_GROUP), F32)],
        compiler_params=_params(("arbitrary",)),
        name="merge",
    )(x2, ya, yb, proj16, yd, pool_w, pool_scale, w_gate, w_branch, w_out, ln_g, ln_b)


def _ffn_kernel(x_ref, wg_ref, wu_ref, wd_ref, g_ref, b_ref, o_ref):
    x = x_ref[...]
    x16 = x.astype(BF16)
    h = _silu(_dot(x16, wg_ref[...])) * _dot(x16, wu_ref[...])
    y = DEEPNORM_ALPHA * x + _dot(h.astype(BF16), wd_ref[...])
    o_ref[...] = _layer_norm(y, g_ref[...], b_ref[...])


def _ffn(x2, wg, wu, wd, ln_g, ln_b, tm=512):
    T, D = x2.shape
    row = pl.BlockSpec((tm, D), lambda i: (i, 0))
    full = lambda a: pl.BlockSpec(a.shape, lambda i: (0,) * a.ndim, pipeline_mode=pl.Buffered(1))
    return pl.pallas_call(
        _ffn_kernel,
        grid=(T // tm,),
        in_specs=[row, full(wg), full(wu), full(wd), full(ln_g), full(ln_b)],
        out_specs=row,
        out_shape=jax.ShapeDtypeStruct((T, D), F32),
        compiler_params=_params(("parallel",)),
        name="ffn_dense",
    )(x2, wg, wu, wd, ln_g, ln_b)


def _router_kernel(x_ref, w_ref, meta_ref, gate_ref, cnt_ref, base_ref, *, tm):
    i = pl.program_id(0)

    @pl.when(i == 0)
    def _():
        base_ref[...] = jnp.zeros_like(base_ref)

    xh, xm, _ = _split3(x_ref[...])
    wh, wm, _ = _split3(w_ref[...])
    logits = _dot(xh, wh) + _dot(xh, wm) + _dot(xm, wh)
    lane = lax.broadcasted_iota(jnp.int32, (tm, LANES), 1)
    neg = jnp.float32(-jnp.inf)
    logits = jnp.where(lane < N_EXPERTS, logits, neg)
    v1 = jnp.max(logits, -1, keepdims=True)
    e1 = jnp.min(jnp.where(logits == v1, lane, LANES), -1, keepdims=True)
    rest = jnp.where(lane == e1, neg, logits)
    v2 = jnp.max(rest, -1, keepdims=True)
    e2 = jnp.min(jnp.where(rest == v2, lane, LANES), -1, keepdims=True)
    ex = jnp.exp(v2 - v1)
    w1 = 1.0 / (1.0 + ex)
    w2 = ex / (1.0 + ex)
    onehot = ((lane == e1) | (lane == e2)).astype(F32)
    r = lax.broadcasted_iota(jnp.int32, (tm, tm), 0)
    cc = lax.broadcasted_iota(jnp.int32, (tm, tm), 1)
    before = _dot((cc < r).astype(BF16), onehot.astype(BF16)) + base_ref[0:1, :]
    rank1 = jnp.sum(jnp.where(lane == e1, before, 0.0), -1, keepdims=True)
    rank2 = jnp.sum(jnp.where(lane == e2, before, 0.0), -1, keepdims=True)
    total = base_ref[0:1, :] + jnp.sum(onehot, 0, keepdims=True)
    base_ref[...] = jnp.broadcast_to(total, base_ref.shape)
    cnt_ref[...] = jnp.broadcast_to(total, cnt_ref.shape).astype(jnp.int32)
    meta = jnp.where(lane == 0, e1, jnp.where(lane == 1, e2, jnp.where(
        lane == 2, rank1.astype(jnp.int32), jnp.where(lane == 3, rank2.astype(jnp.int32), 0))))
    meta_ref[...] = meta[:, 0:meta_ref.shape[1]]
    gate_ref[...] = jnp.where(lane == 0, w1, jnp.where(lane == 1, w2, 0.0))


def _router(x2, w_router_pad, tm=512):
    T, D = x2.shape
    return pl.pallas_call(
        functools.partial(_router_kernel, tm=tm),
        grid=(T // tm,),
        in_specs=[pl.BlockSpec((tm, D), lambda i: (i, 0)),
                  pl.BlockSpec(w_router_pad.shape, lambda i: (0, 0))],
        out_specs=[pl.BlockSpec((tm, 8), lambda i: (i, 0)),
                   pl.BlockSpec((tm, LANES), lambda i: (i, 0)),
                   pl.BlockSpec((8, LANES), lambda i: (0, 0))],
        out_shape=[jax.ShapeDtypeStruct((T, 8), jnp.int32),
                   jax.ShapeDtypeStruct((T, LANES), F32),
                   jax.ShapeDtypeStruct((8, LANES), jnp.int32)],
        scratch_shapes=[pltpu.VMEM((8, LANES), F32)],
        compiler_params=_params(("arbitrary",)),
        name="moe_router",
    )(x2, w_router_pad)


def _dispatch_kernel(zb_ref, dest_ref, x_ref, xb_ref, zero_ref, sem, zsem, *, td):
    @pl.when(pl.program_id(0) == 0)
    def _():
        zero_ref[...] = jnp.zeros_like(zero_ref)
        nz = 2 * N_EXPERTS
        clear = [pltpu.make_async_copy(zero_ref, xb_ref.at[pl.ds(zb_ref[z] * MOE_ROWS, MOE_ROWS)], zsem)
                 for z in range(nz)]
        for z, cp in enumerate(clear):
            pl.when(zb_ref[nz + z] > 0)(cp.start)
        for z, cp in enumerate(clear):
            pl.when(zb_ref[nz + z] > 0)(cp.wait)

    def start(g, c):
        for u in range(8):
            for slot in range(TOP_K):
                pltpu.make_async_copy(x_ref.at[g, pl.ds(u, 1)],
                                      xb_ref.at[pl.ds(dest_ref[0, 0, (g * 8 + u) * TOP_K + slot], 1)],
                                      sem).start(priority=slot)
        return c

    lax.fori_loop(0, td // 8, start, 0)
    pltpu.make_async_copy(xb_ref.at[pl.ds(0, TOP_K * td)], xb_ref.at[pl.ds(0, TOP_K * td)], sem).wait()


def _dispatch(x2, dest_blocks, clear_blocks, n_rows, td=256):
    T, D = x2.shape
    grid_spec = pltpu.PrefetchScalarGridSpec(
        num_scalar_prefetch=1,
        grid=(T // td,),
        in_specs=[pl.BlockSpec((1, 1, TOP_K * td), lambda i, zb: (i, 0, 0), memory_space=pltpu.SMEM),
                  pl.BlockSpec((td // 8, 8, D), lambda i, zb: (i, 0, 0))],
        out_specs=pl.BlockSpec(memory_space=pl.ANY),
        scratch_shapes=[pltpu.VMEM((MOE_ROWS, D), F32), pltpu.SemaphoreType.DMA(()), pltpu.SemaphoreType.DMA(())],
    )
    return pl.pallas_call(
        functools.partial(_dispatch_kernel, td=td),
        grid_spec=grid_spec,
        out_shape=jax.ShapeDtypeStruct((n_rows, D), F32),
        compiler_params=_params(("arbitrary",)),
        name="moe_dispatch",
    )(clear_blocks, dest_blocks, x2.reshape(T // 8, 8, D))


def _expert_kernel(blk_exp_ref, n_used_ref, x_ref, wg_ref, wu_ref, wd_ref, o_ref, acc_ref):
    i = pl.program_id(0)
    j = pl.program_id(1)

    @pl.when(i < n_used_ref[0])
    def _():
        x16 = x_ref[...].astype(BF16)
        h = _silu(_dot(x16, wg_ref[0])) * _dot(x16, wu_ref[0])
        y = _dot(h.astype(BF16), wd_ref[0])

        @pl.when(j == 0)
        def _():
            acc_ref[...] = y

        @pl.when(j > 0)
        def _():
            acc_ref[...] += y

        @pl.when(j == pl.num_programs(1) - 1)
        def _():
            o_ref[...] = acc_ref[...]

    @pl.when(i >= n_used_ref[0])
    def _():
        o_ref[...] = jnp.zeros_like(o_ref)


def _experts(xb, blk_exp, n_used, wg, wu, wd, tf=1792):
    n_rows, D = xb.shape
    n_blk = n_rows // MOE_ROWS
    nf = D_FF_EXPERT // tf

    def rows(i, j, be, nu):
        return (jnp.minimum(i, nu[0] - 1), 0)

    def chunk(i, j, nu):
        ic = jnp.minimum(i, nu[0] - 1)
        return jnp.where(ic % 2 == 0, jnp.where(i < nu[0], j, nf - 1), jnp.where(i < nu[0], nf - 1 - j, 0))

    grid_spec = pltpu.PrefetchScalarGridSpec(
        num_scalar_prefetch=2,
        grid=(n_blk, nf),
        in_specs=[pl.BlockSpec((MOE_ROWS, D), rows),
                  pl.BlockSpec((1, D, tf), lambda i, j, be, nu: (be[jnp.minimum(i, nu[0] - 1)], 0, chunk(i, j, nu))),
                  pl.BlockSpec((1, D, tf), lambda i, j, be, nu: (be[jnp.minimum(i, nu[0] - 1)], 0, chunk(i, j, nu))),
                  pl.BlockSpec((1, tf, D), lambda i, j, be, nu: (be[jnp.minimum(i, nu[0] - 1)], chunk(i, j, nu), 0))],
        out_specs=pl.BlockSpec((MOE_ROWS, D), lambda i, j, be, nu: (i, 0)),
        scratch_shapes=[pltpu.VMEM((MOE_ROWS, D), F32)],
    )
    return pl.pallas_call(
        _expert_kernel,
        grid_spec=grid_spec,
        out_shape=jax.ShapeDtypeStruct((n_rows, D), F32),
        compiler_params=_params(("arbitrary", "arbitrary")),
        name="moe_experts",
    )(blk_exp, n_used, xb, wg, wu, wd)


def _combine_kernel(dest_ref, dnext_ref, x_ref, gate_ref, yb_ref, g_ref, b_ref, o_ref, buf_ref, sem, *, tc):
    i = pl.program_id(0)
    n = pl.num_programs(0)

    def issue(d_ref, slot):
        def start(g, c):
            for u in range(8):
                for kk in range(TOP_K):
                    pltpu.make_async_copy(yb_ref.at[pl.ds(d_ref[0, 0, (g * 8 + u) * TOP_K + kk], 1)],
                                          buf_ref.at[slot, kk, g, pl.ds(u, 1)], sem.at[slot]).start(priority=kk)
            return c
        lax.fori_loop(0, tc // 8, start, 0)

    @pl.when(i == 0)
    def _():
        issue(dest_ref, 0)

    @pl.when(i + 1 < n)
    def _():
        issue(dnext_ref, (i + 1) % 2)

    slot = i % 2
    pltpu.make_async_copy(buf_ref.at[slot], buf_ref.at[slot], sem.at[slot]).wait()
    gate = gate_ref[...]
    D = x_ref.shape[1]
    ffn = buf_ref[slot, 0].reshape(tc, D) * gate[:, 0:1] + buf_ref[slot, 1].reshape(tc, D) * gate[:, 1:2]
    o_ref[...] = _layer_norm(DEEPNORM_ALPHA * x_ref[...] + ffn, g_ref[...], b_ref[...])


def _combine(x2, gates, yb, dest_blocks, ln_g, ln_b, tc=256):
    T, D = x2.shape
    n = T // tc
    full = lambda a: pl.BlockSpec(a.shape, lambda i: (0,) * a.ndim)
    smem = lambda imap: pl.BlockSpec((1, 1, TOP_K * tc), imap, memory_space=pltpu.SMEM)
    return pl.pallas_call(
        functools.partial(_combine_kernel, tc=tc),
        grid=(n,),
        in_specs=[smem(lambda i: (i, 0, 0)), smem(lambda i: (jnp.minimum(i + 1, n - 1), 0, 0)),
                  pl.BlockSpec((tc, D), lambda i: (i, 0)),
                  pl.BlockSpec((tc, LANES), lambda i: (i, 0)),
                  pl.BlockSpec(memory_space=pl.ANY), full(ln_g), full(ln_b)],
        out_specs=pl.BlockSpec((tc, D), lambda i: (i, 0)),
        out_shape=jax.ShapeDtypeStruct((T, D), F32),
        scratch_shapes=[pltpu.VMEM((2, TOP_K, tc // 8, 8, D), F32), pltpu.SemaphoreType.DMA((2,))],
        compiler_params=_params(("arbitrary",)),
        name="moe_combine",
    )(dest_blocks, dest_blocks, x2, gates, yb, ln_g, ln_b)


def _moe(x2, w_router, wg, wu, wd, ln_g, ln_b, tok=512):
    T, D = x2.shape
    w_router_pad = jnp.pad(w_router.astype(F32), ((0, 0), (0, LANES - N_EXPERTS)))
    meta, gates, counts = _router(x2, w_router_pad)
    counts = counts[0, :N_EXPERTS]
    padded = (counts + MOE_ROWS - 1) // MOE_ROWS * MOE_ROWS
    pends = jnp.cumsum(padded)
    pstarts = pends - padded
    chosen = meta[:, 0:TOP_K, None] == jnp.arange(N_EXPERTS, dtype=jnp.int32)
    dest = jnp.sum(jnp.where(chosen, pstarts, 0), axis=-1) + meta[:, TOP_K:2 * TOP_K]
    dest_blocks = dest.reshape(T // tok, 1, TOP_K * tok)
    n_blk = (T * TOP_K + MOE_ROWS - 1) // MOE_ROWS + N_EXPERTS
    n_rows = n_blk * MOE_ROWS
    blk_start = jnp.arange(n_blk, dtype=jnp.int32) * MOE_ROWS
    blk_exp = jnp.minimum(jnp.sum((blk_start[:, None] >= pends[None, :]).astype(jnp.int32), axis=1), N_EXPERTS - 1)
    n_used = (pends[-1:] // MOE_ROWS).astype(jnp.int32)
    tail = n_used + jnp.arange(N_EXPERTS, dtype=jnp.int32)
    clear_blocks = jnp.concatenate([jnp.maximum(pends // MOE_ROWS - 1, 0), jnp.minimum(tail, n_blk - 1),
                                    counts > 0, tail < n_blk]).astype(jnp.int32)
    xb = _dispatch(x2, dest_blocks, clear_blocks, n_rows, td=tok)
    yb = _experts(xb, blk_exp, n_used, wg, wu, wd)
    return _combine(x2, gates, yb, dest_blocks, ln_g, ln_b, tc=tok)


def _rot_half_cols(w):
    half = w.shape[-1] // 2
    return jnp.concatenate([-w[..., half:], w[..., :half]], axis=-1)


def _pack_w_in(w):
    D = w.shape[0]
    kr = w[:, 384:448]
    z64 = jnp.zeros((D, 64), w.dtype)
    w16 = jnp.concatenate([
        w[:, 448:2496],
        w[:, 2504:3016],
        w[:, 3016:3528],
        w[:, 4040:5064],
        w[:, 0:256],
        w[:, 256:384],
        kr, z64,
        _rot_half_cols(kr), z64,
    ], axis=1)
    w32 = jnp.concatenate([w[:, 3528:4040], w[:, 2496:2504], jnp.zeros((D, LANES - 8), w.dtype)], axis=1)
    return w16.astype(BF16), w32.astype(BF16), w[:, 5064:].astype(BF16)


def _pack_mla(w_uq, w_ukv):
    R = w_uq.shape[0]
    z64 = jnp.zeros((R, 64), w_uq.dtype)
    wq, wqr = [], []
    for h in range(MLA_HEADS):
        o = h * (MLA_NOPE + MLA_ROPE)
        rope_cols = w_uq[:, o + MLA_NOPE:o + MLA_NOPE + MLA_ROPE]
        wq += [w_uq[:, o:o + MLA_NOPE], rope_cols, z64]
        wqr += [_rot_half_cols(rope_cols), z64]
    kv = w_ukv.reshape(w_ukv.shape[0], MLA_HEADS, MLA_NOPE + MLA_V)
    wk = kv[:, :, :MLA_NOPE].reshape(w_ukv.shape[0], MLA_HEADS * MLA_NOPE)
    wv = kv[:, :, MLA_NOPE:].reshape(w_ukv.shape[0], MLA_HEADS * MLA_V)
    return (jnp.concatenate(wq, 1).astype(BF16), jnp.concatenate(wqr, 1).astype(BF16),
            wk.astype(BF16), wv.astype(BF16))


def _lane_row(v, offset=0):
    v = v.astype(F32)
    return jnp.pad(v, (offset, LANES - offset - v.shape[0]))[None, :]


def kernel(x, positions, w_in, mla_q_norm, mla_w_uq, mla_kv_norm, mla_w_ukv, gdn_conv, gdn_a_log, gdn_dt_bias, gdn_norm, pool_w, pool_scale, hgrn_lb_logits, hgrn_norm, w_branch, w_out, ln_mix_g, ln_mix_b, ffn_w_gate, ffn_w_up, ffn_w_down, moe_router, moe_w_gate, moe_w_up, moe_w_down, ln_ffn_g, ln_ffn_b):
    B, S, D = x.shape
    T = B * S
    x2 = x.reshape(T, D)

    half = MLA_ROPE // 2
    inv = ROPE_THETA ** (-jnp.arange(half, dtype=F32) / half)
    inv_row = jnp.concatenate([inv, inv, inv, inv])[None, :]
    off_row = jnp.concatenate([jnp.zeros((MLA_ROPE,), F32), jnp.full((LANES - MLA_ROPE,), -0.5 * math.pi, F32)])[None, :]
    cos_t, sin_t = _rope_tables(positions.reshape(T, 1).astype(jnp.int32), inv_row, off_row)

    p_lb = jax.nn.softmax(hgrn_lb_logits.astype(F32), axis=0)
    lower_bounds = jnp.cumsum(p_lb, axis=0) - p_lb[0]
    row2 = lambda v: v.astype(F32)[None, :]
    tt = jnp.arange(SPAN)
    conv_shift = jnp.concatenate([(tt[None, :] == tt[:, None] - kk) for kk in range(GDN_CONV)], axis=1).astype(BF16)

    for l in range(DEPTH):
        w16, w32, w_gate = _pack_w_in(w_in[l])
        proj16, proj32 = _inproj(x2, w16, w32)
        wq, wqr, wk, wv = _pack_mla(mla_w_uq[l], mla_w_ukv[l])
        y_a = _mla(proj16, cos_t, sin_t, row2(mla_q_norm[l]), row2(mla_kv_norm[l]), wq, wqr, wk, wv, B, S)
        proj16_3 = proj16.reshape(B, S, PK16_WIDTH)
        proj32_3 = proj32.reshape(B, S, PK32_WIDTH)
        y_b, y_d = _recurrent(proj16_3, proj32_3, gdn_conv[l].astype(F32), conv_shift, _lane_row(gdn_a_log[l]),
                              _lane_row(gdn_dt_bias[l]), row2(gdn_norm[l]), row2(lower_bounds[l]), row2(hgrn_norm[l]))
        y_b = y_b.reshape(T, GDN_HEADS * GDN_DV)
        y_d = y_d.reshape(T, HGRN_HEADS * HGRN_DV)
        x2 = _merge(x2, y_a, y_b, proj16, y_d, pool_w[l].astype(BF16), row2(pool_scale[l]), w_gate,
                    w_branch[l].astype(BF16), w_out[l].astype(BF16), row2(ln_mix_g[l]), row2(ln_mix_b[l]), S)
        j = l // 2
        if l % 2 == 0:
            x2 = _ffn(x2, ffn_w_gate[j].astype(BF16), ffn_w_up[j].astype(BF16), ffn_w_down[j].astype(BF16),
                      row2(ln_ffn_g[l]), row2(ln_ffn_b[l]))
        else:
            x2 = _moe(x2, moe_router[j], moe_w_gate[j].astype(BF16), moe_w_up[j].astype(BF16),
                      moe_w_down[j].astype(BF16), row2(ln_ffn_g[l]), row2(ln_ffn_b[l]))
    return x2.reshape(B, S, D)
```

```python
import functools
import math

import jax
import jax.numpy as jnp
from jax import lax
from jax.experimental import pallas as pl
from jax.experimental.pallas import tpu as pltpu

F32 = jnp.float32
BF16 = jnp.bfloat16

D_MODEL = 1024
DEPTH = 2
MLA_HEADS = 4
MLA_Q_RANK = 256
MLA_KV_RANK = 128
MLA_NOPE = 128
MLA_ROPE = 64
MLA_V = 128
ROPE_THETA = 10000.0
MASK_VALUE = -1e30
GDN_HEADS = 4
GDN_DK = 128
GDN_DV = 128
GDN_CONV = 4
GDN_CHUNK = 64
POOL_WINDOWS = (2, 4, 8, 16)
POOL_GROUP = 128
HGRN_HEADS = 4
HGRN_DK = 128
HGRN_DV = 128
N_BRANCH = 4
BRANCH_WIDTH = 512
D_FF = 2816
N_EXPERTS = 8
TOP_K = 2
D_FF_EXPERT = 3584
DEEPNORM_ALPHA = (2 * DEPTH) ** 0.25
LN_EPS = 1e-5
RMS_EPS = 1e-6
LOG2E = math.log2(math.e)

LANES = 128
MLA_HEAD_PAD = 256
SPAN = 256
HALF = SPAN // 2
MOE_ROWS = 512
VMEM_LIMIT = 56 * 1024 * 1024

PK_GQ, PK_GK, PK_GV, PK_GZ, PK_PU, PK_HQ, PK_HI, PK_HG = range(8)
PK_CQ = 16
PK_CKV = 34
PK_KRA = 35
PK_KRB = 36
PK16_WIDTH = 37 * LANES
PK32_HF = 0
PK32_AB = 4
PK32_WIDTH = 5 * LANES


def _params(sem, vmem=VMEM_LIMIT):
    return pltpu.CompilerParams(dimension_semantics=sem, vmem_limit_bytes=vmem)


def _dot(a, b):
    return jnp.dot(a, b, preferred_element_type=F32)


def _dot_nt(a, b):
    return lax.dot_general(a, b, (((1,), (1,)), ((), ())), preferred_element_type=F32)


def _dot_tn(a, b):
    return lax.dot_general(a, b, (((0,), (0,)), ((), ())), preferred_element_type=F32)


def _split3(x):
    hi = x.astype(BF16)
    r1 = x - hi.astype(F32)
    mid = r1.astype(BF16)
    lo = (r1 - mid.astype(F32)).astype(BF16)
    return hi, mid, lo


def _exact_left_dot(m01, x):
    hi, mid, lo = _split3(x)
    return _dot(m01, hi) + _dot(m01, mid) + _dot(m01, lo)


def _sigmoid(x):
    return 1.0 / (1.0 + jnp.exp(-x))


def _silu(x):
    return x * _sigmoid(x)


def _softplus(x):
    return jnp.maximum(x, 0.0) + jnp.log(1.0 + jnp.exp(-jnp.abs(x)))


def _layer_norm(x, g, b):
    mu = jnp.mean(x, -1, keepdims=True)
    xc = x - mu
    var = jnp.mean(xc * xc, -1, keepdims=True)
    return xc * lax.rsqrt(var + LN_EPS) * g + b


def _rms(x, g):
    return x * lax.rsqrt(jnp.mean(x * x, -1, keepdims=True) + RMS_EPS) * g


def _lane_blocks(s):
    return [s[:, c * LANES:(c + 1) * LANES] for c in range(s.shape[1] // LANES)]


def _row_max(s):
    return jnp.max(functools.reduce(jnp.maximum, _lane_blocks(s)), -1, keepdims=True)


def _row_sum(s):
    return jnp.sum(functools.reduce(jnp.add, _lane_blocks(s)), -1, keepdims=True)


def _inproj_kernel(x_ref, w16_ref, w32_ref, o16_ref, o32_ref):
    x16 = x_ref[...].astype(BF16)
    o16_ref[...] = _dot(x16, w16_ref[...]).astype(o16_ref.dtype)
    o32_ref[...] = _dot(x16, w32_ref[...])


def _inproj(x2, w16, w32, tm=512):
    T, D = x2.shape
    full = lambda a: pl.BlockSpec(a.shape, lambda i: (0, 0), pipeline_mode=pl.Buffered(1))
    return pl.pallas_call(
        _inproj_kernel,
        grid=(T // tm,),
        in_specs=[pl.BlockSpec((tm, D), lambda i: (i, 0)), full(w16), full(w32)],
        out_specs=[pl.BlockSpec((tm, w16.shape[1]), lambda i: (i, 0)),
                   pl.BlockSpec((tm, w32.shape[1]), lambda i: (i, 0))],
        out_shape=[jax.ShapeDtypeStruct((T, w16.shape[1]), BF16),
                   jax.ShapeDtypeStruct((T, w32.shape[1]), F32)],
        compiler_params=_params(("parallel",)),
        name="inproj",
    )(x2, w16, w32)


def _rope_table_kernel(pos_ref, inv_ref, off_ref, c_ref, s_ref):
    cs = jnp.cos(pos_ref[...].astype(F32) * inv_ref[...] + off_ref[...])
    c_ref[...] = cs
    s_ref[...] = pltpu.roll(cs, LANES // 2, 1)


def _rope_tables(pos_lanes, inv_row, off_row, tm=1024):
    T = pos_lanes.shape[0]
    return pl.pallas_call(
        _rope_table_kernel,
        grid=(T // tm,),
        in_specs=[pl.BlockSpec((tm, LANES), lambda i: (i, 0)),
                  pl.BlockSpec((1, LANES), lambda i: (0, 0)),
                  pl.BlockSpec((1, LANES), lambda i: (0, 0))],
        out_specs=[pl.BlockSpec((tm, LANES), lambda i: (i, 0))] * 2,
        out_shape=[jax.ShapeDtypeStruct((T, LANES), F32)] * 2,
        compiler_params=_params(("parallel",)),
        name="rope_tables",
    )(pos_lanes, inv_row, off_row)


def _mla_prep_rows(rs, cq_ref, ckv_ref, kra_ref, krb_ref, c_ref, s_ref, gq_ref, gkv_ref,
                   wq_ref, wqr_ref, wk_ref, wv_ref, q_s, k_s, v_s):
    scale = (MLA_NOPE + MLA_ROPE) ** -0.5 * LOG2E
    cos = c_ref[rs, :]
    sin = s_ref[rs, :]
    nq = _rms(cq_ref[rs, :].astype(F32), gq_ref[...]).astype(BF16)
    q_pre = _dot(nq, wq_ref[...])
    q_rot = _dot(nq, wqr_ref[...])
    nkv = _rms(ckv_ref[rs, :].astype(F32), gkv_ref[...]).astype(BF16)
    k_nope = _dot(nkv, wk_ref[...])
    v_s[rs, :] = _dot(nkv, wv_ref[...]).astype(v_s.dtype)
    k_rope = (kra_ref[rs, :].astype(F32) * cos + krb_ref[rs, :].astype(F32) * sin).astype(k_s.dtype)
    for h in range(MLA_HEADS):
        o = h * MLA_HEAD_PAD
        q_s[rs, o:o + LANES] = (q_pre[:, o:o + LANES] * scale).astype(q_s.dtype)
        q_s[rs, o + LANES:o + 2 * LANES] = (
            (q_pre[:, o + LANES:o + 2 * LANES] * cos + q_rot[:, h * LANES:(h + 1) * LANES] * sin) * scale
        ).astype(q_s.dtype)
        k_s[rs, o:o + LANES] = k_nope[:, h * LANES:(h + 1) * LANES].astype(k_s.dtype)
        k_s[rs, o + LANES:o + 2 * LANES] = k_rope


def _attn_tile(q_s, k_s, v_s, o_ref, t, tq, causal):
    r0 = t * tq
    for h in range(MLA_HEADS):
        hs = slice(h * MLA_HEAD_PAD, (h + 1) * MLA_HEAD_PAD)
        vs = slice(h * MLA_V, (h + 1) * MLA_V)
        qh = q_s[r0:r0 + tq, hs]
        s_d = jnp.where(causal, _dot_nt(qh, k_s[r0:r0 + tq, hs]), MASK_VALUE)
        m = _row_max(s_d)
        if t > 0:
            s_o = _dot_nt(qh, k_s[0:r0, hs])
            m = jnp.maximum(m, _row_max(s_o))
        p_d = jnp.exp2(s_d - m)
        l = _row_sum(p_d)
        acc = _dot(p_d.astype(BF16), v_s[r0:r0 + tq, vs])
        if t > 0:
            p_o = jnp.exp2(s_o - m)
            l = l + _row_sum(p_o)
            acc = acc + _dot(p_o.astype(BF16), v_s[0:r0, vs])
        o_ref[:, vs] = (acc / l).astype(o_ref.dtype)


def _mla_kernel(*refs, seq, tq, tprep):
    prep_refs, o_ref, (q_s, k_s, v_s) = refs[:12], refs[12], refs[13:]
    i = pl.program_id(1)

    @pl.when(i == 0)
    def _():
        for c in range(seq // tprep):
            _mla_prep_rows(slice(c * tprep, (c + 1) * tprep), *prep_refs, q_s, k_s, v_s)

    row = lax.broadcasted_iota(jnp.int32, (tq, tq), 0)
    col = lax.broadcasted_iota(jnp.int32, (tq, tq), 1)
    causal = row >= col
    for t in range(seq // tq):
        pl.when(i == t)(functools.partial(_attn_tile, q_s, k_s, v_s, o_ref, t, tq, causal))


def _mla(proj, cos_t, sin_t, g_q, g_kv, wq, wqr, wk, wv, B, S, tq=256, tprep=512):
    T = proj.shape[0]
    nq = S // tq
    HW = MLA_HEADS * MLA_HEAD_PAD
    HV = MLA_HEADS * MLA_V
    seq = lambda w, j: pl.BlockSpec((S, w), lambda b, i, j=j: (b, j))
    full = lambda a: pl.BlockSpec(a.shape, lambda b, i: (0,) * a.ndim)
    return pl.pallas_call(
        functools.partial(_mla_kernel, seq=S, tq=tq, tprep=min(tprep, S)),
        grid=(B, nq),
        in_specs=[seq(256, PK_CQ), seq(128, PK_CKV), seq(128, PK_KRA), seq(128, PK_KRB),
                  seq(128, 0), seq(128, 0), full(g_q), full(g_kv), full(wq), full(wqr), full(wk), full(wv)],
        out_specs=pl.BlockSpec((tq, HV), lambda b, i: (b * nq + i, 0)),
        out_shape=jax.ShapeDtypeStruct((T, HV), BF16),
        scratch_shapes=[pltpu.VMEM((S, HW), BF16), pltpu.VMEM((S, HW), BF16), pltpu.VMEM((S, HV), BF16)],
        compiler_params=_params(("parallel", "arbitrary")),
        name="mla",
    )(proj, proj, proj, proj, cos_t, sin_t, g_q, g_kv, wq, wqr, wk, wv)


def _gdn_body(q_ref, k_ref, v_ref, z_ref, ab_ref, cw_ref, sh_ref, alog_ref, dtb_ref, ng_ref, o_ref,
              carry_ref, state_ref, *, nb):
    si = pl.program_id(1)
    HD = GDN_HEADS * GDN_DK
    NC = SPAN // GDN_CHUNK

    @pl.when(si == 0)
    def _():
        carry_ref[...] = jnp.zeros_like(carry_ref)
        state_ref[...] = jnp.zeros_like(state_ref)

    yield

    def conv_silu(x_ref, b, idx):
        x16 = x_ref[b]
        w16 = cw_ref[:, idx * HD:(idx + 1) * HD].astype(BF16)
        taps = [w16[GDN_CONV - 1 - kk:GDN_CONV - kk] for kk in range(GDN_CONV)]
        y = _dot(sh_ref[...], jnp.concatenate([x16 * tap for tap in taps], axis=0))
        r8 = lax.broadcasted_iota(jnp.int32, (8, HD), 0)
        head = y[0:8]
        for kk in range(1, GDN_CONV):
            term = (pltpu.roll(carry_ref[b, idx], kk, 0).astype(BF16) * taps[kk]).astype(F32)
            head = head + jnp.where(r8 < kk, term, 0.0)
        carry_ref[b, idx] = x16[SPAN - 8:SPAN].astype(F32)
        return _silu(jnp.concatenate([head, y[8:]], axis=0))

    row = lax.broadcasted_iota(jnp.int32, (SPAN, SPAN), 0)
    col = lax.broadcasted_iota(jnp.int32, (SPAN, SPAN), 1)
    same_chunk = (row // GDN_CHUNK) == (col // GDN_CHUNK)
    tril = same_chunk & (col <= row)
    stril = same_chunk & (col < row)
    eye = (row == col).astype(F32)
    tril16 = tril.astype(BF16)

    units = []
    for b in range(nb):
        q = conv_silu(q_ref, b, 0)
        k = conv_silu(k_ref, b, 1)
        v = conv_silu(v_ref, b, 2)
        ab = ab_ref[b]
        g_full = -jnp.exp(alog_ref[...]) * _softplus(ab + dtb_ref[...])
        beta_full = _sigmoid(ab)
        gc_full = _exact_left_dot(tril16, g_full)
        gc_t = gc_full.T
        for h in range(GDN_HEADS):
            sl = slice(h * GDN_DK, (h + 1) * GDN_DK)
            qh = q[:, sl]
            kh = k[:, sl]
            qh = qh * lax.rsqrt(jnp.sum(qh * qh, -1, keepdims=True) + RMS_EPS) * (GDN_DK ** -0.5)
            kh = kh * lax.rsqrt(jnp.sum(kh * kh, -1, keepdims=True) + RMS_EPS)
            gc = gc_full[:, h:h + 1]
            beta = beta_full[:, GDN_HEADS + h:GDN_HEADS + h + 1]
            decay = jnp.where(tril, jnp.exp(gc - gc_t[h:h + 1, :]), 0.0)
            k16 = kh.astype(BF16)
            a = jnp.where(stril, beta * _dot_nt(k16, k16) * decay, 0.0)
            qk16 = (_dot_nt(qh.astype(BF16), k16) * decay).astype(BF16)
            egc = jnp.exp(gc)
            units.append(dict(
                b=b, h=h, sl=sl, kh=kh, gc=gc, qk16=qk16, a=a,
                rhs=jnp.concatenate([v[:, sl] * beta, kh * (beta * egc)], axis=1).astype(BF16),
                qg=(qh * egc).astype(BF16)))
        yield

    xs = [eye - u["a"] for u in units]
    ps = [u["a"].astype(BF16) for u in units]
    for _ in range(5):
        ps = [_dot(p, p).astype(BF16) for p in ps]
        xs = [x + _dot(x.astype(BF16), p) for x, p in zip(xs, ps)]
        yield
    uws = [_dot(x.astype(BF16), u["rhs"]) for x, u in zip(xs, units)]
    yield

    states = [state_ref[u["b"], u["h"]] for u in units]
    o_state = [[] for _ in units]
    v_new = [[] for _ in units]
    for c in range(NC):
        cs = slice(c * GDN_CHUNK, (c + 1) * GDN_CHUNK)
        s16 = [s.astype(BF16) for s in states]
        vns = [uw[cs, :GDN_DV] - _dot(uw[cs, GDN_DV:].astype(BF16), s) for uw, s in zip(uws, s16)]
        for i, u in enumerate(units):
            o_state[i].append(_dot(u["qg"][cs], s16[i]))
            v_new[i].append(vns[i])
            g_last = u["gc"][(c + 1) * GDN_CHUNK - 1:(c + 1) * GDN_CHUNK, :]
            kd = (u["kh"][cs] * jnp.exp(g_last - u["gc"][cs])).astype(BF16)
            states[i] = states[i] * jnp.exp(g_last) + _dot_tn(kd, vns[i].astype(BF16))
        yield
    for i, u in enumerate(units):
        state_ref[u["b"], u["h"]] = states[i]
        o = jnp.concatenate(o_state[i], axis=0) + _dot(u["qk16"], jnp.concatenate(v_new[i], axis=0).astype(BF16))
        o = _rms(o, ng_ref[...]) * _silu(z_ref[u["b"], :, u["sl"]].astype(F32))
        o_ref[u["b"], :, u["sl"]] = o.astype(o_ref.dtype)
        if i % GDN_HEADS == GDN_HEADS - 1:
            yield


def _gdn_kernel(*refs, nb):
    for _ in _gdn_body(*refs, nb=nb):
        pass


def _gdn(proj16, proj32, conv_w, shift, alog_row, dtb_row, norm_g, nb=2):
    B, S, _ = proj16.shape
    HD = GDN_HEADS * GDN_DK
    blk = lambda w, j: pl.BlockSpec((nb, SPAN, w), lambda b, s, j=j: (b, s, j))
    full = lambda a: pl.BlockSpec(a.shape, lambda b, s: (0,) * a.ndim)
    return pl.pallas_call(
        functools.partial(_gdn_kernel, nb=nb),
        grid=(B // nb, S // SPAN),
        in_specs=[blk(HD, PK_GQ), blk(HD, PK_GK), blk(HD, PK_GV), blk(HD, PK_GZ), blk(LANES, PK32_AB),
                  full(conv_w), full(shift), full(alog_row), full(dtb_row), full(norm_g)],
        out_specs=blk(HD, 0),
        out_shape=jax.ShapeDtypeStruct((B, S, HD), BF16),
        scratch_shapes=[pltpu.VMEM((nb, 3, 8, HD), F32), pltpu.VMEM((nb, GDN_HEADS, GDN_DK, GDN_DV), F32)],
        compiler_params=_params(("parallel", "arbitrary")),
        name="gdn",
    )(proj16, proj16, proj16, proj16, proj32, conv_w, shift, alog_row, dtb_row, norm_g)


def _pool_tile(u_ref, w_ref, sc_ref, carry_ref, si, tp):
    G = POOL_GROUP

    @pl.when(si == 0)
    def _():
        carry_ref[...] = jnp.zeros_like(carry_ref)

    u = u_ref[...].astype(F32)
    ext = jnp.concatenate([carry_ref[...], u], axis=0)
    carry_ref[...] = u[tp - 16:tp]
    t = si * tp + lax.broadcasted_iota(jnp.int32, (tp, 1), 0)
    win_sum = ext
    outs = []
    shift = 1
    for gi, win in enumerate(POOL_WINDOWS):
        while shift < win:
            win_sum = win_sum + pltpu.roll(win_sum, shift, 0)
            shift *= 2
        cnt = jnp.minimum(t + 1, win).astype(F32)
        ug = u[:, gi * G:(gi + 1) * G]
        pooled = win_sum[16:, gi * G:(gi + 1) * G] / cnt - ug
        outs.append(_dot(pooled.astype(BF16), w_ref[gi]))
    return (jnp.concatenate(outs, axis=1) * sc_ref[...]).astype(BF16)


def _mid_broadcast(c, s):
    n = c.shape[0]
    if 2 * s >= 8:
        c3 = c.reshape(n // (2 * s), 2 * s, c.shape[1])
        mid = jnp.broadcast_to(c3[:, s - 1:s, :], c3.shape)
        return mid.reshape(c.shape)
    pos = lax.broadcasted_iota(jnp.int32, c.shape, 0) % (2 * s)
    out = c
    for p in range(2 * s):
        d = p - (s - 1)
        if d != 0:
            out = jnp.where(pos == p, pltpu.roll(c, d % n, 0), out)
    return out


def _hgrn_body(q_ref, f_ref, i_ref, g_ref, lb_ref, ng_ref, o_ref, state_ref, *, nb):
    si = pl.program_id(1)

    @pl.when(si == 0)
    def _():
        state_ref[...] = jnp.zeros_like(state_ref)

    yield

    row = lax.broadcasted_iota(jnp.int32, (SPAN, SPAN), 0)
    col = lax.broadcasted_iota(jnp.int32, (SPAN, SPAN), 1)
    lower16 = (col <= row).astype(BF16)
    heads = [slice(h * HGRN_DK, (h + 1) * HGRN_DK) for h in range(HGRN_HEADS)]
    halves = [slice(0, HALF), slice(HALF, SPAN)]
    ri = lax.broadcasted_iota(jnp.int32, (HALF, HALF), 0)
    ci = lax.broadcasted_iota(jnp.int32, (HALF, HALF), 1)
    rowl = lax.broadcasted_iota(jnp.int32, (SPAN, LANES), 0)
    lb = lb_ref[...]

    seqs = []
    for b in range(nb):
        q = _silu(q_ref[b].astype(F32))
        fp = f_ref[b]
        log_f = jnp.log(lb + (1.0 - lb) * _sigmoid(fp))
        k = (1.0 - lb) * _sigmoid(-fp)
        c = _exact_left_dot(lower16, log_f)
        q16 = q.astype(BF16)
        k16 = k.astype(BF16)
        att = [[jnp.where(ri == ci, _dot_nt(q16[r, sl], k16[r, sl]), 0.0) for sl in heads] for r in halves]
        seqs.append(dict(q=q, k=k, c=c, q16=q16, k16=k16, att=att, cross=None))
        yield

    s = 1
    while s < SPAN:
        sgn = jnp.where(((rowl // s) % 2) == 1, LOG2E, -LOG2E)
        if s < HALF:
            bi = ri // s
            bj = ci // s
            region = ((bi - bj) * 2 + (bj & 1)) == 2
        for sq in seqs:
            diff = sq["c"] - _mid_broadcast(sq["c"], s)
            w16 = jnp.concatenate([jnp.exp2(diff[:, sl] * sgn) for sl in heads], axis=1).astype(BF16)
            ql = sq["q16"] * w16
            kl = sq["k16"] * w16
            if s < HALF:
                for d, r in enumerate(halves):
                    for h, sl in enumerate(heads):
                        sq["att"][d][h] = jnp.where(region, _dot_nt(ql[r, sl], kl[r, sl]), sq["att"][d][h])
            else:
                sq["cross"] = [_dot_nt(ql[halves[1], sl], kl[halves[0], sl]) for sl in heads]
        s *= 2
        yield

    for b, sq in enumerate(seqs):
        c = sq["c"]
        c_last = c[SPAN - 1:SPAN, :]
        v16 = i_ref[b]
        qe = (sq["q"] * jnp.exp(c)).astype(BF16)
        ke = (sq["k"] * jnp.exp(c_last - c)).astype(BF16)
        e_last = jnp.exp(c_last)
        for h, sl in enumerate(heads):
            st = state_ref[b, h]
            v_lo = v16[halves[0], sl]
            v_hi = v16[halves[1], sl]
            o_lo = _dot(sq["att"][0][h].astype(BF16), v_lo)
            o_hi = _dot(sq["cross"][h].astype(BF16), v_lo) + _dot(sq["att"][1][h].astype(BF16), v_hi)
            o = jnp.concatenate([o_lo, o_hi], axis=0) + _dot_nt(qe[:, sl], st.astype(BF16))
            state_ref[b, h] = st * e_last[:, sl] + _dot_tn(v16[:, sl], ke[:, sl])
            o = _rms(o, ng_ref[...]) * _sigmoid(g_ref[b, :, sl].astype(F32))
            o_ref[b, :, sl] = o.astype(o_ref.dtype)
        yield


def _hgrn_kernel(*refs, nb):
    for _ in _hgrn_body(*refs, nb=nb):
        pass


def _recurrent_kernel(*refs, nb):
    g_in, h_in, (g_out, h_out), g_scr, h_scr = refs[:10], refs[10:16], refs[16:18], refs[18:20], refs[20:]
    bodies = [_gdn_body(*g_in, g_out, *g_scr, nb=nb), _hgrn_body(*h_in, h_out, *h_scr, nb=nb)]
    while bodies:
        for body in list(bodies):
            try:
                next(body)
            except StopIteration:
                bodies.remove(body)


def _recurrent(proj16, proj32, conv_w, shift, alog_row, dtb_row, gdn_norm_g, lb_row, hgrn_norm_g, nb=2):
    B, S, _ = proj16.shape
    HD = GDN_HEADS * GDN_DK
    blk = lambda w, j: pl.BlockSpec((nb, SPAN, w), lambda b, s, j=j: (b, s, j))
    full = lambda a: pl.BlockSpec(a.shape, lambda b, s: (0,) * a.ndim)
    out = jax.ShapeDtypeStruct((B, S, HD), BF16)
    return pl.pallas_call(
        functools.partial(_recurrent_kernel, nb=nb),
        grid=(B // nb, S // SPAN),
        in_specs=[blk(HD, PK_GQ), blk(HD, PK_GK), blk(HD, PK_GV), blk(HD, PK_GZ), blk(LANES, PK32_AB),
                  full(conv_w), full(shift), full(alog_row), full(dtb_row), full(gdn_norm_g),
                  blk(HD, PK_HQ), blk(HD, PK32_HF), blk(HD, PK_HI), blk(HD, PK_HG), full(lb_row), full(hgrn_norm_g)],
        out_specs=[blk(HD, 0), blk(HD, 0)],
        out_shape=[out, out],
        scratch_shapes=[pltpu.VMEM((nb, 3, 8, HD), F32), pltpu.VMEM((nb, GDN_HEADS, GDN_DK, GDN_DV), F32),
                        pltpu.VMEM((nb, HGRN_HEADS, HGRN_DV, HGRN_DK), F32)],
        compiler_params=_params(("parallel", "arbitrary")),
        name="gdn_hgrn2",
    )(proj16, proj16, proj16, proj16, proj32, conv_w, shift, alog_row, dtb_row, gdn_norm_g,
      proj16, proj32, proj16, proj16, lb_row, hgrn_norm_g)


def _hgrn(proj16, proj32, lb_row, norm_g, nb=2):
    B, S, _ = proj16.shape
    HD = HGRN_HEADS * HGRN_DK
    blk = lambda j: pl.BlockSpec((nb, SPAN, HD), lambda b, s, j=j: (b, s, j))
    full = lambda a: pl.BlockSpec(a.shape, lambda b, s: (0,) * a.ndim)
    return pl.pallas_call(
        functools.partial(_hgrn_kernel, nb=nb),
        grid=(B // nb, S // SPAN),
        in_specs=[blk(PK_HQ), blk(PK32_HF), blk(PK_HI), blk(PK_HG), full(lb_row), full(norm_g)],
        out_specs=blk(0),
        out_shape=jax.ShapeDtypeStruct((B, S, HD), BF16),
        scratch_shapes=[pltpu.VMEM((nb, HGRN_HEADS, HGRN_DV, HGRN_DK), F32)],
        compiler_params=_params(("parallel", "arbitrary")),
        name="hgrn2",
    )(proj16, proj32, proj16, proj16, lb_row, norm_g)


def _merge_kernel(x_ref, ya_ref, yb_ref, u_ref, yd_ref, pw_ref, psc_ref, wg_ref, wb_ref, wo_ref, g_ref, b_ref,
                  o_ref, carry_ref, *, tiles_per_seq, tm):
    yc = _pool_tile(u_ref, pw_ref, psc_ref, carry_ref, pl.program_id(0) % tiles_per_seq, tm)
    x = x_ref[...]
    x16 = x.astype(BF16)
    merged = None
    for m, y in enumerate((ya_ref[...], yb_ref[...], yc, yd_ref[...])):
        gate = _sigmoid(_dot(x16, wg_ref[:, m * D_MODEL:(m + 1) * D_MODEL]))
        term = gate * _dot(y, wb_ref[m])
        merged = term if merged is None else merged + term
    h = DEEPNORM_ALPHA * x + _dot(merged.astype(BF16), wo_ref[...])
    o_ref[...] = _layer_norm(h, g_ref[...], b_ref[...])


def _merge(x2, ya, yb, proj16, yd, pool_w, pool_scale, w_gate, w_branch, w_out, ln_g, ln_b, S, tm=512):
    T, D = x2.shape
    row = lambda w, j=0: pl.BlockSpec((tm, w), lambda i, j=j: (i, j))
    full = lambda a: pl.BlockSpec(a.shape, lambda i: (0,) * a.ndim, pipeline_mode=pl.Buffered(1))
    return pl.pallas_call(
        functools.partial(_merge_kernel, tiles_per_seq=S // tm, tm=tm),
        grid=(T // tm,),
        in_specs=[row(D), row(BRANCH_WIDTH), row(BRANCH_WIDTH), row(BRANCH_WIDTH, PK_PU), row(BRANCH_WIDTH),
                  full(pool_w), full(pool_scale), full(w_gate), full(w_branch), full(w_out), full(ln_g), full(ln_b)],
        out_specs=row(D),
        out_shape=jax.ShapeDtypeStruct((T, D), F32),
        scratch_shapes=[pltpu.VMEM((16, len(POOL_WINDOWS) * POOL_GROUP), F32)],
        compiler_params=_params(("arbitrary",)),
        name="merge",
    )(x2, ya, yb, proj16, yd, pool_w, pool_scale, w_gate, w_branch, w_out, ln_g, ln_b)


def _ffn_kernel(x_ref, wg_ref, wu_ref, wd_ref, g_ref, b_ref, o_ref):
    x = x_ref[...]
    x16 = x.astype(BF16)
    h = _silu(_dot(x16, wg_ref[...])) * _dot(x16, wu_ref[...])
    y = DEEPNORM_ALPHA * x + _dot(h.astype(BF16), wd_ref[...])
    o_ref[...] = _layer_norm(y, g_ref[...], b_ref[...])


def _ffn(x2, wg, wu, wd, ln_g, ln_b, tm=512):
    T, D = x2.shape
    row = pl.BlockSpec((tm, D), lambda i: (i, 0))
    full = lambda a: pl.BlockSpec(a.shape, lambda i: (0,) * a.ndim, pipeline_mode=pl.Buffered(1))
    return pl.pallas_call(
        _ffn_kernel,
        grid=(T // tm,),
        in_specs=[row, full(wg), full(wu), full(wd), full(ln_g), full(ln_b)],
        out_specs=row,
        out_shape=jax.ShapeDtypeStruct((T, D), F32),
        compiler_params=_params(("parallel",)),
        name="ffn_dense",
    )(x2, wg, wu, wd, ln_g, ln_b)


def _router_kernel(x_ref, w_ref, meta_ref, gate_ref, cnt_ref, base_ref, *, tm):
    i = pl.program_id(0)

    @pl.when(i == 0)
    def _():
        base_ref[...] = jnp.zeros_like(base_ref)

    xh, xm, _ = _split3(x_ref[...])
    wh, wm, _ = _split3(w_ref[...])
    logits = _dot(xh, wh) + _dot(xh, wm) + _dot(xm, wh)
    lane = lax.broadcasted_iota(jnp.int32, (tm, LANES), 1)
    neg = jnp.float32(-jnp.inf)
    logits = jnp.where(lane < N_EXPERTS, logits, neg)
    v1 = jnp.max(logits, -1, keepdims=True)
    e1 = jnp.min(jnp.where(logits == v1, lane, LANES), -1, keepdims=True)
    rest = jnp.where(lane == e1, neg, logits)
    v2 = jnp.max(rest, -1, keepdims=True)
    e2 = jnp.min(jnp.where(rest == v2, lane, LANES), -1, keepdims=True)
    ex = jnp.exp(v2 - v1)
    w1 = 1.0 / (1.0 + ex)
    w2 = ex / (1.0 + ex)
    onehot = ((lane == e1) | (lane == e2)).astype(F32)
    r = lax.broadcasted_iota(jnp.int32, (tm, tm), 0)
    cc = lax.broadcasted_iota(jnp.int32, (tm, tm), 1)
    before = _dot((cc < r).astype(BF16), onehot.astype(BF16)) + base_ref[0:1, :]
    rank1 = jnp.sum(jnp.where(lane == e1, before, 0.0), -1, keepdims=True)
    rank2 = jnp.sum(jnp.where(lane == e2, before, 0.0), -1, keepdims=True)
    total = base_ref[0:1, :] + jnp.sum(onehot, 0, keepdims=True)
    base_ref[...] = jnp.broadcast_to(total, base_ref.shape)
    cnt_ref[...] = jnp.broadcast_to(total, cnt_ref.shape).astype(jnp.int32)
    meta = jnp.where(lane == 0, e1, jnp.where(lane == 1, e2, jnp.where(
        lane == 2, rank1.astype(jnp.int32), jnp.where(lane == 3, rank2.astype(jnp.int32), 0))))
    meta_ref[...] = meta[:, 0:meta_ref.shape[1]]
    gate_ref[...] = jnp.where(lane == 0, w1, jnp.where(lane == 1, w2, 0.0))


def _router(x2, w_router_pad, tm=512):
    T, D = x2.shape
    return pl.pallas_call(
        functools.partial(_router_kernel, tm=tm),
        grid=(T // tm,),
        in_specs=[pl.BlockSpec((tm, D), lambda i: (i, 0)),
                  pl.BlockSpec(w_router_pad.shape, lambda i: (0, 0))],
        out_specs=[pl.BlockSpec((tm, 8), lambda i: (i, 0)),
                   pl.BlockSpec((tm, LANES), lambda i: (i, 0)),
                   pl.BlockSpec((8, LANES), lambda i: (0, 0))],
        out_shape=[jax.ShapeDtypeStruct((T, 8), jnp.int32),
                   jax.ShapeDtypeStruct((T, LANES), F32),
                   jax.ShapeDtypeStruct((8, LANES), jnp.int32)],
        scratch_shapes=[pltpu.VMEM((8, LANES), F32)],
        compiler_params=_params(("arbitrary",)),
        name="moe_router",
    )(x2, w_router_pad)


def _dispatch_kernel(zb_ref, dest_ref, x_ref, xb_ref, zero_ref, sem, zsem, *, td):
    @pl.when(pl.program_id(0) == 0)
    def _():
        zero_ref[...] = jnp.zeros_like(zero_ref)
        nz = 2 * N_EXPERTS
        clear = [pltpu.make_async_copy(zero_ref, xb_ref.at[pl.ds(zb_ref[z] * MOE_ROWS, MOE_ROWS)], zsem)
                 for z in range(nz)]
        for z, cp in enumerate(clear):
            pl.when(zb_ref[nz + z] > 0)(cp.start)
        for z, cp in enumerate(clear):
            pl.when(zb_ref[nz + z] > 0)(cp.wait)

    def start(g, c):
        for u in range(8):
            for slot in range(TOP_K):
                pltpu.make_async_copy(x_ref.at[g, pl.ds(u, 1)],
                                      xb_ref.at[pl.ds(dest_ref[0, 0, (g * 8 + u) * TOP_K + slot], 1)],
                                      sem).start(priority=slot)
        return c

    lax.fori_loop(0, td // 8, start, 0)
    pltpu.make_async_copy(xb_ref.at[pl.ds(0, TOP_K * td)], xb_ref.at[pl.ds(0, TOP_K * td)], sem).wait()


def _dispatch(x2, dest_blocks, clear_blocks, n_rows, td=256):
    T, D = x2.shape
    grid_spec = pltpu.PrefetchScalarGridSpec(
        num_scalar_prefetch=1,
        grid=(T // td,),
        in_specs=[pl.BlockSpec((1, 1, TOP_K * td), lambda i, zb: (i, 0, 0), memory_space=pltpu.SMEM),
                  pl.BlockSpec((td // 8, 8, D), lambda i, zb: (i, 0, 0))],
        out_specs=pl.BlockSpec(memory_space=pl.ANY),
        scratch_shapes=[pltpu.VMEM((MOE_ROWS, D), F32), pltpu.SemaphoreType.DMA(()), pltpu.SemaphoreType.DMA(())],
    )
    return pl.pallas_call(
        functools.partial(_dispatch_kernel, td=td),
        grid_spec=grid_spec,
        out_shape=jax.ShapeDtypeStruct((n_rows, D), F32),
        compiler_params=_params(("arbitrary",)),
        name="moe_dispatch",
    )(clear_blocks, dest_blocks, x2.reshape(T // 8, 8, D))


def _expert_kernel(blk_exp_ref, n_used_ref, x_ref, wg_ref, wu_ref, wd_ref, o_ref, acc_ref):
    i = pl.program_id(0)
    j = pl.program_id(1)

    @pl.when(i < n_used_ref[0])
    def _():
        x16 = x_ref[...].astype(BF16)
        y = None
        for c in range(wg_ref.shape[2] // 512):
            sl = slice(c * 512, (c + 1) * 512)
            h = _silu(_dot(x16, wg_ref[0, :, sl])) * _dot(x16, wu_ref[0, :, sl])
            part = _dot(h.astype(BF16), wd_ref[0, sl, :])
            y = part if y is None else y + part

        @pl.when(j == 0)
        def _():
            acc_ref[...] = y

        @pl.when(j > 0)
        def _():
            acc_ref[...] += y

        @pl.when(j == pl.num_programs(1) - 1)
        def _():
            o_ref[...] = acc_ref[...]

    @pl.when(i >= n_used_ref[0])
    def _():
        o_ref[...] = jnp.zeros_like(o_ref)


def _experts(xb, blk_exp, n_used, wg, wu, wd, tf=3584):
    n_rows, D = xb.shape
    n_blk = n_rows // MOE_ROWS
    nf = D_FF_EXPERT // tf

    def rows(i, j, be, nu):
        return (jnp.minimum(i, nu[0] - 1), 0)

    def chunk(i, j, nu):
        ic = jnp.minimum(i, nu[0] - 1)
        return jnp.where(ic % 2 == 0, jnp.where(i < nu[0], j, nf - 1), jnp.where(i < nu[0], nf - 1 - j, 0))

    grid_spec = pltpu.PrefetchScalarGridSpec(
        num_scalar_prefetch=2,
        grid=(n_blk, nf),
        in_specs=[pl.BlockSpec((MOE_ROWS, D), rows),
                  pl.BlockSpec((1, D, tf), lambda i, j, be, nu: (be[jnp.minimum(i, nu[0] - 1)], 0, chunk(i, j, nu))),
                  pl.BlockSpec((1, D, tf), lambda i, j, be, nu: (be[jnp.minimum(i, nu[0] - 1)], 0, chunk(i, j, nu))),
                  pl.BlockSpec((1, tf, D), lambda i, j, be, nu: (be[jnp.minimum(i, nu[0] - 1)], chunk(i, j, nu), 0))],
        out_specs=pl.BlockSpec((MOE_ROWS, D), lambda i, j, be, nu: (i, 0)),
        scratch_shapes=[pltpu.VMEM((MOE_ROWS, D), F32)],
    )
    return pl.pallas_call(
        _expert_kernel,
        grid_spec=grid_spec,
        out_shape=jax.ShapeDtypeStruct((n_rows, D), F32),
        compiler_params=_params(("arbitrary", "arbitrary")),
        name="moe_experts",
    )(blk_exp, n_used, xb, wg, wu, wd)


def _combine_kernel(dest_ref, dnext_ref, x_ref, gate_ref, yb_ref, g_ref, b_ref, o_ref, buf_ref, sem, *, tc):
    i = pl.program_id(0)
    n = pl.num_programs(0)

    def issue(d_ref, slot):
        def start(g, c):
            for u in range(8):
                for kk in range(TOP_K):
                    pltpu.make_async_copy(yb_ref.at[pl.ds(d_ref[0, 0, (g * 8 + u) * TOP_K + kk], 1)],
                                          buf_ref.at[slot, kk, g, pl.ds(u, 1)], sem.at[slot]).start(priority=kk)
            return c
        lax.fori_loop(0, tc // 8, start, 0)

    @pl.when(i == 0)
    def _():
        issue(dest_ref, 0)

    @pl.when(i + 1 < n)
    def _():
        issue(dnext_ref, (i + 1) % 2)

    slot = i % 2
    pltpu.make_async_copy(buf_ref.at[slot], buf_ref.at[slot], sem.at[slot]).wait()
    gate = gate_ref[...]
    D = x_ref.shape[1]
    ffn = buf_ref[slot, 0].reshape(tc, D) * gate[:, 0:1] + buf_ref[slot, 1].reshape(tc, D) * gate[:, 1:2]
    o_ref[...] = _layer_norm(DEEPNORM_ALPHA * x_ref[...] + ffn, g_ref[...], b_ref[...])


def _combine(x2, gates, yb, dest_blocks, ln_g, ln_b, tc=256):
    T, D = x2.shape
    n = T // tc
    full = lambda a: pl.BlockSpec(a.shape, lambda i: (0,) * a.ndim)
    smem = lambda imap: pl.BlockSpec((1, 1, TOP_K * tc), imap, memory_space=pltpu.SMEM)
    return pl.pallas_call(
        functools.partial(_combine_kernel, tc=tc),
        grid=(n,),
        in_specs=[smem(lambda i: (i, 0, 0)), smem(lambda i: (jnp.minimum(i + 1, n - 1), 0, 0)),
                  pl.BlockSpec((tc, D), lambda i: (i, 0)),
                  pl.BlockSpec((tc, LANES), lambda i: (i, 0)),
                  pl.BlockSpec(memory_space=pl.ANY), full(ln_g), full(ln_b)],
        out_specs=pl.BlockSpec((tc, D), lambda i: (i, 0)),
        out_shape=jax.ShapeDtypeStruct((T, D), F32),
        scratch_shapes=[pltpu.VMEM((2, TOP_K, tc // 8, 8, D), F32), pltpu.SemaphoreType.DMA((2,))],
        compiler_params=_params(("arbitrary",)),
        name="moe_combine",
    )(dest_blocks, dest_blocks, x2, gates, yb, ln_g, ln_b)


def _moe(x2, w_router, wg, wu, wd, ln_g, ln_b, tok=512):
    T, D = x2.shape
    w_router_pad = jnp.pad(w_router.astype(F32), ((0, 0), (0, LANES - N_EXPERTS)))
    meta, gates, counts = _router(x2, w_router_pad)
    counts = counts[0, :N_EXPERTS]
    padded = (counts + MOE_ROWS - 1) // MOE_ROWS * MOE_ROWS
    pends = jnp.cumsum(padded)
    pstarts = pends - padded
    dest = pstarts[meta[:, 0:2]] + meta[:, 2:4]
    dest_blocks = dest.reshape(T // tok, 1, TOP_K * tok)
    n_blk = (T * TOP_K + MOE_ROWS - 1) // MOE_ROWS + N_EXPERTS
    n_rows = n_blk * MOE_ROWS
    blk_start = jnp.arange(n_blk, dtype=jnp.int32) * MOE_ROWS
    blk_exp = jnp.minimum(jnp.sum((blk_start[:, None] >= pends[None, :]).astype(jnp.int32), axis=1), N_EXPERTS - 1)
    n_used = (pends[-1:] // MOE_ROWS).astype(jnp.int32)
    tail = n_used + jnp.arange(N_EXPERTS, dtype=jnp.int32)
    clear_blocks = jnp.concatenate([jnp.maximum(pends // MOE_ROWS - 1, 0), jnp.minimum(tail, n_blk - 1),
                                    counts > 0, tail < n_blk]).astype(jnp.int32)
    xb = _dispatch(x2, dest_blocks, clear_blocks, n_rows, td=tok)
    yb = _experts(xb, blk_exp, n_used, wg, wu, wd)
    return _combine(x2, gates, yb, dest_blocks, ln_g, ln_b, tc=tok)


def _rot_half_cols(w):
    half = w.shape[-1] // 2
    return jnp.concatenate([-w[..., half:], w[..., :half]], axis=-1)


def _pack_w_in(w):
    D = w.shape[0]
    kr = w[:, 384:448]
    z64 = jnp.zeros((D, 64), w.dtype)
    w16 = jnp.concatenate([
        w[:, 448:2496],
        w[:, 2504:3016],
        w[:, 3016:3528],
        w[:, 4040:5064],
        w[:, 0:256],
        w[:, 256:384],
        kr, z64,
        _rot_half_cols(kr), z64,
    ], axis=1)
    w32 = jnp.concatenate([w[:, 3528:4040], w[:, 2496:2504], jnp.zeros((D, LANES - 8), w.dtype)], axis=1)
    return w16.astype(BF16), w32.astype(BF16), w[:, 5064:].astype(BF16)


def _pack_mla(w_uq, w_ukv):
    R = w_uq.shape[0]
    z64 = jnp.zeros((R, 64), w_uq.dtype)
    wq, wqr = [], []
    for h in range(MLA_HEADS):
        o = h * (MLA_NOPE + MLA_ROPE)
        rope_cols = w_uq[:, o + MLA_NOPE:o + MLA_NOPE + MLA_ROPE]
        wq += [w_uq[:, o:o + MLA_NOPE], rope_cols, z64]
        wqr += [_rot_half_cols(rope_cols), z64]
    kv = w_ukv.reshape(w_ukv.shape[0], MLA_HEADS, MLA_NOPE + MLA_V)
    wk = kv[:, :, :MLA_NOPE].reshape(w_ukv.shape[0], MLA_HEADS * MLA_NOPE)
    wv = kv[:, :, MLA_NOPE:].reshape(w_ukv.shape[0], MLA_HEADS * MLA_V)
    return (jnp.concatenate(wq, 1).astype(BF16), jnp.concatenate(wqr, 1).astype(BF16),
            wk.astype(BF16), wv.astype(BF16))


def _lane_row(v, offset=0):
    v = v.astype(F32)
    return jnp.pad(v, (offset, LANES - offset - v.shape[0]))[None, :]


def kernel(x, positions, w_in, mla_q_norm, mla_w_uq, mla_kv_norm, mla_w_ukv, gdn_conv, gdn_a_log, gdn_dt_bias, gdn_norm, pool_w, pool_scale, hgrn_lb_logits, hgrn_norm, w_branch, w_out, ln_mix_g, ln_mix_b, ffn_w_gate, ffn_w_up, ffn_w_down, moe_router, moe_w_gate, moe_w_up, moe_w_down, ln_ffn_g, ln_ffn_b):
    B, S, D = x.shape
    T = B * S
    x2 = x.reshape(T, D)

    half = MLA_ROPE // 2
    inv = ROPE_THETA ** (-jnp.arange(half, dtype=F32) / half)
    inv_row = jnp.concatenate([inv, inv, inv, inv])[None, :]
    off_row = jnp.concatenate([jnp.zeros((MLA_ROPE,), F32), jnp.full((LANES - MLA_ROPE,), -0.5 * math.pi, F32)])[None, :]
    pos_lanes = jnp.broadcast_to(positions.reshape(T, 1).astype(jnp.int32), (T, LANES))
    cos_t, sin_t = _rope_tables(pos_lanes, inv_row, off_row)

    p_lb = jax.nn.softmax(hgrn_lb_logits.astype(F32), axis=0)
    lower_bounds = jnp.cumsum(p_lb, axis=0) - p_lb[0]
    row2 = lambda v: v.astype(F32)[None, :]
    tt = jnp.arange(SPAN)
    conv_shift = jnp.concatenate([(tt[None, :] == tt[:, None] - kk) for kk in range(GDN_CONV)], axis=1).astype(BF16)

    for l in range(DEPTH):
        w16, w32, w_gate = _pack_w_in(w_in[l])
        proj16, proj32 = _inproj(x2, w16, w32)
        wq, wqr, wk, wv = _pack_mla(mla_w_uq[l], mla_w_ukv[l])
        y_a = _mla(proj16, cos_t, sin_t, row2(mla_q_norm[l]), row2(mla_kv_norm[l]), wq, wqr, wk, wv, B, S)
        proj16_3 = proj16.reshape(B, S, PK16_WIDTH)
        proj32_3 = proj32.reshape(B, S, PK32_WIDTH)
        y_b, y_d = _recurrent(proj16_3, proj32_3, gdn_conv[l].astype(F32), conv_shift, _lane_row(gdn_a_log[l]),
                              _lane_row(gdn_dt_bias[l]), row2(gdn_norm[l]), row2(lower_bounds[l]), row2(hgrn_norm[l]))
        y_b = y_b.reshape(T, GDN_HEADS * GDN_DV)
        y_d = y_d.reshape(T, HGRN_HEADS * HGRN_DV)
        x2 = _merge(x2, y_a, y_b, proj16, y_d, pool_w[l].astype(BF16), row2(pool_scale[l]), w_gate,
                    w_branch[l].astype(BF16), w_out[l].astype(BF16), row2(ln_mix_g[l]), row2(ln_mix_b[l]), S)
        j = l // 2
        if l % 2 == 0:
            x2 = _ffn(x2, ffn_w_gate[j].astype(BF16), ffn_w_up[j].astype(BF16), ffn_w_down[j].astype(BF16),
                      row2(ln_ffn_g[l]), row2(ln_ffn_b[l]))
        else:
            x2 = _moe(x2, moe_router[j], moe_w_gate[j].astype(BF16), moe_w_up[j].astype(BF16),
                      moe_w_down[j].astype(BF16), row2(ln_ffn_g[l]), row2(ln_ffn_b[l]))
    return x2.reshape(B, S, D)
```
